```python
import jax, jax.numpy as jnp
from jax import lax
import numpy as np

D_MODEL = 1024
BATCH = 32
SEQ = 2048
DEPTH = 4
DEC_BATCH = 32
DEC_SEQ = 32
PAST_LEN = 2048

CHUNK = 64
N_META = 16
WINDOW = 128
WINDOW_CHUNKS = WINDOW // CHUNK
SWA_CACHE = min(WINDOW, PAST_LEN)
ATTN_HEADS = 8
ATTN_KV_HEADS = 2
HEAD_DIM = 64
GROUP = ATTN_HEADS // ATTN_KV_HEADS
ATTN_SCALE = HEAD_DIM ** -0.5
RET_HEADS = 8
RET_DK = 64
RET_DV = 64
ATTN_WIDTH = ATTN_HEADS * HEAD_DIM
KV_WIDTH = ATTN_KV_HEADS * HEAD_DIM
RET_QK_WIDTH = RET_HEADS * RET_DK
RET_V_WIDTH = RET_HEADS * RET_DV
MIX_WIDTH = ATTN_WIDTH + RET_V_WIDTH
IN_WIDTH = ATTN_WIDTH + 2 * KV_WIDTH + 2 * RET_QK_WIDTH + 2 * RET_V_WIDTH
SPLIT_POINTS = (ATTN_WIDTH,
                ATTN_WIDTH + KV_WIDTH,
                ATTN_WIDTH + 2 * KV_WIDTH,
                ATTN_WIDTH + 2 * KV_WIDTH + RET_QK_WIDTH,
                ATTN_WIDTH + 2 * KV_WIDTH + 2 * RET_QK_WIDTH,
                ATTN_WIDTH + 2 * KV_WIDTH + 2 * RET_QK_WIDTH + RET_V_WIDTH)
D_FF = 2816
ROPE_BASE = 10000.0
EPS = 1e-6
NEG = -1e30

kernel_name = 'hymba_swa_retention_macaron_stream_step'


def rmsnorm(x, g):
    x32 = x.astype(jnp.float32)
    y = x32 * lax.rsqrt(jnp.mean(x32 * x32, axis=-1, keepdims=True) + EPS)
    return y.astype(x.dtype) * g


def half_ffn(x, g_pre, w_in, w_out, g_post):
    gate, up = jnp.split(rmsnorm(x, g_pre) @ w_in, 2, axis=-1)
    return x + 0.5 * rmsnorm((jax.nn.silu(gate) * up) @ w_out, g_post)


def project(h, w_in):
    return jnp.split(h @ w_in, SPLIT_POINTS, axis=-1)


def sink_softmax(scores, sink):
    m = jnp.maximum(jnp.max(scores, axis=-1, keepdims=True), sink)
    p = jnp.exp(scores - m)
    return p / (jnp.sum(p, axis=-1, keepdims=True) + jnp.exp(sink - m))


def rotate_half(x, cos, sin, axis):
    x1, x2 = jnp.split(x, 2, axis=axis)
    return jnp.concatenate([x1 * cos - x2 * sin, x2 * cos + x1 * sin], axis=axis)


def swa_prompt(q, k, v, key_valid, sinks):
    B, Lp, _ = q.shape
    nC = Lp // CHUNK
    qb = q.reshape(B, nC, CHUNK, ATTN_KV_HEADS, GROUP, HEAD_DIM)
    k = k.reshape(B, Lp, ATTN_KV_HEADS, HEAD_DIM)
    v = v.reshape(B, Lp, ATTN_KV_HEADS, HEAD_DIM)
    lead = WINDOW_CHUNKS * CHUNK
    kp = jnp.pad(k, ((0, 0), (lead, 0), (0, 0), (0, 0)))
    vp = jnp.pad(v, ((0, 0), (lead, 0), (0, 0), (0, 0)))
    okp = jnp.pad(key_valid, (lead, 0))

    def band(a):
        return jnp.concatenate(
            [a[:, j * CHUNK: j * CHUNK + Lp].reshape((B, nC, CHUNK) + a.shape[2:])
             for j in range(WINDOW_CHUNKS + 1)], axis=2)

    kb, vb = band(kp), band(vp)
    okb = jnp.concatenate([okp[j * CHUNK: j * CHUNK + Lp].reshape(nC, CHUNK)
                           for j in range(WINDOW_CHUNKS + 1)], axis=1)
    s = jnp.einsum('bnqkgd,bnskd->bnkgqs', qb, kb).astype(jnp.float32) * ATTN_SCALE
    s = jnp.where(okb[None, :, None, None, None, :], s, NEG)
    sink = sinks.astype(jnp.float32).reshape(ATTN_KV_HEADS, GROUP)[None, None, :, :, None, None]
    p = sink_softmax(s, sink)
    o = jnp.einsum('bnkgqs,bnskd->bnqkgd', p.astype(v.dtype), vb)
    return o.reshape(B, Lp, ATTN_WIDTH)


def swa_sample(q, k, v, cache_k, cache_v, sinks):
    B, T, _ = q.shape
    qg = q.reshape(B, T, ATTN_KV_HEADS, GROUP, HEAD_DIM)
    kc = jnp.concatenate([cache_k, k.reshape(B, T, ATTN_KV_HEADS, HEAD_DIM)], axis=1)
    vc = jnp.concatenate([cache_v, v.reshape(B, T, ATTN_KV_HEADS, HEAD_DIM)], axis=1)
    s = jnp.einsum('btkgd,bskd->bkgts', qg, kc).astype(jnp.float32) * ATTN_SCALE
    sink = sinks.astype(jnp.float32).reshape(ATTN_KV_HEADS, GROUP)[None, :, :, None, None]
    p = sink_softmax(s, sink)
    o = jnp.einsum('bkgts,bskd->btkgd', p.astype(vc.dtype), vc)
    return o.reshape(B, T, ATTN_WIDTH), kc[:, -SWA_CACHE:], vc[:, -SWA_CACHE:]


def retention_blocks(q, k, v, s0):
    q, k, v = q.astype(jnp.float32), k.astype(jnp.float32), v.astype(jnp.float32)
    T = q.shape[2]
    log_g = jnp.log1p(-jnp.exp2(-5.0 - jnp.arange(RET_HEADS, dtype=jnp.float32)))
    freqs = ROPE_BASE ** (-jnp.arange(RET_DK // 2, dtype=jnp.float32) / (RET_DK // 2))
    pos = jnp.arange(T, dtype=jnp.float32)
    ang = pos[:, None] * freqs[None, :]
    cos, sin = jnp.cos(ang)[:, None, :], jnp.sin(ang)[:, None, :]
    qr = rotate_half(q, cos, sin, -1) * RET_DK ** -0.5
    kr = rotate_half(k, cos, sin, -1)
    diff = pos[:, None] - pos[None, :]
    intra_decay = jnp.where(diff >= 0, jnp.exp(log_g[:, None, None] * jnp.maximum(diff, 0.0)), 0.0)
    scores = jnp.einsum('bnihd,bnjhd->bnhij', qr, kr) * intra_decay
    intra = jnp.einsum('bnhij,bnjhe->bnihe', scores, v)
    kv_decay = jnp.exp((T - pos)[:, None] * log_g[None, :])
    w_blk = jnp.einsum('bnihd,bnihe->bnhde', kr * kv_decay[:, :, None], v)
    g_T = jnp.exp(T * log_g)[:, None, None]
    shift = -T * freqs
    cos_s, sin_s = jnp.cos(shift)[:, None], jnp.sin(shift)[:, None]

    def step(s, w):
        return rotate_half(g_T * s + w, cos_s, sin_s, -2), s

    s_final, s_before = lax.scan(step, s0.astype(jnp.float32), jnp.moveaxis(w_blk, 1, 0))
    s_before = jnp.moveaxis(s_before, 0, 1)
    q_decay = jnp.exp(pos[:, None] * log_g[None, :])[:, :, None]
    cross = jnp.einsum('bnihd,bnhde->bnihe', qr, s_before) * q_decay
    return intra + cross, s_final


def retention_out(o, gate):
    o = o * lax.rsqrt(jnp.mean(o * o, axis=-1, keepdims=True) + EPS)
    return o.reshape(o.shape[:2] + (RET_V_WIDTH,)).astype(gate.dtype) * jax.nn.silu(gate)


def mixer_prompt(h, w_in, sinks, key_valid):
    B, Lp, _ = h.shape
    nC = Lp // CHUNK
    qa, ka, va, qr, kr, vr, gr = project(h, w_in)
    attn = swa_prompt(qa, ka, va, key_valid, sinks)
    kr = kr * key_valid[None, :, None].astype(kr.dtype)
    s0 = jnp.zeros((B, RET_HEADS, RET_DK, RET_DV), jnp.float32)
    o, s_final = retention_blocks(qr.reshape(B, nC, CHUNK, RET_HEADS, RET_DK),
                                  kr.reshape(B, nC, CHUNK, RET_HEADS, RET_DK),
                                  vr.reshape(B, nC, CHUNK, RET_HEADS, RET_DV), s0)
    ret = retention_out(o.reshape(B, Lp, RET_HEADS, RET_DV), gr)
    new_k = ka[:, -SWA_CACHE:].reshape(B, SWA_CACHE, ATTN_KV_HEADS, HEAD_DIM)
    new_v = va[:, -SWA_CACHE:].reshape(B, SWA_CACHE, ATTN_KV_HEADS, HEAD_DIM)
    return jnp.concatenate([attn, ret], axis=-1), new_k, new_v, s_final


def mixer_sample(h, w_in, sinks, cache_k, cache_v, state):
    B, T, _ = h.shape
    qa, ka, va, qr, kr, vr, gr = project(h, w_in)
    attn, new_k, new_v = swa_sample(qa, ka, va, cache_k, cache_v, sinks)
    o, s_new = retention_blocks(qr.reshape(B, 1, T, RET_HEADS, RET_DK),
                                kr.reshape(B, 1, T, RET_HEADS, RET_DK),
                                vr.reshape(B, 1, T, RET_HEADS, RET_DV), state)
    ret = retention_out(o.reshape(B, T, RET_HEADS, RET_DV), gr)
    return jnp.concatenate([attn, ret], axis=-1), new_k, new_v, s_new


def setup_inputs(seed: int = 0) -> dict:
    key = jax.random.key(seed)
    ks = jax.random.split(key, 20)

    def nrm(k, shape, scale):
        return scale * jax.random.normal(k, shape, jnp.float32)

    def gain(k, shape):
        return 1.0 + 0.1 * jax.random.normal(k, shape, jnp.float32)

    return {
        'x_prompt': nrm(ks[0], (BATCH, SEQ, D_MODEL), 1.0),
        'x_sample': nrm(ks[1], (DEC_BATCH, DEC_SEQ, D_MODEL), 1.0),
        'cache_swa_k': nrm(ks[2], (DEPTH, DEC_BATCH, SWA_CACHE, ATTN_KV_HEADS, HEAD_DIM), 1.0),
        'cache_swa_v': nrm(ks[3], (DEPTH, DEC_BATCH, SWA_CACHE, ATTN_KV_HEADS, HEAD_DIM), 1.0),
        'state_ret': nrm(ks[4], (DEPTH, DEC_BATCH, RET_HEADS, RET_DK, RET_DV), 1.0),
        'meta_tokens': nrm(ks[5], (N_META, D_MODEL), 1.0),
        'w_in': nrm(ks[6], (DEPTH, D_MODEL, IN_WIDTH), D_MODEL ** -0.5),
        'w_out': nrm(ks[7], (DEPTH, MIX_WIDTH, D_MODEL), MIX_WIDTH ** -0.5),
        'attn_sinks': nrm(ks[8], (DEPTH, ATTN_HEADS), 0.5),
        'ffn1_w_in': nrm(ks[9], (DEPTH, D_MODEL, 2 * D_FF), D_MODEL ** -0.5),
        'ffn1_w_out': nrm(ks[10], (DEPTH, D_FF, D_MODEL), D_FF ** -0.5),
        'ffn2_w_in': nrm(ks[11], (DEPTH, D_MODEL, 2 * D_FF), D_MODEL ** -0.5),
        'ffn2_w_out': nrm(ks[12], (DEPTH, D_FF, D_MODEL), D_FF ** -0.5),
        'norm_ffn1_pre': gain(ks[13], (DEPTH, D_MODEL)),
        'norm_ffn1_post': gain(ks[14], (DEPTH, D_MODEL)),
        'norm_mix_pre': gain(ks[15], (DEPTH, D_MODEL)),
        'norm_mix_post': gain(ks[16], (DEPTH, D_MODEL)),
        'norm_ffn2_pre': gain(ks[17], (DEPTH, D_MODEL)),
        'norm_ffn2_post': gain(ks[18], (DEPTH, D_MODEL)),
        'final_norm': gain(ks[19], (D_MODEL,)),
    }


def reference(x_prompt, x_sample, cache_swa_k, cache_swa_v, state_ret, meta_tokens, w_in, w_out,
              attn_sinks, ffn1_w_in, ffn1_w_out, ffn2_w_in, ffn2_w_out, norm_ffn1_pre, norm_ffn1_post,
              norm_mix_pre, norm_mix_post, norm_ffn2_pre, norm_ffn2_post, final_norm):
    B, S, D = x_prompt.shape
    lead = CHUNK - N_META
    Lp = S + CHUNK
    xp = jnp.concatenate([jnp.zeros((B, lead, D), x_prompt.dtype),
                          jnp.broadcast_to(meta_tokens.astype(x_prompt.dtype), (B, N_META, D)),
                          x_prompt], axis=1)
    key_valid = jnp.arange(Lp) >= lead
    xs = x_sample
    pk, pv, ps, sk, sv, ss = [], [], [], [], [], []
    for l in range(DEPTH):
        xp = half_ffn(xp, norm_ffn1_pre[l], ffn1_w_in[l], ffn1_w_out[l], norm_ffn1_post[l])
        xs = half_ffn(xs, norm_ffn1_pre[l], ffn1_w_in[l], ffn1_w_out[l], norm_ffn1_post[l])

        mp, k_p, v_p, s_p = mixer_prompt(rmsnorm(xp, norm_mix_pre[l]), w_in[l], attn_sinks[l], key_valid)
        xp = xp + rmsnorm(mp @ w_out[l], norm_mix_post[l])
        ms, k_s, v_s, s_s = mixer_sample(rmsnorm(xs, norm_mix_pre[l]), w_in[l], attn_sinks[l],
                                         cache_swa_k[l], cache_swa_v[l], state_ret[l])
        xs = xs + rmsnorm(ms @ w_out[l], norm_mix_post[l])
        pk.append(k_p); pv.append(v_p); ps.append(s_p)
        sk.append(k_s); sv.append(v_s); ss.append(s_s)

        xp = half_ffn(xp, norm_ffn2_pre[l], ffn2_w_in[l], ffn2_w_out[l], norm_ffn2_post[l])
        xs = half_ffn(xs, norm_ffn2_pre[l], ffn2_w_in[l], ffn2_w_out[l], norm_ffn2_post[l])

    y_prompt = rmsnorm(xp[:, CHUNK:], final_norm)
    y_sample = rmsnorm(xs, final_norm)
    prompt_swa_k = jnp.stack(pk)
    prompt_swa_v = jnp.stack(pv)
    prompt_ret = jnp.stack(ps)
    sample_swa_k = jnp.stack(sk)
    sample_swa_v = jnp.stack(sv)
    sample_ret = jnp.stack(ss)
    return (y_prompt, y_sample, prompt_swa_k, prompt_swa_v, prompt_ret, sample_swa_k, sample_swa_v, sample_ret)
```

```python
import functools

import jax
import jax.numpy as jnp
from jax import lax
from jax.experimental import pallas as pl
from jax.experimental.pallas import tpu as pltpu

CHUNK = 64
N_META = 16
WINDOW_CHUNKS = 2
ATTN_HEADS = 8
ATTN_KV_HEADS = 2
HEAD_DIM = 64
GROUP = ATTN_HEADS // ATTN_KV_HEADS
ATTN_SCALE = HEAD_DIM ** -0.5
RET_HEADS = 8
RET_DK = 64
RET_DV = 64
ATTN_WIDTH = ATTN_HEADS * HEAD_DIM
KV_WIDTH = ATTN_KV_HEADS * HEAD_DIM
RET_WIDTH = RET_HEADS * RET_DK
ROPE_BASE = 10000.0
EPS = 1e-6
NEG = -1e30

OFF_QA = 0
OFF_KA = OFF_QA + ATTN_WIDTH
OFF_VA = OFF_KA + KV_WIDTH
OFF_QR = OFF_VA + KV_WIDTH
OFF_KR = OFF_QR + RET_WIDTH
OFF_VR = OFF_KR + RET_WIDTH
OFF_GR = OFF_VR + RET_WIDTH
IN_WIDTH = OFF_GR + RET_WIDTH
MIX_WIDTH = ATTN_WIDTH + RET_WIDTH

V7X_VMEM_LIMIT_BYTES = 56 * 1024 * 1024
FF_CHUNK = 256
MAX_ROW_TILE = 512
MAX_TILE_CHUNKS = 11

BF16 = jnp.bfloat16
F32 = jnp.float32


def _row_tile(n, cap=MAX_ROW_TILE):
    t = cap
    while t > 8 and n % t:
        t //= 2
    assert n % t == 0, (n, t)
    return t


def _largest_divisor(n, cap):
    return max(d for d in range(1, cap + 1) if n % d == 0)


def _rms(x):
    return x * lax.rsqrt(jnp.mean(x * x, axis=-1, keepdims=True) + EPS)


def _silu(x):
    return x / (1.0 + jnp.exp(-x))


def _compiler_params(n_axes):
    return pltpu.CompilerParams(dimension_semantics=("arbitrary",) * n_axes,
                                vmem_limit_bytes=V7X_VMEM_LIMIT_BYTES)


def _resident(shape):
    return pl.BlockSpec(shape, lambda *_: (0,) * len(shape), pipeline_mode=pl.Buffered(1))


def _ffn_kernel(x_ref, gpre_ref, win_ref, wout_ref, gpost_ref, o_ref, h_ref, *, d_ff, ff_chunk):
    x = x_ref[...]
    xn = (_rms(x) * gpre_ref[...]).astype(BF16)
    for j in range(d_ff // ff_chunk):
        lo = j * ff_chunk
        gate = jnp.dot(xn, win_ref[:, lo:lo + ff_chunk], preferred_element_type=F32)
        up = jnp.dot(xn, win_ref[:, d_ff + lo:d_ff + lo + ff_chunk], preferred_element_type=F32)
        h_ref[:, lo:lo + ff_chunk] = (_silu(gate) * up).astype(BF16)
    y = jnp.dot(h_ref[...], wout_ref[...], preferred_element_type=F32)
    o_ref[...] = x + 0.5 * (_rms(y) * gpost_ref[...])


def _ffn(x, g_pre, w_in, w_out, g_post):
    n, d = x.shape
    d_ff = w_out.shape[0]
    tm = _row_tile(n)
    ff_chunk = FF_CHUNK if d_ff % FF_CHUNK == 0 else d_ff
    row = pl.BlockSpec((tm, d), lambda i: (i, 0))
    return pl.pallas_call(
        functools.partial(_ffn_kernel, d_ff=d_ff, ff_chunk=ff_chunk),
        grid=(n // tm,),
        in_specs=[row, _resident((1, d)), _resident((d, 2 * d_ff)), _resident((d_ff, d)),
                  _resident((1, d))],
        out_specs=row,
        out_shape=jax.ShapeDtypeStruct((n, d), F32),
        scratch_shapes=[pltpu.VMEM((tm, d_ff), BF16)],
        compiler_params=_compiler_params(1),
        name="ffn",
    )(x, g_pre.reshape(1, d), w_in, w_out, g_post.reshape(1, d))


def _proj_kernel(x_ref, g_ref, w_ref, z_ref):
    xn = (_rms(x_ref[...]) * g_ref[...]).astype(BF16)
    z_ref[...] = jnp.dot(xn, w_ref[...], preferred_element_type=F32).astype(BF16)


def _proj(x, g, w):
    n, d = x.shape
    width = w.shape[1]
    tm = _row_tile(n)
    return pl.pallas_call(
        _proj_kernel,
        grid=(n // tm,),
        in_specs=[pl.BlockSpec((tm, d), lambda i: (i, 0)), _resident((1, d)), _resident((d, width))],
        out_specs=pl.BlockSpec((tm, width), lambda i: (i, 0)),
        out_shape=jax.ShapeDtypeStruct((n, width), BF16),
        compiler_params=_compiler_params(1),
        name="mix_proj",
    )(x, g.reshape(1, d), w)


def _out_kernel(x_ref, m_ref, w_ref, g_ref, o_ref):
    y = jnp.dot(m_ref[...], w_ref[...], preferred_element_type=F32)
    o_ref[...] = x_ref[...] + _rms(y) * g_ref[...]


def _out_proj(x, mix, w, g):
    n, d = x.shape
    width = mix.shape[1]
    tm = _row_tile(n)
    row = pl.BlockSpec((tm, d), lambda i: (i, 0))
    return pl.pallas_call(
        _out_kernel,
        grid=(n // tm,),
        in_specs=[row, pl.BlockSpec((tm, width), lambda i: (i, 0)), _resident((width, d)),
                  _resident((1, d))],
        out_specs=row,
        out_shape=jax.ShapeDtypeStruct((n, d), F32),
        compiler_params=_compiler_params(1),
        name="mix_out",
    )(x, mix, w, g.reshape(1, d))


def _norm_kernel(x_ref, g_ref, o_ref):
    o_ref[...] = _rms(x_ref[...]) * g_ref[...]


def _final_norm(x, g):
    n, d = x.shape
    tm = _row_tile(n)
    row = pl.BlockSpec((tm, d), lambda i: (i, 0))
    return pl.pallas_call(
        _norm_kernel,
        grid=(n // tm,),
        in_specs=[row, _resident((1, d))],
        out_specs=row,
        out_shape=jax.ShapeDtypeStruct((n, d), F32),
        compiler_params=_compiler_params(1),
        name="final_norm",
    )(x, g.reshape(1, d))


def _retention_tables(t):
    log_g = jnp.log1p(-jnp.exp2(-5.0 - jnp.arange(RET_HEADS, dtype=F32)))
    half = RET_DK // 2
    freqs = ROPE_BASE ** (-jnp.arange(half, dtype=F32) / half)
    pos = jnp.arange(t, dtype=F32)
    ang = pos[:, None] * freqs[None, :]
    cos, sin = jnp.cos(ang), jnp.sin(ang)
    cos_h = jnp.concatenate([cos, cos], axis=-1)
    sin_h = jnp.concatenate([-sin, sin], axis=-1)
    cos_t = jnp.tile(cos_h, (1, RET_HEADS))
    sin_t = jnp.tile(sin_h, (1, RET_HEADS))
    diff = pos[:, None] - pos[None, :]
    intra = jnp.where(diff >= 0, jnp.exp(log_g[:, None, None] * jnp.maximum(diff, 0.0)), 0.0)
    kv_decay = jnp.repeat(jnp.exp((t - pos)[:, None] * log_g[None, :]), RET_DK, axis=1)
    q_decay = jnp.repeat(jnp.exp(pos[:, None] * log_g[None, :]), RET_DV, axis=1)
    g_t = jnp.exp(t * log_g)
    shift = -t * freqs
    cs, sn = jnp.cos(shift), jnp.sin(shift)
    cos_s = jnp.broadcast_to(jnp.concatenate([cs, cs])[:, None], (RET_DK, RET_DV))
    sin_s = jnp.broadcast_to(jnp.concatenate([-sn, sn])[:, None], (RET_DK, RET_DV))
    qk_tab = jnp.stack([cos_t * RET_DK ** -0.5, sin_t * RET_DK ** -0.5, cos_t, sin_t, kv_decay, q_decay])
    s_tab = jnp.stack([cos_s, sin_s])
    return qk_tab, intra, s_tab, g_t


def _swap_half_lanes(x):
    w = x.shape[-1]
    fwd = pltpu.roll(x, RET_DK // 2, axis=1)
    bwd = pltpu.roll(x, w - RET_DK // 2, axis=1)
    lane = lax.broadcasted_iota(jnp.int32, x.shape, 1)
    return jnp.where((lane % RET_DK) < RET_DK // 2, bwd, fwd)


def _attention(q, kc, vc, valid, sinks_ref):
    t = q.shape[0]
    outs = []
    for kv in range(ATTN_KV_HEADS):
        heads = [kv * GROUP + g for g in range(GROUP)]
        q4 = jnp.concatenate([q[:, h * HEAD_DIM:(h + 1) * HEAD_DIM] for h in heads], axis=0)
        sink = jnp.concatenate([jnp.full((t, 1), sinks_ref[h], F32) for h in heads], axis=0)
        k = kc[:, kv * HEAD_DIM:(kv + 1) * HEAD_DIM]
        v = vc[:, kv * HEAD_DIM:(kv + 1) * HEAD_DIM]
        s = lax.dot_general(q4, k, (((1,), (1,)), ((), ())), preferred_element_type=F32) * ATTN_SCALE
        if valid is not None:
            s = jnp.where(valid, s, NEG)
        m = jnp.maximum(jnp.max(s, axis=-1, keepdims=True), sink)
        p = jnp.exp(s - m)
        den = jnp.sum(p, axis=-1, keepdims=True) + jnp.exp(sink - m)
        o = jnp.dot(p.astype(BF16), v, preferred_element_type=F32) / den
        outs.extend(o[g * t:(g + 1) * t] for g in range(GROUP))
    return jnp.concatenate(outs, axis=-1)


def _retention(qr, kr, vr, gate, qk_tab_ref, intra_ref, s_tab_ref, gt_ref, state_ref):
    q_rot = qr * qk_tab_ref[0] + _swap_half_lanes(qr) * qk_tab_ref[1]
    k_rot = kr * qk_tab_ref[2] + _swap_half_lanes(kr) * qk_tab_ref[3]
    k_dec = k_rot * qk_tab_ref[4]
    q_decay = qk_tab_ref[5]
    cos_s, sin_s = s_tab_ref[0], s_tab_ref[1]
    half = RET_DK // 2
    outs = []
    for h in range(RET_HEADS):
        sl = slice(h * RET_DK, (h + 1) * RET_DK)
        qh = q_rot[:, sl].astype(BF16)
        kh = k_rot[:, sl].astype(BF16)
        kdh = k_dec[:, sl].astype(BF16)
        vh = vr[:, sl].astype(BF16)
        s_prev = state_ref[h]
        scores = lax.dot_general(qh, kh, (((1,), (1,)), ((), ())), preferred_element_type=F32) * intra_ref[h]
        o = jnp.dot(scores.astype(BF16), vh, preferred_element_type=F32)
        o = o + jnp.dot(qh, s_prev.astype(BF16), preferred_element_type=F32) * q_decay[:, sl]
        w_blk = lax.dot_general(kdh, vh, (((0,), (0,)), ((), ())), preferred_element_type=F32)
        s_new = gt_ref[h] * s_prev + w_blk
        s_swapped = jnp.concatenate([s_new[half:], s_new[:half]], axis=0)
        state_ref[h] = s_new * cos_s + s_swapped * sin_s
        outs.append(_rms(o))
    return jnp.concatenate(outs, axis=-1) * _silu(gate)


def _prompt_mixer_kernel(sinks_ref, gt_ref, z_ref, qk_tab_ref, intra_ref, s_tab_ref,
                         mix_ref, klast_ref, vlast_ref, sfin_ref,
                         kctx_ref, vctx_ref, state_ref, *, tile_chunks, lead):
    g = pl.program_id(1)
    n_g = pl.num_programs(1)
    ctx = WINDOW_CHUNKS * CHUNK
    rows = tile_chunks * CHUNK

    @pl.when(g == 0)
    def _():
        kctx_ref[0:ctx, :] = jnp.zeros((ctx, KV_WIDTH), BF16)
        vctx_ref[0:ctx, :] = jnp.zeros((ctx, KV_WIDTH), BF16)
        state_ref[...] = jnp.zeros(state_ref.shape, F32)

    kctx_ref[ctx:ctx + rows, :] = z_ref[:, OFF_KA:OFF_KA + KV_WIDTH]
    vctx_ref[ctx:ctx + rows, :] = z_ref[:, OFF_VA:OFF_VA + KV_WIDTH]

    def chunk_body(c, carry):
        r0 = pl.multiple_of(c * CHUNK, CHUNK)
        first_row = (g * tile_chunks + c) * CHUNK
        zc = z_ref[pl.ds(r0, CHUNK), :]
        kc = kctx_ref[pl.ds(r0, ctx + CHUNK), :]
        vc = vctx_ref[pl.ds(r0, ctx + CHUNK), :]
        key_pos = lax.broadcasted_iota(jnp.int32, (1, ctx + CHUNK), 1) + (first_row - ctx)
        attn = _attention(zc[:, OFF_QA:OFF_QA + ATTN_WIDTH], kc, vc, key_pos >= lead, sinks_ref)
        row_pos = lax.broadcasted_iota(jnp.int32, (CHUNK, 1), 0) + first_row
        kr = jnp.where(row_pos >= lead, zc[:, OFF_KR:OFF_KR + RET_WIDTH].astype(F32), 0.0)
        ret = _retention(zc[:, OFF_QR:OFF_QR + RET_WIDTH].astype(F32), kr,
                         zc[:, OFF_VR:OFF_VR + RET_WIDTH].astype(F32),
                         zc[:, OFF_GR:OFF_GR + RET_WIDTH].astype(F32),
                         qk_tab_ref, intra_ref, s_tab_ref, gt_ref, state_ref)
        mix_ref[pl.ds(r0, CHUNK), :] = jnp.concatenate([attn, ret], axis=-1).astype(BF16)
        return carry

    lax.fori_loop(0, tile_chunks, chunk_body, 0)

    kctx_ref[0:ctx, :] = kctx_ref[rows:rows + ctx, :]
    vctx_ref[0:ctx, :] = vctx_ref[rows:rows + ctx, :]

    @pl.when(g == n_g - 1)
    def _():
        klast_ref[0] = kctx_ref[0:ctx, :].astype(F32)
        vlast_ref[0] = vctx_ref[0:ctx, :].astype(F32)
        sfin_ref[0] = state_ref[...]


def _prompt_mixer(z, sinks, tables, batch, seq_rows, lead):
    qk_tab, intra, s_tab, g_t = tables
    n_chunks = seq_rows // CHUNK
    tile_chunks = _largest_divisor(n_chunks, MAX_TILE_CHUNKS)
    n_g = n_chunks // tile_chunks
    rows = tile_chunks * CHUNK
    ctx = WINDOW_CHUNKS * CHUNK
    assert rows >= ctx
    smem = pl.BlockSpec(memory_space=pltpu.SMEM)
    state_shape = (RET_HEADS, RET_DK, RET_DV)
    return pl.pallas_call(
        functools.partial(_prompt_mixer_kernel, tile_chunks=tile_chunks, lead=lead),
        grid=(batch, n_g),
        in_specs=[smem, smem,
                  pl.BlockSpec((rows, IN_WIDTH), lambda b, g: (b * n_g + g, 0)),
                  _resident(qk_tab.shape), _resident(intra.shape), _resident(s_tab.shape)],
        out_specs=[pl.BlockSpec((rows, MIX_WIDTH), lambda b, g: (b * n_g + g, 0)),
                   pl.BlockSpec((1, ctx, KV_WIDTH), lambda b, g: (b, 0, 0)),
                   pl.BlockSpec((1, ctx, KV_WIDTH), lambda b, g: (b, 0, 0)),
                   pl.BlockSpec((1,) + state_shape, lambda b, g: (b, 0, 0, 0))],
        out_shape=[jax.ShapeDtypeStruct((batch * seq_rows, MIX_WIDTH), BF16),
                   jax.ShapeDtypeStruct((batch, ctx, KV_WIDTH), F32),
                   jax.ShapeDtypeStruct((batch, ctx, KV_WIDTH), F32),
                   jax.ShapeDtypeStruct((batch,) + state_shape, F32)],
        scratch_shapes=[pltpu.VMEM((ctx + rows, KV_WIDTH), BF16),
                        pltpu.VMEM((ctx + rows, KV_WIDTH), BF16),
                        pltpu.VMEM(state_shape, F32)],
        compiler_params=_compiler_params(2),
        name="prompt_mixer",
    )(sinks, g_t, z, qk_tab, intra, s_tab)


def _sample_mixer_kernel(sinks_ref, gt_ref, z_ref, ck_ref, cv_ref, s0_ref, qk_tab_ref, intra_ref, s_tab_ref,
                         mix_ref, knew_ref, vnew_ref, snew_ref, state_ref, *, cache_rows):
    z = z_ref[...]
    k_all = jnp.concatenate([ck_ref[0], z[:, OFF_KA:OFF_KA + KV_WIDTH].astype(F32)], axis=0)
    v_all = jnp.concatenate([cv_ref[0], z[:, OFF_VA:OFF_VA + KV_WIDTH].astype(F32)], axis=0)
    attn = _attention(z[:, OFF_QA:OFF_QA + ATTN_WIDTH], k_all.astype(BF16), v_all.astype(BF16), None, sinks_ref)
    state_ref[...] = s0_ref[0]
    ret = _retention(z[:, OFF_QR:OFF_QR + RET_WIDTH].astype(F32), z[:, OFF_KR:OFF_KR + RET_WIDTH].astype(F32),
                     z[:, OFF_VR:OFF_VR + RET_WIDTH].astype(F32), z[:, OFF_GR:OFF_GR + RET_WIDTH].astype(F32),
                     qk_tab_ref, intra_ref, s_tab_ref, gt_ref, state_ref)
    mix_ref[...] = jnp.concatenate([attn, ret], axis=-1).astype(BF16)
    total = k_all.shape[0]
    knew_ref[0] = k_all[total - cache_rows:]
    vnew_ref[0] = v_all[total - cache_rows:]
    snew_ref[0] = state_ref[...]


def _sample_mixer(z, sinks, tables, cache_k, cache_v, state, batch, t, row_offset):
    qk_tab, intra, s_tab, g_t = tables
    cache_rows = cache_k.shape[1]
    assert row_offset % t == 0
    first_block = row_offset // t
    smem = pl.BlockSpec(memory_space=pltpu.SMEM)
    state_shape = (RET_HEADS, RET_DK, RET_DV)
    cache_spec = pl.BlockSpec((1, cache_rows, KV_WIDTH), lambda b: (b, 0, 0))
    state_spec = pl.BlockSpec((1,) + state_shape, lambda b: (b, 0, 0, 0))
    return pl.pallas_call(
        functools.partial(_sample_mixer_kernel, cache_rows=cache_rows),
        grid=(batch,),
        in_specs=[smem, smem, pl.BlockSpec((t, IN_WIDTH), lambda b: (first_block + b, 0)),
                  cache_spec, cache_spec, state_spec,
                  _resident(qk_tab.shape), _resident(intra.shape), _resident(s_tab.shape)],
        out_specs=[pl.BlockSpec((t, MIX_WIDTH), lambda b: (b, 0)), cache_spec, cache_spec, state_spec],
        out_shape=[jax.ShapeDtypeStruct((batch * t, MIX_WIDTH), BF16),
                   jax.ShapeDtypeStruct(cache_k.shape, F32),
                   jax.ShapeDtypeStruct(cache_v.shape, F32),
                   jax.ShapeDtypeStruct(state.shape, F32)],
        scratch_shapes=[pltpu.VMEM(state_shape, F32)],
        compiler_params=_compiler_params(1),
        name="sample_mixer",
    )(sinks, g_t, z, cache_k, cache_v, state, qk_tab, intra, s_tab)


def kernel(x_prompt, x_sample, cache_swa_k, cache_swa_v, state_ret, meta_tokens, w_in, w_out, attn_sinks,
           ffn1_w_in, ffn1_w_out, ffn2_w_in, ffn2_w_out, norm_ffn1_pre, norm_ffn1_post, norm_mix_pre,
           norm_mix_post, norm_ffn2_pre, norm_ffn2_post, final_norm):
    batch, seq, d = x_prompt.shape
    dec_batch, dec_seq, _ = x_sample.shape
    depth = w_in.shape[0]
    cache_rows = cache_swa_k.shape[2]
    lead = CHUNK - N_META
    seq_rows = seq + CHUNK
    n_prompt = batch * seq_rows
    assert w_in.shape[2] == IN_WIDTH and w_out.shape[1] == MIX_WIDTH
    assert seq % CHUNK == 0 and cache_rows == WINDOW_CHUNKS * CHUNK and seq_rows >= cache_rows

    xp = jnp.concatenate([jnp.zeros((batch, lead, d), x_prompt.dtype),
                          jnp.broadcast_to(meta_tokens.astype(x_prompt.dtype), (batch, N_META, d)),
                          x_prompt], axis=1)
    x = jnp.concatenate([xp.reshape(n_prompt, d), x_sample.reshape(dec_batch * dec_seq, d)], axis=0)

    tables_p = _retention_tables(CHUNK)
    tables_s = _retention_tables(dec_seq)
    cache_k = cache_swa_k.reshape(depth, dec_batch, cache_rows, KV_WIDTH)
    cache_v = cache_swa_v.reshape(depth, dec_batch, cache_rows, KV_WIDTH)

    pk, pv, ps, sk, sv, ss = [], [], [], [], [], []
    for l in range(depth):
        x = _ffn(x, norm_ffn1_pre[l], ffn1_w_in[l].astype(BF16), ffn1_w_out[l].astype(BF16), norm_ffn1_post[l])
        z = _proj(x, norm_mix_pre[l], w_in[l].astype(BF16))
        mix_p, k_p, v_p, s_p = _prompt_mixer(z, attn_sinks[l], tables_p, batch, seq_rows, lead)
        mix_s, k_s, v_s, s_s = _sample_mixer(z, attn_sinks[l], tables_s, cache_k[l], cache_v[l],
                                             state_ret[l], dec_batch, dec_seq, n_prompt)
        x = _out_proj(x, jnp.concatenate([mix_p, mix_s], axis=0), w_out[l].astype(BF16), norm_mix_post[l])
        x = _ffn(x, norm_ffn2_pre[l], ffn2_w_in[l].astype(BF16), ffn2_w_out[l].astype(BF16), norm_ffn2_post[l])
        pk.append(k_p); pv.append(v_p); ps.append(s_p)
        sk.append(k_s); sv.append(v_s); ss.append(s_s)

    y = _final_norm(x, final_norm)
    y_prompt = y[:n_prompt].reshape(batch, seq_rows, d)[:, CHUNK:]
    y_sample = y[n_prompt:].reshape(dec_batch, dec_seq, d)
    kv_shape_p = (depth, batch, cache_rows, ATTN_KV_HEADS, HEAD_DIM)
    kv_shape_s = (depth, dec_batch, cache_rows, ATTN_KV_HEADS, HEAD_DIM)
    return (y_prompt, y_sample,
            jnp.stack(pk).reshape(kv_shape_p), jnp.stack(pv).reshape(kv_shape_p), jnp.stack(ps),
            jnp.stack(sk).reshape(kv_shape_s), jnp.stack(sv).reshape(kv_shape_s), jnp.stack(ss))
```

```python
import functools
import math

import jax
import jax.numpy as jnp
import numpy as np
from jax import lax
from jax.experimental import pallas as pl
from jax.experimental.pallas import tpu as pltpu

CHUNK = 64
N_META = 16
WINDOW_CHUNKS = 2
ATTN_HEADS = 8
ATTN_KV_HEADS = 2
HEAD_DIM = 64
GROUP = ATTN_HEADS // ATTN_KV_HEADS
ATTN_SCALE = HEAD_DIM ** -0.5
RET_HEADS = 8
RET_DK = 64
RET_DV = 64
ATTN_WIDTH = ATTN_HEADS * HEAD_DIM
KV_WIDTH = ATTN_KV_HEADS * HEAD_DIM
RET_WIDTH = RET_HEADS * RET_DK
ROPE_BASE = 10000.0
EPS = 1e-6
NEG = -1e30

OFF_QA = 0
OFF_KA = OFF_QA + ATTN_WIDTH
OFF_VA = OFF_KA + KV_WIDTH
OFF_QR = OFF_VA + KV_WIDTH
OFF_KR = OFF_QR + RET_WIDTH
OFF_VR = OFF_KR + RET_WIDTH
OFF_GR = OFF_VR + RET_WIDTH
IN_WIDTH = OFF_GR + RET_WIDTH
MIX_WIDTH = ATTN_WIDTH + RET_WIDTH

LANES = 128
HALF = RET_DK // 2
RG = 4
RGW = RG * RET_DK
N_RG = RET_HEADS // RG
KEY_PAD = 256
V7X_VMEM_LIMIT_BYTES = 56 * 1024 * 1024
FF_CHUNK = 256
MAX_ROW_TILE = 512
MAX_TILE_CHUNKS = 8
CHUNK_UNROLL = 4

BF16 = jnp.bfloat16
F32 = jnp.float32

assert RGW == 2 * LANES and KV_WIDTH == LANES and GROUP == 4 and ATTN_KV_HEADS == 2


def _row_tile(n, cap=MAX_ROW_TILE):
    t = cap
    while t > 8 and n % t:
        t //= 2
    assert n % t == 0, (n, t)
    return t


def _largest_divisor(n, cap):
    return max(d for d in range(1, cap + 1) if n % d == 0)


def _rms(x):
    return x * lax.rsqrt(jnp.mean(x * x, axis=-1, keepdims=True) + EPS)


def _silu(x):
    return x / (1.0 + jnp.exp(-x))


def _compiler_params(n_axes):
    return pltpu.CompilerParams(dimension_semantics=("arbitrary",) * n_axes,
                                vmem_limit_bytes=V7X_VMEM_LIMIT_BYTES)


def _resident(shape):
    return pl.BlockSpec(shape, lambda *_: (0,) * len(shape), pipeline_mode=pl.Buffered(1))


def _token_kernel(*refs, d_ff, ff_chunk, n_main_tiles, n_meta_tiles, split_in, has_mix, has_proj, final):
    refs = list(refs)
    take = lambda n: [refs.pop(0) for _ in range(n)]
    i = pl.program_id(0)
    if split_in:
        xm_ref, xt_ref = take(2)
        x = jnp.where(i < n_main_tiles, xm_ref[...], xt_ref[...])
    else:
        (x_ref,) = take(1)
        x = x_ref[...]
    if has_mix:
        mm_ref, mt_ref, ms_ref, wmix_ref, gmix_ref = take(5)
        mix = jnp.where(i < n_main_tiles, mm_ref[...],
                        jnp.where(i < n_main_tiles + n_meta_tiles, mt_ref[...], ms_ref[...]))
        x = x + _rms(jnp.dot(mix, wmix_ref[...], preferred_element_type=F32)) * gmix_ref[...]
    gpre_ref, win_ref, wout_ref, gpost_ref = take(4)
    if has_proj:
        gproj_ref, wproj_ref = take(2)
    if final:
        (gfin_ref,) = take(1)
        ym_ref, yt_ref = take(2)
    else:
        (o_ref,) = take(1)
    if has_proj:
        (z_ref,) = take(1)
    (h_ref,) = take(1)
    assert not refs

    xn = (_rms(x) * gpre_ref[...]).astype(BF16)
    for j in range(d_ff // ff_chunk):
        lo = j * ff_chunk
        gate = jnp.dot(xn, win_ref[:, lo:lo + ff_chunk], preferred_element_type=F32)
        up = jnp.dot(xn, win_ref[:, d_ff + lo:d_ff + lo + ff_chunk], preferred_element_type=F32)
        h_ref[:, lo:lo + ff_chunk] = (_silu(gate) * up).astype(BF16)
    y = jnp.dot(h_ref[...], wout_ref[...], preferred_element_type=F32)
    x = x + 0.5 * (_rms(y) * gpost_ref[...])
    if has_proj:
        xn = (_rms(x) * gproj_ref[...]).astype(BF16)
        z_ref[...] = jnp.dot(xn, wproj_ref[...], preferred_element_type=F32).astype(BF16)
    if final:
        out = _rms(x) * gfin_ref[...]

        @pl.when(i < n_main_tiles)
        def _():
            ym_ref[...] = out

        @pl.when(i >= n_main_tiles)
        def _():
            yt_ref[...] = out
    else:
        o_ref[...] = x


def _token_step(x, ffn, rows, *, mix=None, proj=None, final=None):
    n_main, n_meta, n_sample = rows
    n = n_main + n_meta + n_sample
    split_in = isinstance(x, tuple)
    d = x[0].shape[1] if split_in else x.shape[1]
    g_pre, w_in, w_out, g_post = ffn
    d_ff = w_out.shape[0]
    tm = _row_tile(math.gcd(math.gcd(n_main, n_meta), n_sample))
    n_main_tiles, n_meta_tiles = n_main // tm, n_meta // tm
    ff_chunk = FF_CHUNK if d_ff % FF_CHUNK == 0 else d_ff

    row = lambda w: pl.BlockSpec((tm, w), lambda i: (i, 0))
    main_rows = lambda w: pl.BlockSpec((tm, w), lambda i: (jnp.minimum(i, n_main_tiles - 1), 0))
    tail_rows = lambda w: pl.BlockSpec((tm, w), lambda i: (jnp.maximum(i - n_main_tiles, 0), 0))
    gain = lambda g: g.reshape(1, d)

    args, in_specs = [], []
    if split_in:
        args += list(x)
        in_specs += [main_rows(d), tail_rows(d)]
    else:
        args.append(x)
        in_specs.append(row(d))
    if mix is not None:
        mix_main, mix_meta, mix_sample, w_mix, g_mix = mix
        args += [mix_main, mix_meta, mix_sample, w_mix, gain(g_mix)]
        in_specs += [
            main_rows(MIX_WIDTH),
            pl.BlockSpec((tm, MIX_WIDTH), lambda i: (jnp.clip(i - n_main_tiles, 0, n_meta_tiles - 1), 0)),
            pl.BlockSpec((tm, MIX_WIDTH), lambda i: (jnp.maximum(i - n_main_tiles - n_meta_tiles, 0), 0)),
            _resident(w_mix.shape), _resident((1, d))]
    args += [gain(g_pre), w_in, w_out, gain(g_post)]
    in_specs += [_resident((1, d)), _resident(w_in.shape), _resident(w_out.shape), _resident((1, d))]
    if proj is not None:
        args += [gain(proj[0]), proj[1]]
        in_specs += [_resident((1, d)), _resident(proj[1].shape)]
    out_shape, out_specs = [], []
    if final is not None:
        args.append(gain(final))
        in_specs.append(_resident((1, d)))
        out_shape += [jax.ShapeDtypeStruct((n_main, d), F32), jax.ShapeDtypeStruct((n_meta + n_sample, d), F32)]
        out_specs += [main_rows(d), tail_rows(d)]
    else:
        out_shape.append(jax.ShapeDtypeStruct((n, d), F32))
        out_specs.append(row(d))
    if proj is not None:
        out_shape.append(jax.ShapeDtypeStruct((n, IN_WIDTH), BF16))
        out_specs.append(row(IN_WIDTH))

    outs = pl.pallas_call(
        functools.partial(_token_kernel, d_ff=d_ff, ff_chunk=ff_chunk, n_main_tiles=n_main_tiles,
                          n_meta_tiles=n_meta_tiles, split_in=split_in, has_mix=mix is not None,
                          has_proj=proj is not None, final=final is not None),
        grid=(n // tm,),
        in_specs=in_specs,
        out_specs=out_specs,
        out_shape=out_shape,
        scratch_shapes=[pltpu.VMEM((tm, d_ff), BF16)],
        compiler_params=_compiler_params(1),
        name="token_step",
    )(*args)
    return outs[0] if len(outs) == 1 else tuple(outs)


def _permute_ret_columns(w):
    d = w.shape[0]
    return w.reshape(d, N_RG, RG, 2, HALF).transpose(0, 1, 3, 2, 4).reshape(d, RET_WIDTH)


def _projection_weight(w):
    parts = [w[:, :OFF_QR], _permute_ret_columns(w[:, OFF_QR:OFF_KR]), _permute_ret_columns(w[:, OFF_KR:OFF_VR]),
             w[:, OFF_VR:]]
    return jnp.concatenate(parts, axis=1).astype(BF16)


def _mixer_tables(t, s_keys):
    log_g = jnp.log1p(-jnp.exp2(-5.0 - jnp.arange(RET_HEADS, dtype=F32)))
    freqs = ROPE_BASE ** (-jnp.arange(HALF, dtype=F32) / HALF)
    pos = jnp.arange(t, dtype=F32)
    ang = pos[:, None] * freqs[None, :]
    cos, sin = jnp.tile(jnp.cos(ang), (1, RG)), jnp.tile(jnp.sin(ang), (1, RG))
    qs = RET_DK ** -0.5
    rot = jnp.stack([cos * qs, sin * qs, cos, sin])

    head = lambda g: slice(g * RG, (g + 1) * RG)
    diff = pos[:, None] - pos[None, :]
    intra = jnp.where(diff >= 0, jnp.exp(log_g[:, None, None] * jnp.maximum(diff, 0.0)), 0.0)
    kv_dec = jnp.exp((t - pos)[:, None] * log_g[None, :])
    q_dec = jnp.exp(pos[:, None] * log_g[None, :])
    dec = []
    for g in range(N_RG):
        kd = jnp.tile(jnp.repeat(kv_dec[:, head(g)], HALF, axis=1), (1, 2))
        qd = jnp.repeat(q_dec[:, head(g)], RET_DV, axis=1)
        dec += [kd, qd]
    dec = jnp.stack(dec)
    dtab = jnp.stack([jnp.concatenate(list(intra[head(g)]), axis=1) for g in range(N_RG)])

    g_t = jnp.exp(t * log_g)
    shift = -t * freqs
    stab = [jnp.broadcast_to(jnp.repeat(g_t[head(g)], RET_DV)[None, :], (HALF, RGW)) for g in range(N_RG)]
    stab += [jnp.broadcast_to(jnp.cos(shift)[:, None], (HALF, RGW)),
             jnp.broadcast_to(jnp.sin(shift)[:, None], (HALF, RGW))]
    stab = jnp.stack(stab)

    r4 = np.arange(RG * t)[:, None] // t
    lane = np.arange(RGW)[None, :]
    mask_k = jnp.asarray(r4 == (lane % LANES) // HALF, BF16)
    mask_v = jnp.asarray(r4 == lane // RET_DV, BF16)
    rs = np.arange(LANES)[:, None] // HALF
    mask_s = jnp.asarray(rs == lane // RET_DV, F32)
    seg = jnp.asarray(np.arange(RGW)[:, None] // RET_DV == lane // RET_DV, BF16)

    rowk = np.arange(KEY_PAD)[:, None] < s_keys
    lane_v = np.arange(LANES)[None, :] // HEAD_DIM
    v_keep = jnp.stack([jnp.asarray(rowk & (lane_v == p), BF16) for p in range(2)])
    v_ones = jnp.stack([jnp.asarray(np.broadcast_to(lane_v != p, (KEY_PAD, LANES)), BF16) for p in range(2)])
    return dict(rot=rot, dec=dec, dtab=dtab, stab=stab, mask_k=mask_k, mask_v=mask_v, mask_s=mask_s, seg=seg,
                v_keep=v_keep, v_ones=v_ones)


_TABLE_ORDER = ("rot", "dec", "dtab", "stab", "mask_k", "mask_v", "mask_s", "seg", "v_keep", "v_ones")


def _fill_table(sinks_ref, fill_ref, t, s_keys):
    col = lax.broadcasted_iota(jnp.int32, (t, KEY_PAD), 1)
    for kv in range(ATTN_KV_HEADS):
        for par in range(2):
            for j in range(2):
                h = kv * GROUP + par + 2 * j
                fill_ref[2 * kv + par, j * t:(j + 1) * t, :] = jnp.where(col == s_keys, sinks_ref[h], NEG)


def _swap_kv_halves(x):
    return pltpu.roll(x.astype(F32), HEAD_DIM, axis=1).astype(BF16)


def _attention(q, kwin, kswin, vwin, vswin, valid, fill_ref, tb):
    t = q.shape[0]
    lane = lax.broadcasted_iota(jnp.int32, (t, LANES), 1)
    lower = lane < HEAD_DIM
    blocks = [q[:, j * LANES:(j + 1) * LANES].astype(F32) * ATTN_SCALE for j in range(ATTN_HEADS // 2)]
    outs = {}
    for kv in range(ATTN_KV_HEADS):
        for par in range(2):
            keep = lower if par == 0 else ~lower
            qs = jnp.concatenate([jnp.where(keep, blocks[2 * kv + j], 0.0) for j in range(2)], axis=0).astype(BF16)
            ksrc, vsrc = (kwin, vwin) if par == kv else (kswin, vswin)
            s = lax.dot_general(qs, ksrc, (((1,), (1,)), ((), ())), preferred_element_type=F32)
            s = jnp.where(valid, s, fill_ref[2 * kv + par])
            p = jnp.exp(s - jnp.max(s, axis=-1, keepdims=True))
            vext = vsrc * tb["v_keep"][par] + tb["v_ones"][par]
            res = jnp.dot(p.astype(BF16), vext, preferred_element_type=F32)
            outs[kv, par] = res / pltpu.roll(res, HEAD_DIM, axis=1)
    cols = []
    for j in range(ATTN_HEADS // 2):
        kv, r = j // 2, (j % 2) * t
        cols.append(jnp.where(lower, outs[kv, 0][r:r + t], outs[kv, 1][r:r + t]))
    return jnp.concatenate(cols, axis=-1)


def _retention(z, state_ref, tb, row_keep):
    t = z.shape[0]
    cq, sq, ck, sk = tb["rot"][0], tb["rot"][1], tb["rot"][2], tb["rot"][3]
    cos_s, sin_s = tb["stab"][N_RG], tb["stab"][N_RG + 1]
    outs = []
    for g in range(N_RG):
        lo = g * RGW
        q1 = z[:, OFF_QR + lo:OFF_QR + lo + LANES].astype(F32)
        q2 = z[:, OFF_QR + lo + LANES:OFF_QR + lo + RGW].astype(F32)
        k1 = z[:, OFF_KR + lo:OFF_KR + lo + LANES].astype(F32)
        k2 = z[:, OFF_KR + lo + LANES:OFF_KR + lo + RGW].astype(F32)
        if row_keep is not None:
            k1, k2 = k1 * row_keep, k2 * row_keep
        q_rot = jnp.concatenate([q1 * cq - q2 * sq, q2 * cq + q1 * sq], axis=-1)
        k_rot = jnp.concatenate([k1 * ck - k2 * sk, k2 * ck + k1 * sk], axis=-1)
        qb, kb = q_rot.astype(BF16), k_rot.astype(BF16)
        kdb = (k_rot * tb["dec"][2 * g]).astype(BF16)
        vb = z[:, OFF_VR + lo:OFF_VR + lo + RGW]
        gate = z[:, OFF_GR + lo:OFF_GR + lo + RGW].astype(F32)

        k_bd = jnp.concatenate([kb] * RG, axis=0) * tb["mask_k"][...]
        v_bd = jnp.concatenate([vb] * RG, axis=0) * tb["mask_v"][...]
        scores = lax.dot_general(qb, k_bd, (((1,), (1,)), ((), ())), preferred_element_type=F32)
        o = jnp.dot((scores * tb["dtab"][g]).astype(BF16), v_bd, preferred_element_type=F32)

        mask_s = tb["mask_s"][...]
        c1, c2 = state_ref[g, 0:HALF, :], state_ref[g, HALF:2 * HALF, :]
        s_bd = jnp.concatenate([jnp.concatenate([c] * RG, axis=0) * mask_s for c in (c1, c2)], axis=0)
        o = o + jnp.dot(qb, s_bd.astype(BF16), preferred_element_type=F32) * tb["dec"][2 * g + 1]

        w_full = lax.dot_general(kdb, vb, (((0,), (0,)), ((), ())), preferred_element_type=F32)
        w = []
        for half in range(2):
            wm = w_full[half * LANES:(half + 1) * LANES] * mask_s
            w.append(wm[0:HALF] + wm[HALF:2 * HALF] + wm[2 * HALF:3 * HALF] + wm[3 * HALF:4 * HALF])
        a1 = tb["stab"][g] * c1 + w[0]
        a2 = tb["stab"][g] * c2 + w[1]
        state_ref[g, 0:HALF, :] = a1 * cos_s - a2 * sin_s
        state_ref[g, HALF:2 * HALF, :] = a2 * cos_s + a1 * sin_s

        sq_hi = (o * o).astype(BF16)
        sq_lo = (o * o - sq_hi.astype(F32)).astype(BF16)
        seg = tb["seg"][...]
        ssq = jnp.dot(sq_hi, seg, preferred_element_type=F32) + jnp.dot(sq_lo, seg, preferred_element_type=F32)
        outs.append(o * lax.rsqrt(ssq * (1.0 / RET_DV) + EPS) * _silu(gate))
    return jnp.concatenate(outs, axis=-1)


def _mixer_block(z, windows, valid, fill_ref, state_ref, tb, row_keep=None):
    attn = _attention(z[:, OFF_QA:OFF_QA + ATTN_WIDTH], *windows, valid, fill_ref, tb)
    ret = _retention(z, state_ref, tb, row_keep)
    return jnp.concatenate([attn, ret], axis=-1).astype(BF16)


def _state_to_heads(state_ref, out_ref):
    for g in range(N_RG):
        for hh in range(RG):
            out_ref[0, g * RG + hh] = state_ref[g, :, hh * RET_DV:(hh + 1) * RET_DV]


def _prompt_mixer_kernel(sinks_ref, z_ref, zmeta_ref, *rest, tile_chunks, lead):
    tb = dict(zip(_TABLE_ORDER, rest[:len(_TABLE_ORDER)]))
    (mix_ref, mixmeta_ref, klast_ref, vlast_ref, sfin_ref,
     kctx_ref, ksw_ref, vctx_ref, vsw_ref, fill_ref, state_ref) = rest[len(_TABLE_ORDER):]
    g = pl.program_id(1)
    n_g = pl.num_programs(1)
    ctx = WINDOW_CHUNKS * CHUNK
    rows = tile_chunks * CHUNK
    s_keys = ctx + CHUNK
    col = lax.broadcasted_iota(jnp.int32, (1, KEY_PAD), 1)

    def put_keys(dst, z, n):
        k, v = z[:, OFF_KA:OFF_KA + KV_WIDTH], z[:, OFF_VA:OFF_VA + KV_WIDTH]
        kctx_ref[dst:dst + n, :] = k
        vctx_ref[dst:dst + n, :] = v
        ksw_ref[dst:dst + n, :] = _swap_kv_halves(k)
        vsw_ref[dst:dst + n, :] = _swap_kv_halves(v)

    def block(z, r0, first_row, row_keep=None):
        windows = [r[pl.ds(r0, KEY_PAD), :] for r in (kctx_ref, ksw_ref, vctx_ref, vsw_ref)]
        valid = (col < s_keys) & (col + (first_row - ctx) >= lead)
        return _mixer_block(z, windows, valid, fill_ref, state_ref, tb, row_keep)

    @pl.when(g == 0)
    def _():
        for r in (kctx_ref, ksw_ref, vctx_ref, vsw_ref):
            r[...] = jnp.zeros(r.shape, BF16)
        state_ref[...] = jnp.zeros(state_ref.shape, F32)
        _fill_table(sinks_ref, fill_ref, CHUNK, s_keys)
        zm = zmeta_ref[...]
        put_keys(ctx, zm, CHUNK)
        row_keep = (lax.broadcasted_iota(jnp.int32, (CHUNK, 1), 0) >= lead).astype(F32)
        mixmeta_ref[...] = block(zm, 0, 0, row_keep)
        put_keys(ctx - CHUNK, zm, CHUNK)

    put_keys(ctx, z_ref[...], rows)

    def chunk_body(c, carry):
        r0 = pl.multiple_of(c * CHUNK, CHUNK)
        first_row = (1 + g * tile_chunks + c) * CHUNK
        mix_ref[pl.ds(r0, CHUNK), :] = block(z_ref[pl.ds(r0, CHUNK), :], r0, first_row)
        return carry

    lax.fori_loop(0, tile_chunks, chunk_body, 0, unroll=CHUNK_UNROLL if tile_chunks % CHUNK_UNROLL == 0 else 1)

    for r in (kctx_ref, ksw_ref, vctx_ref, vsw_ref):
        r[0:ctx, :] = r[rows:rows + ctx, :]

    @pl.when(g == n_g - 1)
    def _():
        klast_ref[0] = kctx_ref[0:ctx, :].astype(F32)
        vlast_ref[0] = vctx_ref[0:ctx, :].astype(F32)
        _state_to_heads(state_ref, sfin_ref)


def _prompt_mixer(z, sinks, tables, batch, seq, n_main, lead):
    n_chunks = seq // CHUNK
    tile_chunks = _largest_divisor(n_chunks, MAX_TILE_CHUNKS)
    n_g = n_chunks // tile_chunks
    rows = tile_chunks * CHUNK
    ctx = WINDOW_CHUNKS * CHUNK
    assert rows >= ctx and n_main % CHUNK == 0
    meta_block0 = n_main // CHUNK
    tabs = [tables[k] for k in _TABLE_ORDER]
    state_shape = (RET_HEADS, RET_DK, RET_DV)
    ctx_rows = ctx + rows + KEY_PAD - (ctx + CHUNK)
    ctx_buf = pltpu.VMEM((ctx_rows, KV_WIDTH), BF16)
    return pl.pallas_call(
        functools.partial(_prompt_mixer_kernel, tile_chunks=tile_chunks, lead=lead),
        grid=(batch, n_g),
        in_specs=[pl.BlockSpec(memory_space=pltpu.SMEM),
                  pl.BlockSpec((rows, IN_WIDTH), lambda b, g: (b * n_g + g, 0)),
                  pl.BlockSpec((CHUNK, IN_WIDTH), lambda b, g: (meta_block0 + b, 0))]
                 + [_resident(t.shape) for t in tabs],
        out_specs=[pl.BlockSpec((rows, MIX_WIDTH), lambda b, g: (b * n_g + g, 0)),
                   pl.BlockSpec((CHUNK, MIX_WIDTH), lambda b, g: (b, 0)),
                   pl.BlockSpec((1, ctx, KV_WIDTH), lambda b, g: (b, 0, 0)),
                   pl.BlockSpec((1, ctx, KV_WIDTH), lambda b, g: (b, 0, 0)),
                   pl.BlockSpec((1,) + state_shape, lambda b, g: (b, 0, 0, 0))],
        out_shape=[jax.ShapeDtypeStruct((n_main, MIX_WIDTH), BF16),
                   jax.ShapeDtypeStruct((batch * CHUNK, MIX_WIDTH), BF16),
                   jax.ShapeDtypeStruct((batch, ctx, KV_WIDTH), F32),
                   jax.ShapeDtypeStruct((batch, ctx, KV_WIDTH), F32),
                   jax.ShapeDtypeStruct((batch,) + state_shape, F32)],
        scratch_shapes=[ctx_buf, ctx_buf, ctx_buf, ctx_buf,
                        pltpu.VMEM((2 * ATTN_KV_HEADS, 2 * CHUNK, KEY_PAD), F32),
                        pltpu.VMEM((N_RG, RET_DK, RGW), F32)],
        compiler_params=_compiler_params(2),
        name="prompt_mixer",
    )(sinks, z, z, *tabs)


def _sample_mixer_kernel(sinks_ref, z_ref, ck_ref, cv_ref, s0_ref, *rest, cache_rows):
    tb = dict(zip(_TABLE_ORDER, rest[:len(_TABLE_ORDER)]))
    mix_ref, knew_ref, vnew_ref, snew_ref, fill_ref, state_ref = rest[len(_TABLE_ORDER):]
    z = z_ref[...]
    t = z.shape[0]
    s_keys = cache_rows + t
    pad = jnp.zeros((KEY_PAD - s_keys, KV_WIDTH), F32)
    k_all = jnp.concatenate([ck_ref[0], z[:, OFF_KA:OFF_KA + KV_WIDTH].astype(F32)], axis=0)
    v_all = jnp.concatenate([cv_ref[0], z[:, OFF_VA:OFF_VA + KV_WIDTH].astype(F32)], axis=0)
    windows = []
    for a in (k_all, v_all):
        w = jnp.concatenate([a, pad], axis=0)
        windows += [w.astype(BF16), pltpu.roll(w, HEAD_DIM, axis=1).astype(BF16)]
    _fill_table(sinks_ref, fill_ref, t, s_keys)
    for g in range(N_RG):
        for hh in range(RG):
            state_ref[g, :, hh * RET_DV:(hh + 1) * RET_DV] = s0_ref[0, g * RG + hh]
    col = lax.broadcasted_iota(jnp.int32, (1, KEY_PAD), 1)
    mix_ref[...] = _mixer_block(z, windows, col < s_keys, fill_ref, state_ref, tb)
    knew_ref[0] = k_all[s_keys - cache_rows:]
    vnew_ref[0] = v_all[s_keys - cache_rows:]
    _state_to_heads(state_ref, snew_ref)


def _sample_mixer(z, sinks, tables, cache_k, cache_v, state, batch, t, row_offset):
    cache_rows = cache_k.shape[1]
    assert row_offset % t == 0 and cache_rows + t < KEY_PAD
    first_block = row_offset // t
    tabs = [tables[k] for k in _TABLE_ORDER]
    state_shape = (RET_HEADS, RET_DK, RET_DV)
    cache_spec = pl.BlockSpec((1, cache_rows, KV_WIDTH), lambda b: (b, 0, 0))
    state_spec = pl.BlockSpec((1,) + state_shape, lambda b: (b, 0, 0, 0))
    return pl.pallas_call(
        functools.partial(_sample_mixer_kernel, cache_rows=cache_rows),
        grid=(batch,),
        in_specs=[pl.BlockSpec(memory_space=pltpu.SMEM),
                  pl.BlockSpec((t, IN_WIDTH), lambda b: (first_block + b, 0)),
                  cache_spec, cache_spec, state_spec] + [_resident(x.shape) for x in tabs],
        out_specs=[pl.BlockSpec((t, MIX_WIDTH), lambda b: (b, 0)), cache_spec, cache_spec, state_spec],
        out_shape=[jax.ShapeDtypeStruct((batch * t, MIX_WIDTH), BF16),
                   jax.ShapeDtypeStruct(cache_k.shape, F32),
                   jax.ShapeDtypeStruct(cache_v.shape, F32),
                   jax.ShapeDtypeStruct(state.shape, F32)],
        scratch_shapes=[pltpu.VMEM((2 * ATTN_KV_HEADS, 2 * t, KEY_PAD), F32),
                        pltpu.VMEM((N_RG, RET_DK, RGW), F32)],
        compiler_params=_compiler_params(1),
        name="sample_mixer",
    )(sinks, z, cache_k, cache_v, state, *tabs)


def kernel(x_prompt, x_sample, cache_swa_k, cache_swa_v, state_ret, meta_tokens, w_in, w_out, attn_sinks,
           ffn1_w_in, ffn1_w_out, ffn2_w_in, ffn2_w_out, norm_ffn1_pre, norm_ffn1_post, norm_mix_pre,
           norm_mix_post, norm_ffn2_pre, norm_ffn2_post, final_norm):
    batch, seq, d = x_prompt.shape
    dec_batch, dec_seq, _ = x_sample.shape
    depth = w_in.shape[0]
    cache_rows = cache_swa_k.shape[2]
    lead = CHUNK - N_META
    n_main, n_meta, n_sample = batch * seq, batch * CHUNK, dec_batch * dec_seq
    rows = (n_main, n_meta, n_sample)
    assert w_in.shape[2] == IN_WIDTH and w_out.shape[1] == MIX_WIDTH
    assert seq % CHUNK == 0 and cache_rows == WINDOW_CHUNKS * CHUNK and seq >= cache_rows

    meta_chunk = jnp.concatenate([jnp.zeros((lead, d), x_prompt.dtype), meta_tokens.astype(x_prompt.dtype)], axis=0)
    x_tail = jnp.concatenate([jnp.broadcast_to(meta_chunk, (batch, CHUNK, d)).reshape(n_meta, d),
                              x_sample.reshape(n_sample, d)], axis=0)
    x = (x_prompt.reshape(n_main, d), x_tail)

    tables_p = _mixer_tables(CHUNK, WINDOW_CHUNKS * CHUNK + CHUNK)
    tables_s = _mixer_tables(dec_seq, cache_rows + dec_seq)
    cache_k = cache_swa_k.reshape(depth, dec_batch, cache_rows, KV_WIDTH)
    cache_v = cache_swa_v.reshape(depth, dec_batch, cache_rows, KV_WIDTH)

    pk, pv, ps, sk, sv, ss = [], [], [], [], [], []
    for l in range(depth):
        ffn1 = (norm_ffn1_pre[l], ffn1_w_in[l].astype(BF16), ffn1_w_out[l].astype(BF16), norm_ffn1_post[l])
        ffn2 = (norm_ffn2_pre[l], ffn2_w_in[l].astype(BF16), ffn2_w_out[l].astype(BF16), norm_ffn2_post[l])
        w_proj = _projection_weight(w_in[l])
        x, z = _token_step(x, ffn1, rows, proj=(norm_mix_pre[l], w_proj))
        mix_main, mix_meta, k_p, v_p, s_p = _prompt_mixer(z, attn_sinks[l], tables_p, batch, seq, n_main, lead)
        mix_s, k_s, v_s, s_s = _sample_mixer(z, attn_sinks[l], tables_s, cache_k[l], cache_v[l], state_ret[l],
                                             dec_batch, dec_seq, n_main + n_meta)
        mix = (mix_main, mix_meta, mix_s, w_out[l].astype(BF16), norm_mix_post[l])
        x = _token_step(x, ffn2, rows, mix=mix, final=final_norm if l == depth - 1 else None)
        pk.append(k_p); pv.append(v_p); ps.append(s_p)
        sk.append(k_s); sv.append(v_s); ss.append(s_s)

    y_main, y_tail = x
    y_prompt = y_main.reshape(batch, seq, d)
    y_sample = y_tail[n_meta:].reshape(dec_batch, dec_seq, d)
    kv_shape_p = (depth, batch, cache_rows, ATTN_KV_HEADS, HEAD_DIM)
    kv_shape_s = (depth, dec_batch, cache_rows, ATTN_KV_HEADS, HEAD_DIM)
    return (y_prompt, y_sample,
            jnp.stack(pk).reshape(kv_shape_p), jnp.stack(pv).reshape(kv_shape_p), jnp.stack(ps),
            jnp.stack(sk).reshape(kv_shape_s), jnp.stack(sv).reshape(kv_shape_s), jnp.stack(ss))
```

```python
import functools
import math

import jax
import jax.numpy as jnp
import numpy as np
from jax import lax
from jax.experimental import pallas as pl
from jax.experimental.pallas import tpu as pltpu

CHUNK = 64
N_META = 16
WINDOW_CHUNKS = 2
ATTN_HEADS = 8
ATTN_KV_HEADS = 2
HEAD_DIM = 64
GROUP = ATTN_HEADS // ATTN_KV_HEADS
ATTN_SCALE = HEAD_DIM ** -0.5
RET_HEADS = 8
RET_DK = 64
RET_DV = 64
ATTN_WIDTH = ATTN_HEADS * HEAD_DIM
KV_WIDTH = ATTN_KV_HEADS * HEAD_DIM
RET_WIDTH = RET_HEADS * RET_DK
ROPE_BASE = 10000.0
EPS = 1e-6
NEG = -1e30

OFF_QA = 0
OFF_KA = OFF_QA + ATTN_WIDTH
OFF_VA = OFF_KA + KV_WIDTH
OFF_QR = OFF_VA + KV_WIDTH
OFF_KR = OFF_QR + RET_WIDTH
OFF_VR = OFF_KR + RET_WIDTH
OFF_GR = OFF_VR + RET_WIDTH
IN_WIDTH = OFF_GR + RET_WIDTH
MIX_WIDTH = ATTN_WIDTH + RET_WIDTH

LANES = 128
HALF = RET_DK // 2
RG = 4
RGW = RG * RET_DK
N_RG = RET_HEADS // RG
KEY_PAD = 256
V7X_VMEM_LIMIT_BYTES = 56 * 1024 * 1024
FF_CHUNK = 256
MAX_ROW_TILE = 512
ROW_SPLIT = 2
MAX_TILE_CHUNKS = 8
CHUNK_UNROLL = 4

BF16 = jnp.bfloat16
F32 = jnp.float32

assert RGW == 2 * LANES and KV_WIDTH == LANES and GROUP == 4 and ATTN_KV_HEADS == 2


def _row_tile(n, cap=MAX_ROW_TILE):
    t = cap
    while t > 8 and n % t:
        t //= 2
    assert n % t == 0, (n, t)
    return t


def _largest_divisor(n, cap):
    return max(d for d in range(1, cap + 1) if n % d == 0)


def _rms(x):
    return x * lax.rsqrt(jnp.mean(x * x, axis=-1, keepdims=True) + EPS)


def _silu(x):
    return x / (1.0 + jnp.exp(-x))


def _compiler_params(n_axes):
    return pltpu.CompilerParams(dimension_semantics=("arbitrary",) * n_axes,
                                vmem_limit_bytes=V7X_VMEM_LIMIT_BYTES)


def _resident(shape):
    return pl.BlockSpec(shape, lambda *_: (0,) * len(shape), pipeline_mode=pl.Buffered(1))


def _row_halves(tm):
    n_split = ROW_SPLIT if tm % (ROW_SPLIT * 16) == 0 else 1
    return [slice(r * tm // n_split, (r + 1) * tm // n_split) for r in range(n_split)]


def _token_kernel(*refs, d_ff, ff_chunk, n_main_tiles, n_meta_tiles, split_in, has_mix, has_proj, final):
    refs = list(refs)
    take = lambda n: [refs.pop(0) for _ in range(n)]
    i = pl.program_id(0)
    if split_in:
        xm_ref, xt_ref = take(2)
        x = jnp.where(i < n_main_tiles, xm_ref[...], xt_ref[...])
    else:
        (x_ref,) = take(1)
        x = x_ref[...]
    if has_mix:
        mm_ref, mt_ref, ms_ref, wmix_ref, gmix_ref = take(5)
        mix = jnp.where(i < n_main_tiles, mm_ref[...],
                        jnp.where(i < n_main_tiles + n_meta_tiles, mt_ref[...], ms_ref[...]))
        x = jnp.concatenate(
            [x[rows] + _rms(jnp.dot(mix[rows], wmix_ref[...], preferred_element_type=F32)) * gmix_ref[...]
             for rows in _row_halves(x.shape[0])], axis=0)
    gpre_ref, win_ref, wout_ref, gpost_ref = take(4)
    if has_proj:
        gproj_ref, wproj_ref = take(2)
    if final:
        (gfin_ref,) = take(1)
        ym_ref, yt_ref = take(2)
    else:
        (o_ref,) = take(1)
    if has_proj:
        (z_ref,) = take(1)
    (h_ref,) = take(1)
    assert not refs

    xn = (_rms(x) * gpre_ref[...]).astype(BF16)
    for j in range(d_ff // ff_chunk):
        lo = j * ff_chunk
        gate = jnp.dot(xn, win_ref[:, lo:lo + ff_chunk], preferred_element_type=F32)
        up = jnp.dot(xn, win_ref[:, d_ff + lo:d_ff + lo + ff_chunk], preferred_element_type=F32)
        h_ref[:, lo:lo + ff_chunk] = (_silu(gate) * up).astype(BF16)
    halves = _row_halves(x.shape[0])
    ys = [jnp.dot(h_ref[rows, :], wout_ref[...], preferred_element_type=F32) for rows in halves]
    outs = []
    for rows, y in zip(halves, ys):
        xr = x[rows] + 0.5 * (_rms(y) * gpost_ref[...])
        if has_proj:
            xn = (_rms(xr) * gproj_ref[...]).astype(BF16)
            z_ref[rows, :] = jnp.dot(xn, wproj_ref[...], preferred_element_type=F32).astype(BF16)
        if final:
            outs.append(_rms(xr) * gfin_ref[...])
        else:
            o_ref[rows, :] = xr
    if final:
        out = jnp.concatenate(outs, axis=0)

        @pl.when(i < n_main_tiles)
        def _():
            ym_ref[...] = out

        @pl.when(i >= n_main_tiles)
        def _():
            yt_ref[...] = out


def _token_step(x, ffn, rows, *, mix=None, proj=None, final=None):
    n_main, n_meta, n_sample = rows
    n = n_main + n_meta + n_sample
    split_in = isinstance(x, tuple)
    d = x[0].shape[1] if split_in else x.shape[1]
    g_pre, w_in, w_out, g_post = ffn
    d_ff = w_out.shape[0]
    tm = _row_tile(math.gcd(math.gcd(n_main, n_meta), n_sample))
    n_main_tiles, n_meta_tiles = n_main // tm, n_meta // tm
    ff_chunk = FF_CHUNK if d_ff % FF_CHUNK == 0 else d_ff

    row = lambda w: pl.BlockSpec((tm, w), lambda i: (i, 0))
    main_rows = lambda w: pl.BlockSpec((tm, w), lambda i: (jnp.minimum(i, n_main_tiles - 1), 0))
    tail_rows = lambda w: pl.BlockSpec((tm, w), lambda i: (jnp.maximum(i - n_main_tiles, 0), 0))
    gain = lambda g: g.reshape(1, d)

    args, in_specs = [], []
    if split_in:
        args += list(x)
        in_specs += [main_rows(d), tail_rows(d)]
    else:
        args.append(x)
        in_specs.append(row(d))
    if mix is not None:
        mix_main, mix_meta, mix_sample, w_mix, g_mix = mix
        args += [mix_main, mix_meta, mix_sample, w_mix, gain(g_mix)]
        in_specs += [
            main_rows(MIX_WIDTH),
            pl.BlockSpec((tm, MIX_WIDTH), lambda i: (jnp.clip(i - n_main_tiles, 0, n_meta_tiles - 1), 0)),
            pl.BlockSpec((tm, MIX_WIDTH), lambda i: (jnp.maximum(i - n_main_tiles - n_meta_tiles, 0), 0)),
            _resident(w_mix.shape), _resident((1, d))]
    args += [gain(g_pre), w_in, w_out, gain(g_post)]
    in_specs += [_resident((1, d)), _resident(w_in.shape), _resident(w_out.shape), _resident((1, d))]
    if proj is not None:
        args += [gain(proj[0]), proj[1]]
        in_specs += [_resident((1, d)), _resident(proj[1].shape)]
    out_shape, out_specs = [], []
    if final is not None:
        args.append(gain(final))
        in_specs.append(_resident((1, d)))
        out_shape += [jax.ShapeDtypeStruct((n_main, d), F32), jax.ShapeDtypeStruct((n_meta + n_sample, d), F32)]
        out_specs += [main_rows(d), tail_rows(d)]
    else:
        out_shape.append(jax.ShapeDtypeStruct((n, d), F32))
        out_specs.append(row(d))
    if proj is not None:
        out_shape.append(jax.ShapeDtypeStruct((n, IN_WIDTH), BF16))
        out_specs.append(row(IN_WIDTH))

    outs = pl.pallas_call(
        functools.partial(_token_kernel, d_ff=d_ff, ff_chunk=ff_chunk, n_main_tiles=n_main_tiles,
                          n_meta_tiles=n_meta_tiles, split_in=split_in, has_mix=mix is not None,
                          has_proj=proj is not None, final=final is not None),
        grid=(n // tm,),
        in_specs=in_specs,
        out_specs=out_specs,
        out_shape=out_shape,
        scratch_shapes=[pltpu.VMEM((tm, d_ff), BF16)],
        compiler_params=_compiler_params(1),
        name="token_step",
    )(*args)
    return outs[0] if len(outs) == 1 else tuple(outs)


def _permute_ret_columns(w):
    d = w.shape[0]
    return w.reshape(d, N_RG, RG, 2, HALF).transpose(0, 1, 3, 2, 4).reshape(d, RET_WIDTH)


def _projection_weight(w):
    parts = [w[:, :OFF_QR], _permute_ret_columns(w[:, OFF_QR:OFF_KR]), _permute_ret_columns(w[:, OFF_KR:OFF_VR]),
             w[:, OFF_VR:]]
    return jnp.concatenate(parts, axis=1).astype(BF16)


def _mixer_tables(t, s_keys):
    log_g = jnp.log1p(-jnp.exp2(-5.0 - jnp.arange(RET_HEADS, dtype=F32)))
    freqs = ROPE_BASE ** (-jnp.arange(HALF, dtype=F32) / HALF)
    pos = jnp.arange(t, dtype=F32)
    ang = pos[:, None] * freqs[None, :]
    cos, sin = jnp.tile(jnp.cos(ang), (1, RG)), jnp.tile(jnp.sin(ang), (1, RG))
    qs = RET_DK ** -0.5
    rot = jnp.stack([cos * qs, sin * qs, cos, sin])

    head = lambda g: slice(g * RG, (g + 1) * RG)
    diff = pos[:, None] - pos[None, :]
    intra = jnp.where(diff >= 0, jnp.exp(log_g[:, None, None] * jnp.maximum(diff, 0.0)), 0.0)
    kv_dec = jnp.exp((t - pos)[:, None] * log_g[None, :])
    q_dec = jnp.exp(pos[:, None] * log_g[None, :])
    dec = []
    for g in range(N_RG):
        kd = jnp.tile(jnp.repeat(kv_dec[:, head(g)], HALF, axis=1), (1, 2))
        qd = jnp.repeat(q_dec[:, head(g)], RET_DV, axis=1)
        dec += [kd, qd]
    dec = jnp.stack(dec)
    dtab = jnp.stack([jnp.concatenate(list(intra[head(g)]), axis=1) for g in range(N_RG)])

    g_t = jnp.exp(t * log_g)
    shift = -t * freqs
    stab = [jnp.broadcast_to(jnp.repeat(g_t[head(g)], RET_DV)[None, :], (HALF, RGW)) for g in range(N_RG)]
    stab += [jnp.broadcast_to(jnp.cos(shift)[:, None], (HALF, RGW)),
             jnp.broadcast_to(jnp.sin(shift)[:, None], (HALF, RGW))]
    stab = jnp.stack(stab)

    r4 = np.arange(RG * t)[:, None] // t
    lane = np.arange(RGW)[None, :]
    mask_k = jnp.asarray(r4 == (lane % LANES) // HALF, BF16)
    mask_v = jnp.asarray(r4 == lane // RET_DV, BF16)
    rs = np.arange(LANES)[:, None] // HALF
    mask_s = jnp.asarray(rs == lane // RET_DV, F32)
    seg = jnp.asarray(np.arange(RGW)[:, None] // RET_DV == lane // RET_DV, BF16)

    rowk = np.arange(KEY_PAD)[:, None] < s_keys
    lane_v = np.arange(LANES)[None, :] // HEAD_DIM
    v_keep = jnp.stack([jnp.asarray(rowk & (lane_v == p), BF16) for p in range(2)])
    v_ones = jnp.stack([jnp.asarray(np.broadcast_to(lane_v != p, (KEY_PAD, LANES)), BF16) for p in range(2)])
    return dict(rot=rot, dec=dec, dtab=dtab, stab=stab, mask_k=mask_k, mask_v=mask_v, mask_s=mask_s, seg=seg,
                v_keep=v_keep, v_ones=v_ones)


_TABLE_ORDER = ("rot", "dec", "dtab", "stab", "mask_k", "mask_v", "mask_s", "seg", "v_keep", "v_ones")


def _fill_table(sinks_ref, fill_ref, t, s_keys):
    col = lax.broadcasted_iota(jnp.int32, (t, KEY_PAD), 1)
    for kv in range(ATTN_KV_HEADS):
        for j in range(GROUP):
            fill_ref[kv, j * t:(j + 1) * t, :] = jnp.where(col == s_keys, sinks_ref[kv * GROUP + j], NEG)


def _attention(q, kwin, vwin, valid, fill_ref, tb):
    t = q.shape[0]
    lower = lax.broadcasted_iota(jnp.int32, (t, LANES), 1) < HEAD_DIM
    cols = []
    for kv in range(ATTN_KV_HEADS):
        keep = lower if kv == 0 else ~lower
        parts = []
        for j in range(2):
            blk = q[:, (2 * kv + j) * LANES:(2 * kv + j + 1) * LANES].astype(F32) * ATTN_SCALE
            rolled = pltpu.roll(blk, HEAD_DIM, axis=1)
            lo_head, hi_head = (blk, rolled) if kv == 0 else (rolled, blk)
            parts += [jnp.where(keep, lo_head, 0.0), jnp.where(keep, hi_head, 0.0)]
        qs = jnp.concatenate(parts, axis=0).astype(BF16)
        s = lax.dot_general(qs, kwin, (((1,), (1,)), ((), ())), preferred_element_type=F32)
        s = jnp.where(valid, s, fill_ref[kv])
        p = jnp.exp(s - jnp.max(s, axis=-1, keepdims=True))
        vext = vwin * tb["v_keep"][kv] + tb["v_ones"][kv]
        res = jnp.dot(p.astype(BF16), vext, preferred_element_type=F32)
        swapped = pltpu.roll(res, HEAD_DIM, axis=1)
        for j in range(2):
            lo_rows, hi_rows = slice(2 * j * t, (2 * j + 1) * t), slice((2 * j + 1) * t, (2 * j + 2) * t)
            if kv == 0:
                lo, hi = res[lo_rows] / swapped[lo_rows], swapped[hi_rows] / res[hi_rows]
            else:
                lo, hi = swapped[lo_rows] / res[lo_rows], res[hi_rows] / swapped[hi_rows]
            cols.append(jnp.where(lower, lo, hi))
    return jnp.concatenate(cols, axis=-1)


def _retention(z, state_ref, tb, row_keep):
    t = z.shape[0]
    cq, sq, ck, sk = tb["rot"][0], tb["rot"][1], tb["rot"][2], tb["rot"][3]
    cos_s, sin_s = tb["stab"][N_RG], tb["stab"][N_RG + 1]
    mask_s = tb["mask_s"][...]
    mask_sb = mask_s.astype(BF16)
    outs, gates = [], []
    for g in range(N_RG):
        lo = g * RGW
        q1 = z[:, OFF_QR + lo:OFF_QR + lo + LANES].astype(F32)
        q2 = z[:, OFF_QR + lo + LANES:OFF_QR + lo + RGW].astype(F32)
        k1 = z[:, OFF_KR + lo:OFF_KR + lo + LANES].astype(F32)
        k2 = z[:, OFF_KR + lo + LANES:OFF_KR + lo + RGW].astype(F32)
        if row_keep is not None:
            k1, k2 = k1 * row_keep, k2 * row_keep
        q_rot = jnp.concatenate([q1 * cq - q2 * sq, q2 * cq + q1 * sq], axis=-1)
        k_rot = jnp.concatenate([k1 * ck - k2 * sk, k2 * ck + k1 * sk], axis=-1)
        qb, kb = q_rot.astype(BF16), k_rot.astype(BF16)
        kdb = (k_rot * tb["dec"][2 * g]).astype(BF16)
        vb = z[:, OFF_VR + lo:OFF_VR + lo + RGW]
        gate = z[:, OFF_GR + lo:OFF_GR + lo + RGW].astype(F32)

        k_bd = jnp.concatenate([kb] * RG, axis=0) * tb["mask_k"][...]
        v_bd = jnp.concatenate([vb] * RG, axis=0) * tb["mask_v"][...]
        scores = lax.dot_general(qb, k_bd, (((1,), (1,)), ((), ())), preferred_element_type=F32)
        o = jnp.dot((scores * tb["dtab"][g]).astype(BF16), v_bd, preferred_element_type=F32)

        c1, c2 = state_ref[g, 0:HALF, :], state_ref[g, HALF:2 * HALF, :]
        s_bd = jnp.concatenate([jnp.concatenate([c.astype(BF16)] * RG, axis=0) * mask_sb for c in (c1, c2)], axis=0)
        o = o + jnp.dot(qb, s_bd, preferred_element_type=F32) * tb["dec"][2 * g + 1]

        w_full = lax.dot_general(kdb, vb, (((0,), (0,)), ((), ())), preferred_element_type=F32)
        w = []
        for half in range(2):
            wm = w_full[half * LANES:(half + 1) * LANES] * mask_s
            w.append(wm[0:HALF] + wm[HALF:2 * HALF] + wm[2 * HALF:3 * HALF] + wm[3 * HALF:4 * HALF])
        a1 = tb["stab"][g] * c1 + w[0]
        a2 = tb["stab"][g] * c2 + w[1]
        state_ref[g, 0:HALF, :] = a1 * cos_s - a2 * sin_s
        state_ref[g, HALF:2 * HALF, :] = a2 * cos_s + a1 * sin_s

        outs.append(o)
        gates.append(gate)

    parts = []
    for o in outs:
        sq = o * o
        hi = sq.astype(BF16)
        parts += [hi, (sq - hi.astype(F32)).astype(BF16)]
    ssq = jnp.dot(jnp.concatenate(parts, axis=0), tb["seg"][...], preferred_element_type=F32)
    res = []
    for g, (o, gate) in enumerate(zip(outs, gates)):
        ss = ssq[2 * g * t:(2 * g + 1) * t] + ssq[(2 * g + 1) * t:(2 * g + 2) * t]
        res.append(o * lax.rsqrt(ss * (1.0 / RET_DV) + EPS) * _silu(gate))
    return jnp.concatenate(res, axis=-1)


def _mixer_block(z, windows, valid, fill_ref, state_ref, tb, row_keep=None):
    attn = _attention(z[:, OFF_QA:OFF_QA + ATTN_WIDTH], *windows, valid, fill_ref, tb)
    ret = _retention(z, state_ref, tb, row_keep)
    return jnp.concatenate([attn, ret], axis=-1).astype(BF16)


def _state_to_heads(state_ref, out_ref):
    for g in range(N_RG):
        for hh in range(RG):
            out_ref[0, g * RG + hh] = state_ref[g, :, hh * RET_DV:(hh + 1) * RET_DV]


def _prompt_mixer_kernel(sinks_ref, z_ref, zmeta_ref, *rest, tile_chunks, lead):
    tb = dict(zip(_TABLE_ORDER, rest[:len(_TABLE_ORDER)]))
    (mix_ref, mixmeta_ref, klast_ref, vlast_ref, sfin_ref,
     kctx_ref, vctx_ref, fill_ref, state_ref) = rest[len(_TABLE_ORDER):]
    g = pl.program_id(1)
    n_g = pl.num_programs(1)
    ctx = WINDOW_CHUNKS * CHUNK
    rows = tile_chunks * CHUNK
    s_keys = ctx + CHUNK
    col = lax.broadcasted_iota(jnp.int32, (1, KEY_PAD), 1)

    def put_keys(dst, z, n):
        kctx_ref[dst:dst + n, :] = z[:, OFF_KA:OFF_KA + KV_WIDTH]
        vctx_ref[dst:dst + n, :] = z[:, OFF_VA:OFF_VA + KV_WIDTH]

    def block(z, r0, first_row, row_keep=None):
        windows = [r[pl.ds(r0, KEY_PAD), :] for r in (kctx_ref, vctx_ref)]
        valid = (col < s_keys) & (col + (first_row - ctx) >= lead)
        return _mixer_block(z, windows, valid, fill_ref, state_ref, tb, row_keep)

    @pl.when(g == 0)
    def _():
        for r in (kctx_ref, vctx_ref):
            r[...] = jnp.zeros(r.shape, BF16)
        state_ref[...] = jnp.zeros(state_ref.shape, F32)
        _fill_table(sinks_ref, fill_ref, CHUNK, s_keys)
        zm = zmeta_ref[...]
        put_keys(ctx, zm, CHUNK)
        row_keep = (lax.broadcasted_iota(jnp.int32, (CHUNK, 1), 0) >= lead).astype(F32)
        mixmeta_ref[...] = block(zm, 0, 0, row_keep)
        put_keys(ctx - CHUNK, zm, CHUNK)

    put_keys(ctx, z_ref[...], rows)

    def chunk_body(c, carry):
        r0 = pl.multiple_of(c * CHUNK, CHUNK)
        first_row = (1 + g * tile_chunks + c) * CHUNK
        mix_ref[pl.ds(r0, CHUNK), :] = block(z_ref[pl.ds(r0, CHUNK), :], r0, first_row)
        return carry

    lax.fori_loop(0, tile_chunks, chunk_body, 0, unroll=CHUNK_UNROLL if tile_chunks % CHUNK_UNROLL == 0 else 1)

    for r in (kctx_ref, vctx_ref):
        r[0:ctx, :] = r[rows:rows + ctx, :]

    @pl.when(g == n_g - 1)
    def _():
        klast_ref[0] = kctx_ref[0:ctx, :].astype(F32)
        vlast_ref[0] = vctx_ref[0:ctx, :].astype(F32)
        _state_to_heads(state_ref, sfin_ref)


def _prompt_mixer(z, sinks, tables, batch, seq, n_main, lead):
    n_chunks = seq // CHUNK
    tile_chunks = _largest_divisor(n_chunks, MAX_TILE_CHUNKS)
    n_g = n_chunks // tile_chunks
    rows = tile_chunks * CHUNK
    ctx = WINDOW_CHUNKS * CHUNK
    assert rows >= ctx and n_main % CHUNK == 0
    meta_block0 = n_main // CHUNK
    tabs = [tables[k] for k in _TABLE_ORDER]
    state_shape = (RET_HEADS, RET_DK, RET_DV)
    ctx_rows = ctx + rows + KEY_PAD - (ctx + CHUNK)
    ctx_buf = pltpu.VMEM((ctx_rows, KV_WIDTH), BF16)
    return pl.pallas_call(
        functools.partial(_prompt_mixer_kernel, tile_chunks=tile_chunks, lead=lead),
        grid=(batch, n_g),
        in_specs=[pl.BlockSpec(memory_space=pltpu.SMEM),
                  pl.BlockSpec((rows, IN_WIDTH), lambda b, g: (b * n_g + g, 0)),
                  pl.BlockSpec((CHUNK, IN_WIDTH), lambda b, g: (meta_block0 + b, 0))]
                 + [_resident(t.shape) for t in tabs],
        out_specs=[pl.BlockSpec((rows, MIX_WIDTH), lambda b, g: (b * n_g + g, 0)),
                   pl.BlockSpec((CHUNK, MIX_WIDTH), lambda b, g: (b, 0)),
                   pl.BlockSpec((1, ctx, KV_WIDTH), lambda b, g: (b, 0, 0)),
                   pl.BlockSpec((1, ctx, KV_WIDTH), lambda b, g: (b, 0, 0)),
                   pl.BlockSpec((1,) + state_shape, lambda b, g: (b, 0, 0, 0))],
        out_shape=[jax.ShapeDtypeStruct((n_main, MIX_WIDTH), BF16),
                   jax.ShapeDtypeStruct((batch * CHUNK, MIX_WIDTH), BF16),
                   jax.ShapeDtypeStruct((batch, ctx, KV_WIDTH), F32),
                   jax.ShapeDtypeStruct((batch, ctx, KV_WIDTH), F32),
                   jax.ShapeDtypeStruct((batch,) + state_shape, F32)],
        scratch_shapes=[ctx_buf, ctx_buf,
                        pltpu.VMEM((ATTN_KV_HEADS, GROUP * CHUNK, KEY_PAD), F32),
                        pltpu.VMEM((N_RG, RET_DK, RGW), F32)],
        compiler_params=_compiler_params(2),
        name="prompt_mixer",
    )(sinks, z, z, *tabs)


def _sample_mixer_kernel(sinks_ref, z_ref, ck_ref, cv_ref, s0_ref, *rest, cache_rows):
    tb = dict(zip(_TABLE_ORDER, rest[:len(_TABLE_ORDER)]))
    mix_ref, knew_ref, vnew_ref, snew_ref, fill_ref, state_ref = rest[len(_TABLE_ORDER):]
    z = z_ref[...]
    t = z.shape[0]
    s_keys = cache_rows + t
    pad = jnp.zeros((KEY_PAD - s_keys, KV_WIDTH), F32)
    k_all = jnp.concatenate([ck_ref[0], z[:, OFF_KA:OFF_KA + KV_WIDTH].astype(F32)], axis=0)
    v_all = jnp.concatenate([cv_ref[0], z[:, OFF_VA:OFF_VA + KV_WIDTH].astype(F32)], axis=0)
    windows = [jnp.concatenate([a, pad], axis=0).astype(BF16) for a in (k_all, v_all)]
    _fill_table(sinks_ref, fill_ref, t, s_keys)
    for g in range(N_RG):
        for hh in range(RG):
            state_ref[g, :, hh * RET_DV:(hh + 1) * RET_DV] = s0_ref[0, g * RG + hh]
    col = lax.broadcasted_iota(jnp.int32, (1, KEY_PAD), 1)
    mix_ref[...] = _mixer_block(z, windows, col < s_keys, fill_ref, state_ref, tb)
    knew_ref[0] = k_all[s_keys - cache_rows:]
    vnew_ref[0] = v_all[s_keys - cache_rows:]
    _state_to_heads(state_ref, snew_ref)


def _sample_mixer(z, sinks, tables, cache_k, cache_v, state, batch, t, row_offset):
    cache_rows = cache_k.shape[1]
    assert row_offset % t == 0 and cache_rows + t < KEY_PAD
    first_block = row_offset // t
    tabs = [tables[k] for k in _TABLE_ORDER]
    state_shape = (RET_HEADS, RET_DK, RET_DV)
    cache_spec = pl.BlockSpec((1, cache_rows, KV_WIDTH), lambda b: (b, 0, 0))
    state_spec = pl.BlockSpec((1,) + state_shape, lambda b: (b, 0, 0, 0))
    return pl.pallas_call(
        functools.partial(_sample_mixer_kernel, cache_rows=cache_rows),
        grid=(batch,),
        in_specs=[pl.BlockSpec(memory_space=pltpu.SMEM),
                  pl.BlockSpec((t, IN_WIDTH), lambda b: (first_block + b, 0)),
                  cache_spec, cache_spec, state_spec] + [_resident(x.shape) for x in tabs],
        out_specs=[pl.BlockSpec((t, MIX_WIDTH), lambda b: (b, 0)), cache_spec, cache_spec, state_spec],
        out_shape=[jax.ShapeDtypeStruct((batch * t, MIX_WIDTH), BF16),
                   jax.ShapeDtypeStruct(cache_k.shape, F32),
                   jax.ShapeDtypeStruct(cache_v.shape, F32),
                   jax.ShapeDtypeStruct(state.shape, F32)],
        scratch_shapes=[pltpu.VMEM((ATTN_KV_HEADS, GROUP * t, KEY_PAD), F32),
                        pltpu.VMEM((N_RG, RET_DK, RGW), F32)],
        compiler_params=_compiler_params(1),
        name="sample_mixer",
    )(sinks, z, cache_k, cache_v, state, *tabs)


def kernel(x_prompt, x_sample, cache_swa_k, cache_swa_v, state_ret, meta_tokens, w_in, w_out, attn_sinks,
           ffn1_w_in, ffn1_w_out, ffn2_w_in, ffn2_w_out, norm_ffn1_pre, norm_ffn1_post, norm_mix_pre,
           norm_mix_post, norm_ffn2_pre, norm_ffn2_post, final_norm):
    batch, seq, d = x_prompt.shape
    dec_batch, dec_seq, _ = x_sample.shape
    depth = w_in.shape[0]
    cache_rows = cache_swa_k.shape[2]
    lead = CHUNK - N_META
    n_main, n_meta, n_sample = batch * seq, batch * CHUNK, dec_batch * dec_seq
    rows = (n_main, n_meta, n_sample)
    assert w_in.shape[2] == IN_WIDTH and w_out.shape[1] == MIX_WIDTH
    assert seq % CHUNK == 0 and cache_rows == WINDOW_CHUNKS * CHUNK and seq >= cache_rows

    meta_chunk = jnp.concatenate([jnp.zeros((lead, d), x_prompt.dtype), meta_tokens.astype(x_prompt.dtype)], axis=0)
    x_tail = jnp.concatenate([jnp.broadcast_to(meta_chunk, (batch, CHUNK, d)).reshape(n_meta, d),
                              x_sample.reshape(n_sample, d)], axis=0)
    x = (x_prompt.reshape(n_main, d), x_tail)

    tables_p = _mixer_tables(CHUNK, WINDOW_CHUNKS * CHUNK + CHUNK)
    tables_s = _mixer_tables(dec_seq, cache_rows + dec_seq)
    cache_k = cache_swa_k.reshape(depth, dec_batch, cache_rows, KV_WIDTH)
    cache_v = cache_swa_v.reshape(depth, dec_batch, cache_rows, KV_WIDTH)

    pk, pv, ps, sk, sv, ss = [], [], [], [], [], []
    for l in range(depth):
        ffn1 = (norm_ffn1_pre[l], ffn1_w_in[l].astype(BF16), ffn1_w_out[l].astype(BF16), norm_ffn1_post[l])
        ffn2 = (norm_ffn2_pre[l], ffn2_w_in[l].astype(BF16), ffn2_w_out[l].astype(BF16), norm_ffn2_post[l])
        w_proj = _projection_weight(w_in[l])
        x, z = _token_step(x, ffn1, rows, proj=(norm_mix_pre[l], w_proj))
        mix_main, mix_meta, k_p, v_p, s_p = _prompt_mixer(z, attn_sinks[l], tables_p, batch, seq, n_main, lead)
        mix_s, k_s, v_s, s_s = _sample_mixer(z, attn_sinks[l], tables_s, cache_k[l], cache_v[l], state_ret[l],
                                             dec_batch, dec_seq, n_main + n_meta)
        mix = (mix_main, mix_meta, mix_s, w_out[l].astype(BF16), norm_mix_post[l])
        x = _token_step(x, ffn2, rows, mix=mix, final=final_norm if l == depth - 1 else None)
        pk.append(k_p); pv.append(v_p); ps.append(s_p)
        sk.append(k_s); sv.append(v_s); ss.append(s_s)

    y_main, y_tail = x
    y_prompt = y_main.reshape(batch, seq, d)
    y_sample = y_tail[n_meta:].reshape(dec_batch, dec_seq, d)
    kv_shape_p = (depth, batch, cache_rows, ATTN_KV_HEADS, HEAD_DIM)
    kv_shape_s = (depth, dec_batch, cache_rows, ATTN_KV_HEADS, HEAD_DIM)
    return (y_prompt, y_sample,
            jnp.stack(pk).reshape(kv_shape_p), jnp.stack(pv).reshape(kv_shape_p), jnp.stack(ps),
            jnp.stack(sk).reshape(kv_shape_s), jnp.stack(sv).reshape(kv_shape_s), jnp.stack(ss))
```

```python
import functools
import math

import jax
import jax.numpy as jnp
import numpy as np
from jax import lax
from jax.experimental import pallas as pl
from jax.experimental.pallas import tpu as pltpu

CHUNK = 64
N_META = 16
WINDOW_CHUNKS = 2
ATTN_HEADS = 8
ATTN_KV_HEADS = 2
HEAD_DIM = 64
GROUP = ATTN_HEADS // ATTN_KV_HEADS
ATTN_SCALE = HEAD_DIM ** -0.5
RET_HEADS = 8
RET_DK = 64
RET_DV = 64
ATTN_WIDTH = ATTN_HEADS * HEAD_DIM
KV_WIDTH = ATTN_KV_HEADS * HEAD_DIM
RET_WIDTH = RET_HEADS * RET_DK
ROPE_BASE = 10000.0
EPS = 1e-6
NEG = -1e30

OFF_QA = 0
OFF_KA = OFF_QA + ATTN_WIDTH
OFF_VA = OFF_KA + KV_WIDTH
OFF_QR = OFF_VA + KV_WIDTH
OFF_KR = OFF_QR + RET_WIDTH
OFF_VR = OFF_KR + RET_WIDTH
OFF_GR = OFF_VR + RET_WIDTH
IN_WIDTH = OFF_GR + RET_WIDTH
MIX_WIDTH = ATTN_WIDTH + RET_WIDTH

LANES = 128
HALF = RET_DK // 2
RG = 4
RGW = RG * RET_DK
N_RG = RET_HEADS // RG
KEY_PAD = 256
V7X_VMEM_LIMIT_BYTES = 56 * 1024 * 1024
FF_CHUNK = 256
MAX_ROW_TILE = 512
ROW_SPLIT = 2
MAX_TILE_CHUNKS = 8

BF16 = jnp.bfloat16
F32 = jnp.float32

assert RGW == 2 * LANES and KV_WIDTH == LANES and GROUP == 4 and ATTN_KV_HEADS == 2


def _row_tile(n, cap=MAX_ROW_TILE):
    t = cap
    while t > 8 and n % t:
        t //= 2
    assert n % t == 0, (n, t)
    return t


def _largest_divisor(n, cap):
    return max(d for d in range(1, cap + 1) if n % d == 0)


def _rms(x):
    return x * lax.rsqrt(jnp.mean(x * x, axis=-1, keepdims=True) + EPS)


def _silu(x):
    return x / (1.0 + jnp.exp(-x))


def _compiler_params(n_axes):
    return pltpu.CompilerParams(dimension_semantics=("arbitrary",) * n_axes,
                                vmem_limit_bytes=V7X_VMEM_LIMIT_BYTES)


def _resident(shape):
    return pl.BlockSpec(shape, lambda *_: (0,) * len(shape), pipeline_mode=pl.Buffered(1))


def _row_halves(tm):
    n_split = ROW_SPLIT if tm % (ROW_SPLIT * 16) == 0 else 1
    return [slice(r * tm // n_split, (r + 1) * tm // n_split) for r in range(n_split)]


def _token_kernel(*refs, d_ff, ff_chunk, n_main_tiles, n_meta_tiles, split_in, has_mix, has_proj, final):
    refs = list(refs)
    take = lambda n: [refs.pop(0) for _ in range(n)]
    i = pl.program_id(0)
    if split_in:
        xm_ref, xt_ref = take(2)
        x = jnp.where(i < n_main_tiles, xm_ref[...], xt_ref[...])
    else:
        (x_ref,) = take(1)
        x = x_ref[...]
    if has_mix:
        mm_ref, mt_ref, ms_ref, wmix_ref, gmix_ref = take(5)
        mix = jnp.where(i < n_main_tiles, mm_ref[...],
                        jnp.where(i < n_main_tiles + n_meta_tiles, mt_ref[...], ms_ref[...]))
        x = jnp.concatenate(
            [x[rows] + _rms(jnp.dot(mix[rows], wmix_ref[...], preferred_element_type=F32)) * gmix_ref[...]
             for rows in _row_halves(x.shape[0])], axis=0)
    gpre_ref, win_ref, wout_ref, gpost_ref = take(4)
    if has_proj:
        gproj_ref, wproj_ref = take(2)
    if final:
        (gfin_ref,) = take(1)
        ym_ref, yt_ref = take(2)
    else:
        (o_ref,) = take(1)
    if has_proj:
        (z_ref,) = take(1)
    (h_ref,) = take(1)
    assert not refs

    xn = (_rms(x) * gpre_ref[...]).astype(BF16)
    for j in range(d_ff // ff_chunk):
        lo = j * ff_chunk
        gate = jnp.dot(xn, win_ref[:, lo:lo + ff_chunk], preferred_element_type=F32)
        up = jnp.dot(xn, win_ref[:, d_ff + lo:d_ff + lo + ff_chunk], preferred_element_type=F32)
        h_ref[:, lo:lo + ff_chunk] = (_silu(gate) * up).astype(BF16)
    halves = _row_halves(x.shape[0])
    ys = [jnp.dot(h_ref[rows, :], wout_ref[...], preferred_element_type=F32) for rows in halves]
    outs = []
    for rows, y in zip(halves, ys):
        xr = x[rows] + 0.5 * (_rms(y) * gpost_ref[...])
        if has_proj:
            xn = (_rms(xr) * gproj_ref[...]).astype(BF16)
            z_ref[rows, :] = jnp.dot(xn, wproj_ref[...], preferred_element_type=F32).astype(BF16)
        if final:
            outs.append(_rms(xr) * gfin_ref[...])
        else:
            o_ref[rows, :] = xr
    if final:
        out = jnp.concatenate(outs, axis=0)

        @pl.when(i < n_main_tiles)
        def _():
            ym_ref[...] = out

        @pl.when(i >= n_main_tiles)
        def _():
            yt_ref[...] = out


def _token_step(x, ffn, rows, *, mix=None, proj=None, final=None):
    n_main, n_meta, n_sample = rows
    n = n_main + n_meta + n_sample
    split_in = isinstance(x, tuple)
    d = x[0].shape[1] if split_in else x.shape[1]
    g_pre, w_in, w_out, g_post = ffn
    d_ff = w_out.shape[0]
    tm = _row_tile(math.gcd(math.gcd(n_main, n_meta), n_sample))
    n_main_tiles, n_meta_tiles = n_main // tm, n_meta // tm
    ff_chunk = FF_CHUNK if d_ff % FF_CHUNK == 0 else d_ff

    row = lambda w: pl.BlockSpec((tm, w), lambda i: (i, 0))
    main_rows = lambda w: pl.BlockSpec((tm, w), lambda i: (jnp.minimum(i, n_main_tiles - 1), 0))
    tail_rows = lambda w: pl.BlockSpec((tm, w), lambda i: (jnp.maximum(i - n_main_tiles, 0), 0))
    gain = lambda g: g.reshape(1, d)

    args, in_specs = [], []
    if split_in:
        args += list(x)
        in_specs += [main_rows(d), tail_rows(d)]
    else:
        args.append(x)
        in_specs.append(row(d))
    if mix is not None:
        mix_main, mix_meta, mix_sample, w_mix, g_mix = mix
        args += [mix_main, mix_meta, mix_sample, w_mix, gain(g_mix)]
        in_specs += [
            main_rows(MIX_WIDTH),
            pl.BlockSpec((tm, MIX_WIDTH), lambda i: (jnp.clip(i - n_main_tiles, 0, n_meta_tiles - 1), 0)),
            pl.BlockSpec((tm, MIX_WIDTH), lambda i: (jnp.maximum(i - n_main_tiles - n_meta_tiles, 0), 0)),
            _resident(w_mix.shape), _resident((1, d))]
    args += [gain(g_pre), w_in, w_out, gain(g_post)]
    in_specs += [_resident((1, d)), _resident(w_in.shape), _resident(w_out.shape), _resident((1, d))]
    if proj is not None:
        args += [gain(proj[0]), proj[1]]
        in_specs += [_resident((1, d)), _resident(proj[1].shape)]
    out_shape, out_specs = [], []
    if final is not None:
        args.append(gain(final))
        in_specs.append(_resident((1, d)))
        out_shape += [jax.ShapeDtypeStruct((n_main, d), F32), jax.ShapeDtypeStruct((n_meta + n_sample, d), F32)]
        out_specs += [main_rows(d), tail_rows(d)]
    else:
        out_shape.append(jax.ShapeDtypeStruct((n, d), F32))
        out_specs.append(row(d))
    if proj is not None:
        out_shape.append(jax.ShapeDtypeStruct((n, IN_WIDTH), BF16))
        out_specs.append(row(IN_WIDTH))

    outs = pl.pallas_call(
        functools.partial(_token_kernel, d_ff=d_ff, ff_chunk=ff_chunk, n_main_tiles=n_main_tiles,
                          n_meta_tiles=n_meta_tiles, split_in=split_in, has_mix=mix is not None,
                          has_proj=proj is not None, final=final is not None),
        grid=(n // tm,),
        in_specs=in_specs,
        out_specs=out_specs,
        out_shape=out_shape,
        scratch_shapes=[pltpu.VMEM((tm, d_ff), BF16)],
        compiler_params=_compiler_params(1),
        name="token_step",
    )(*args)
    return outs[0] if len(outs) == 1 else tuple(outs)


def _permute_ret_columns(w):
    d = w.shape[0]
    return w.reshape(d, N_RG, RG, 2, HALF).transpose(0, 1, 3, 2, 4).reshape(d, RET_WIDTH)


def _projection_weight(w):
    parts = [w[:, :OFF_QR], _permute_ret_columns(w[:, OFF_QR:OFF_KR]), _permute_ret_columns(w[:, OFF_KR:OFF_VR]),
             w[:, OFF_VR:]]
    return jnp.concatenate(parts, axis=1).astype(BF16)


def _mixer_tables(t, s_keys):
    log_g = jnp.log1p(-jnp.exp2(-5.0 - jnp.arange(RET_HEADS, dtype=F32)))
    freqs = ROPE_BASE ** (-jnp.arange(HALF, dtype=F32) / HALF)
    pos = jnp.arange(t, dtype=F32)
    ang = pos[:, None] * freqs[None, :]
    cos, sin = jnp.tile(jnp.cos(ang), (1, RG)), jnp.tile(jnp.sin(ang), (1, RG))
    qs = RET_DK ** -0.5
    rot = jnp.stack([cos * qs, sin * qs, cos, sin])

    head = lambda g: slice(g * RG, (g + 1) * RG)
    diff = pos[:, None] - pos[None, :]
    intra = jnp.where(diff >= 0, jnp.exp(log_g[:, None, None] * jnp.maximum(diff, 0.0)), 0.0)
    kv_dec = jnp.exp((t - pos)[:, None] * log_g[None, :])
    q_dec = jnp.exp(pos[:, None] * log_g[None, :])
    dec = []
    for g in range(N_RG):
        kd = jnp.tile(jnp.repeat(kv_dec[:, head(g)], HALF, axis=1), (1, 2))
        qd = jnp.repeat(q_dec[:, head(g)], RET_DV, axis=1)
        dec += [kd, qd]
    dec = jnp.stack(dec)
    dtab = jnp.stack([jnp.concatenate(list(intra[head(g)]), axis=1) for g in range(N_RG)])

    g_t = jnp.exp(t * log_g)
    shift = -t * freqs
    stab = [jnp.broadcast_to(jnp.repeat(g_t[head(g)], RET_DV)[None, :], (HALF, RGW)) for g in range(N_RG)]
    stab += [jnp.broadcast_to(jnp.cos(shift)[:, None], (HALF, RGW)),
             jnp.broadcast_to(jnp.sin(shift)[:, None], (HALF, RGW))]
    stab = jnp.stack(stab)

    r4 = np.arange(RG * t)[:, None] // t
    lane = np.arange(RGW)[None, :]
    mask_k = jnp.asarray(r4 == (lane % LANES) // HALF, BF16)
    mask_v = jnp.asarray(r4 == lane // RET_DV, BF16)
    rs = np.arange(LANES)[:, None] // HALF
    mask_s = jnp.asarray(rs == lane // RET_DV, F32)
    seg = jnp.asarray(np.arange(RGW)[:, None] // RET_DV == lane // RET_DV, BF16)

    rowk = np.arange(KEY_PAD)[:, None] < s_keys
    lane_v = np.arange(LANES)[None, :] // HEAD_DIM
    v_keep = jnp.stack([jnp.asarray(rowk & (lane_v == p), BF16) for p in range(2)])
    v_ones = jnp.stack([jnp.asarray(np.broadcast_to(lane_v != p, (KEY_PAD, LANES)), BF16) for p in range(2)])
    return dict(rot=rot, dec=dec, dtab=dtab, stab=stab, mask_k=mask_k, mask_v=mask_v, mask_s=mask_s, seg=seg,
                v_keep=v_keep, v_ones=v_ones)


_TABLE_ORDER = ("rot", "dec", "dtab", "stab", "mask_k", "mask_v", "mask_s", "seg", "v_keep", "v_ones")


def _fill_table(sinks_ref, fill_ref, t, s_keys):
    col = lax.broadcasted_iota(jnp.int32, (t, KEY_PAD), 1)
    for kv in range(ATTN_KV_HEADS):
        for j in range(GROUP):
            fill_ref[kv, j * t:(j + 1) * t, :] = jnp.where(col == s_keys, sinks_ref[kv * GROUP + j], NEG)


def _attn_scores(q, kwin):
    t = q.shape[0]
    lower = lax.broadcasted_iota(jnp.int32, (t, LANES), 1) < HEAD_DIM
    out = []
    for kv in range(ATTN_KV_HEADS):
        keep = lower if kv == 0 else ~lower
        parts = []
        for j in range(2):
            blk = q[:, (2 * kv + j) * LANES:(2 * kv + j + 1) * LANES].astype(F32) * ATTN_SCALE
            rolled = pltpu.roll(blk, HEAD_DIM, axis=1)
            lo_head, hi_head = (blk, rolled) if kv == 0 else (rolled, blk)
            parts += [jnp.where(keep, lo_head, 0.0), jnp.where(keep, hi_head, 0.0)]
        qs = jnp.concatenate(parts, axis=0).astype(BF16)
        out.append(lax.dot_general(qs, kwin, (((1,), (1,)), ((), ())), preferred_element_type=F32))
    return out


def _attn_values(scores, vwin, valid, fill_ref, tb):
    out = []
    for kv, s in enumerate(scores):
        s = jnp.where(valid, s, fill_ref[kv])
        p = jnp.exp(s - jnp.max(s, axis=-1, keepdims=True))
        vext = vwin * tb["v_keep"][kv] + tb["v_ones"][kv]
        out.append(jnp.dot(p.astype(BF16), vext, preferred_element_type=F32))
    return out


def _attn_normalise(results, t):
    lower = lax.broadcasted_iota(jnp.int32, (t, LANES), 1) < HEAD_DIM
    cols = []
    for kv, res in enumerate(results):
        swapped = pltpu.roll(res, HEAD_DIM, axis=1)
        for j in range(2):
            lo_rows, hi_rows = slice(2 * j * t, (2 * j + 1) * t), slice((2 * j + 1) * t, (2 * j + 2) * t)
            if kv == 0:
                lo, hi = res[lo_rows] / swapped[lo_rows], swapped[hi_rows] / res[hi_rows]
            else:
                lo, hi = swapped[lo_rows] / res[lo_rows], res[hi_rows] / swapped[hi_rows]
            cols.append(jnp.where(lower, lo, hi))
    return jnp.concatenate(cols, axis=-1)


def _ret_scores(zcols, g, tb, row_keep):
    lo = g * RGW
    cq, sq, ck, sk = tb["rot"][0], tb["rot"][1], tb["rot"][2], tb["rot"][3]
    q1 = zcols(OFF_QR + lo, OFF_QR + lo + LANES).astype(F32)
    q2 = zcols(OFF_QR + lo + LANES, OFF_QR + lo + RGW).astype(F32)
    k1 = zcols(OFF_KR + lo, OFF_KR + lo + LANES).astype(F32)
    k2 = zcols(OFF_KR + lo + LANES, OFF_KR + lo + RGW).astype(F32)
    if row_keep is not None:
        k1, k2 = k1 * row_keep, k2 * row_keep
    q_rot = jnp.concatenate([q1 * cq - q2 * sq, q2 * cq + q1 * sq], axis=-1)
    k_rot = jnp.concatenate([k1 * ck - k2 * sk, k2 * ck + k1 * sk], axis=-1)
    qb, kb = q_rot.astype(BF16), k_rot.astype(BF16)
    kdb = (k_rot * tb["dec"][2 * g]).astype(BF16)
    vb = zcols(OFF_VR + lo, OFF_VR + lo + RGW)
    k_bd = jnp.concatenate([kb] * RG, axis=0) * tb["mask_k"][...]
    scores = lax.dot_general(qb, k_bd, (((1,), (1,)), ((), ())), preferred_element_type=F32)
    w_full = lax.dot_general(kdb, vb, (((0,), (0,)), ((), ())), preferred_element_type=F32)
    return qb, vb, scores, w_full


def _ret_outputs(g, qb, vb, scores, w_full, state_ref, tb):
    mask_s = tb["mask_s"][...]
    v_bd = jnp.concatenate([vb] * RG, axis=0) * tb["mask_v"][...]
    intra = jnp.dot((scores * tb["dtab"][g]).astype(BF16), v_bd, preferred_element_type=F32)
    c1, c2 = state_ref[g, 0:HALF, :], state_ref[g, HALF:2 * HALF, :]
    mask_sb = mask_s.astype(BF16)
    s_bd = jnp.concatenate([jnp.concatenate([c.astype(BF16)] * RG, axis=0) * mask_sb for c in (c1, c2)], axis=0)
    cross = jnp.dot(qb, s_bd, preferred_element_type=F32)
    w = []
    for half in range(2):
        wm = w_full[half * LANES:(half + 1) * LANES] * mask_s
        w.append(wm[0:HALF] + wm[HALF:2 * HALF] + wm[2 * HALF:3 * HALF] + wm[3 * HALF:4 * HALF])
    cos_s, sin_s = tb["stab"][N_RG], tb["stab"][N_RG + 1]
    a1 = tb["stab"][g] * c1 + w[0]
    a2 = tb["stab"][g] * c2 + w[1]
    state_ref[g, 0:HALF, :] = a1 * cos_s - a2 * sin_s
    state_ref[g, HALF:2 * HALF, :] = a2 * cos_s + a1 * sin_s
    return intra, cross


def _ret_square_sums(outs, tb):
    parts = []
    for o in outs:
        sq = o * o
        hi = sq.astype(BF16)
        parts += [hi, (sq - hi.astype(F32)).astype(BF16)]
    return jnp.dot(jnp.concatenate(parts, axis=0), tb["seg"][...], preferred_element_type=F32)


def _mixer_stages(zcols, kwin, vwin, valid, fill_ref, state_ref, tb, row_keep, emit_attn, emit_ret):
    attn_scores = _attn_scores(zcols(OFF_QA, OFF_QA + ATTN_WIDTH), kwin())
    ret = [_ret_scores(zcols, g, tb, row_keep) for g in range(N_RG)]
    t = ret[0][0].shape[0]
    yield
    attn_res = _attn_values(attn_scores, vwin(), valid, fill_ref, tb)
    pairs = [_ret_outputs(g, *ret[g], state_ref, tb) for g in range(N_RG)]
    yield
    emit_attn(_attn_normalise(attn_res, t).astype(BF16))
    outs = [intra + cross * tb["dec"][2 * g + 1] for g, (intra, cross) in enumerate(pairs)]
    ssq = _ret_square_sums(outs, tb)
    yield
    res = []
    for g, o in enumerate(outs):
        gate = zcols(OFF_GR + g * RGW, OFF_GR + (g + 1) * RGW).astype(F32)
        ss = ssq[2 * g * t:(2 * g + 1) * t] + ssq[(2 * g + 1) * t:(2 * g + 2) * t]
        res.append(o * lax.rsqrt(ss * (1.0 / RET_DV) + EPS) * _silu(gate))
    emit_ret(jnp.concatenate(res, axis=-1).astype(BF16))
    yield


N_STAGES = 4


def _run_blocks(blocks):
    for step in range(len(blocks) + N_STAGES - 1):
        for stage in reversed(range(N_STAGES)):
            b = step - stage
            if 0 <= b < len(blocks):
                next(blocks[b])


def _state_to_heads(state_ref, out_ref):
    for g in range(N_RG):
        for hh in range(RG):
            out_ref[0, g * RG + hh] = state_ref[g, :, hh * RET_DV:(hh + 1) * RET_DV]


def _prompt_mixer_kernel(sinks_ref, z_ref, zmeta_ref, *rest, tile_chunks, lead):
    tb = dict(zip(_TABLE_ORDER, rest[:len(_TABLE_ORDER)]))
    (mix_ref, mixmeta_ref, klast_ref, vlast_ref, sfin_ref,
     kctx_ref, vctx_ref, fill_ref, state_ref) = rest[len(_TABLE_ORDER):]
    g = pl.program_id(1)
    n_g = pl.num_programs(1)
    ctx = WINDOW_CHUNKS * CHUNK
    rows = tile_chunks * CHUNK
    s_keys = ctx + CHUNK
    col = lax.broadcasted_iota(jnp.int32, (1, KEY_PAD), 1)

    def put_keys(dst, z, n):
        kctx_ref[dst:dst + n, :] = z[:, OFF_KA:OFF_KA + KV_WIDTH]
        vctx_ref[dst:dst + n, :] = z[:, OFF_VA:OFF_VA + KV_WIDTH]

    def block(z_src, out_ref, r0, first_row, row_keep=None):
        rows_c = slice(r0, r0 + CHUNK)
        win = slice(r0, r0 + KEY_PAD)
        valid = (col < s_keys) & (col + (first_row - ctx) >= lead)

        def emit(lo):
            def store(v):
                out_ref[rows_c, lo:lo + v.shape[1]] = v
            return store

        return _mixer_stages(lambda lo, hi: z_src[rows_c, lo:hi], lambda: kctx_ref[win, :], lambda: vctx_ref[win, :],
                             valid, fill_ref, state_ref, tb, row_keep, emit(0), emit(ATTN_WIDTH))

    @pl.when(g == 0)
    def _():
        for r in (kctx_ref, vctx_ref):
            r[...] = jnp.zeros(r.shape, BF16)
        state_ref[...] = jnp.zeros(state_ref.shape, F32)
        _fill_table(sinks_ref, fill_ref, CHUNK, s_keys)
        zm = zmeta_ref[...]
        put_keys(ctx, zm, CHUNK)
        row_keep = (lax.broadcasted_iota(jnp.int32, (CHUNK, 1), 0) >= lead).astype(F32)
        _run_blocks([block(zmeta_ref, mixmeta_ref, 0, 0, row_keep)])
        put_keys(ctx - CHUNK, zm, CHUNK)

    put_keys(ctx, z_ref[...], rows)
    tile_row = (1 + g * tile_chunks) * CHUNK
    _run_blocks([block(z_ref, mix_ref, c * CHUNK, tile_row + c * CHUNK) for c in range(tile_chunks)])

    for r in (kctx_ref, vctx_ref):
        r[0:ctx, :] = r[rows:rows + ctx, :]

    @pl.when(g == n_g - 1)
    def _():
        klast_ref[0] = kctx_ref[0:ctx, :].astype(F32)
        vlast_ref[0] = vctx_ref[0:ctx, :].astype(F32)
        _state_to_heads(state_ref, sfin_ref)


def _prompt_mixer(z, sinks, tables, batch, seq, n_main, lead):
    n_chunks = seq // CHUNK
    tile_chunks = _largest_divisor(n_chunks, MAX_TILE_CHUNKS)
    n_g = n_chunks // tile_chunks
    rows = tile_chunks * CHUNK
    ctx = WINDOW_CHUNKS * CHUNK
    assert rows >= ctx and n_main % CHUNK == 0
    meta_block0 = n_main // CHUNK
    tabs = [tables[k] for k in _TABLE_ORDER]
    state_shape = (RET_HEADS, RET_DK, RET_DV)
    ctx_rows = ctx + rows + KEY_PAD - (ctx + CHUNK)
    ctx_buf = pltpu.VMEM((ctx_rows, KV_WIDTH), BF16)
    return pl.pallas_call(
        functools.partial(_prompt_mixer_kernel, tile_chunks=tile_chunks, lead=lead),
        grid=(batch, n_g),
        in_specs=[pl.BlockSpec(memory_space=pltpu.SMEM),
                  pl.BlockSpec((rows, IN_WIDTH), lambda b, g: (b * n_g + g, 0)),
                  pl.BlockSpec((CHUNK, IN_WIDTH), lambda b, g: (meta_block0 + b, 0))]
                 + [_resident(t.shape) for t in tabs],
        out_specs=[pl.BlockSpec((rows, MIX_WIDTH), lambda b, g: (b * n_g + g, 0)),
                   pl.BlockSpec((CHUNK, MIX_WIDTH), lambda b, g: (b, 0)),
                   pl.BlockSpec((1, ctx, KV_WIDTH), lambda b, g: (b, 0, 0)),
                   pl.BlockSpec((1, ctx, KV_WIDTH), lambda b, g: (b, 0, 0)),
                   pl.BlockSpec((1,) + state_shape, lambda b, g: (b, 0, 0, 0))],
        out_shape=[jax.ShapeDtypeStruct((n_main, MIX_WIDTH), BF16),
                   jax.ShapeDtypeStruct((batch * CHUNK, MIX_WIDTH), BF16),
                   jax.ShapeDtypeStruct((batch, ctx, KV_WIDTH), F32),
                   jax.ShapeDtypeStruct((batch, ctx, KV_WIDTH), F32),
                   jax.ShapeDtypeStruct((batch,) + state_shape, F32)],
        scratch_shapes=[ctx_buf, ctx_buf,
                        pltpu.VMEM((ATTN_KV_HEADS, GROUP * CHUNK, KEY_PAD), F32),
                        pltpu.VMEM((N_RG, RET_DK, RGW), F32)],
        compiler_params=_compiler_params(2),
        name="prompt_mixer",
    )(sinks, z, z, *tabs)


def _sample_mixer_kernel(sinks_ref, z_ref, ck_ref, cv_ref, s0_ref, *rest, cache_rows):
    tb = dict(zip(_TABLE_ORDER, rest[:len(_TABLE_ORDER)]))
    mix_ref, knew_ref, vnew_ref, snew_ref, fill_ref, state_ref = rest[len(_TABLE_ORDER):]
    z = z_ref[...]
    t = z.shape[0]
    s_keys = cache_rows + t
    pad = jnp.zeros((KEY_PAD - s_keys, KV_WIDTH), F32)
    k_all = jnp.concatenate([ck_ref[0], z[:, OFF_KA:OFF_KA + KV_WIDTH].astype(F32)], axis=0)
    v_all = jnp.concatenate([cv_ref[0], z[:, OFF_VA:OFF_VA + KV_WIDTH].astype(F32)], axis=0)
    windows = [jnp.concatenate([a, pad], axis=0).astype(BF16) for a in (k_all, v_all)]
    _fill_table(sinks_ref, fill_ref, t, s_keys)
    for g in range(N_RG):
        for hh in range(RG):
            state_ref[g, :, hh * RET_DV:(hh + 1) * RET_DV] = s0_ref[0, g * RG + hh]
    col = lax.broadcasted_iota(jnp.int32, (1, KEY_PAD), 1)

    def emit(lo):
        def store(v):
            mix_ref[:, lo:lo + v.shape[1]] = v
        return store

    _run_blocks([_mixer_stages(lambda lo, hi: z_ref[:, lo:hi], lambda: windows[0], lambda: windows[1], col < s_keys,
                               fill_ref, state_ref, tb, None, emit(0), emit(ATTN_WIDTH))])
    knew_ref[0] = k_all[s_keys - cache_rows:]
    vnew_ref[0] = v_all[s_keys - cache_rows:]
    _state_to_heads(state_ref, snew_ref)


def _sample_mixer(z, sinks, tables, cache_k, cache_v, state, batch, t, row_offset):
    cache_rows = cache_k.shape[1]
    assert row_offset % t == 0 and cache_rows + t < KEY_PAD
    first_block = row_offset // t
    tabs = [tables[k] for k in _TABLE_ORDER]
    state_shape = (RET_HEADS, RET_DK, RET_DV)
    cache_spec = pl.BlockSpec((1, cache_rows, KV_WIDTH), lambda b: (b, 0, 0))
    state_spec = pl.BlockSpec((1,) + state_shape, lambda b: (b, 0, 0, 0))
    return pl.pallas_call(
        functools.partial(_sample_mixer_kernel, cache_rows=cache_rows),
        grid=(batch,),
        in_specs=[pl.BlockSpec(memory_space=pltpu.SMEM),
                  pl.BlockSpec((t, IN_WIDTH), lambda b: (first_block + b, 0)),
                  cache_spec, cache_spec, state_spec] + [_resident(x.shape) for x in tabs],
        out_specs=[pl.BlockSpec((t, MIX_WIDTH), lambda b: (b, 0)), cache_spec, cache_spec, state_spec],
        out_shape=[jax.ShapeDtypeStruct((batch * t, MIX_WIDTH), BF16),
                   jax.ShapeDtypeStruct(cache_k.shape, F32),
                   jax.ShapeDtypeStruct(cache_v.shape, F32),
                   jax.ShapeDtypeStruct(state.shape, F32)],
        scratch_shapes=[pltpu.VMEM((ATTN_KV_HEADS, GROUP * t, KEY_PAD), F32),
                        pltpu.VMEM((N_RG, RET_DK, RGW), F32)],
        compiler_params=_compiler_params(1),
        name="sample_mixer",
    )(sinks, z, cache_k, cache_v, state, *tabs)


def kernel(x_prompt, x_sample, cache_swa_k, cache_swa_v, state_ret, meta_tokens, w_in, w_out, attn_sinks,
           ffn1_w_in, ffn1_w_out, ffn2_w_in, ffn2_w_out, norm_ffn1_pre, norm_ffn1_post, norm_mix_pre,
           norm_mix_post, norm_ffn2_pre, norm_ffn2_post, final_norm):
    batch, seq, d = x_prompt.shape
    dec_batch, dec_seq, _ = x_sample.shape
    depth = w_in.shape[0]
    cache_rows = cache_swa_k.shape[2]
    lead = CHUNK - N_META
    n_main, n_meta, n_sample = batch * seq, batch * CHUNK, dec_batch * dec_seq
    rows = (n_main, n_meta, n_sample)
    assert w_in.shape[2] == IN_WIDTH and w_out.shape[1] == MIX_WIDTH
    assert seq % CHUNK == 0 and cache_rows == WINDOW_CHUNKS * CHUNK and seq >= cache_rows

    meta_chunk = jnp.concatenate([jnp.zeros((lead, d), x_prompt.dtype), meta_tokens.astype(x_prompt.dtype)], axis=0)
    x_tail = jnp.concatenate([jnp.broadcast_to(meta_chunk, (batch, CHUNK, d)).reshape(n_meta, d),
                              x_sample.reshape(n_sample, d)], axis=0)
    x = (x_prompt.reshape(n_main, d), x_tail)

    tables_p = _mixer_tables(CHUNK, WINDOW_CHUNKS * CHUNK + CHUNK)
    tables_s = _mixer_tables(dec_seq, cache_rows + dec_seq)
    cache_k = cache_swa_k.reshape(depth, dec_batch, cache_rows, KV_WIDTH)
    cache_v = cache_swa_v.reshape(depth, dec_batch, cache_rows, KV_WIDTH)

    pk, pv, ps, sk, sv, ss = [], [], [], [], [], []
    for l in range(depth):
        ffn1 = (norm_ffn1_pre[l], ffn1_w_in[l].astype(BF16), ffn1_w_out[l].astype(BF16), norm_ffn1_post[l])
        ffn2 = (norm_ffn2_pre[l], ffn2_w_in[l].astype(BF16), ffn2_w_out[l].astype(BF16), norm_ffn2_post[l])
        w_proj = _projection_weight(w_in[l])
        x, z = _token_step(x, ffn1, rows, proj=(norm_mix_pre[l], w_proj))
        mix_main, mix_meta, k_p, v_p, s_p = _prompt_mixer(z, attn_sinks[l], tables_p, batch, seq, n_main, lead)
        mix_s, k_s, v_s, s_s = _sample_mixer(z, attn_sinks[l], tables_s, cache_k[l], cache_v[l], state_ret[l],
                                             dec_batch, dec_seq, n_main + n_meta)
        mix = (mix_main, mix_meta, mix_s, w_out[l].astype(BF16), norm_mix_post[l])
        x = _token_step(x, ffn2, rows, mix=mix, final=final_norm if l == depth - 1 else None)
        pk.append(k_p); pv.append(v_p); ps.append(s_p)
        sk.append(k_s); sv.append(v_s); ss.append(s_s)

    y_main, y_tail = x
    y_prompt = y_main.reshape(batch, seq, d)
    y_sample = y_tail[n_meta:].reshape(dec_batch, dec_seq, d)
    kv_shape_p = (depth, batch, cache_rows, ATTN_KV_HEADS, HEAD_DIM)
    kv_shape_s = (depth, dec_batch, cache_rows, ATTN_KV_HEADS, HEAD_DIM)
    return (y_prompt, y_sample,
            jnp.stack(pk).reshape(kv_shape_p), jnp.stack(pv).reshape(kv_shape_p), jnp.stack(ps),
            jnp.stack(sk).reshape(kv_shape_s), jnp.stack(sv).reshape(kv_shape_s), jnp.stack(ss))
```

```python
import functools
import math

import jax
import jax.numpy as jnp
import numpy as np
from jax import lax
from jax.experimental import pallas as pl
from jax.experimental.pallas import tpu as pltpu

CHUNK = 64
N_META = 16
WINDOW_CHUNKS = 2
ATTN_HEADS = 8
ATTN_KV_HEADS = 2
HEAD_DIM = 64
GROUP = ATTN_HEADS // ATTN_KV_HEADS
ATTN_SCALE = HEAD_DIM ** -0.5
RET_HEADS = 8
RET_DK = 64
RET_DV = 64
ATTN_WIDTH = ATTN_HEADS * HEAD_DIM
KV_WIDTH = ATTN_KV_HEADS * HEAD_DIM
RET_WIDTH = RET_HEADS * RET_DK
ROPE_BASE = 10000.0
EPS = 1e-6
NEG = -1e30

OFF_QA = 0
OFF_KA = OFF_QA + ATTN_WIDTH
OFF_VA = OFF_KA + KV_WIDTH
OFF_QR = OFF_VA + KV_WIDTH
OFF_KR = OFF_QR + RET_WIDTH
OFF_VR = OFF_KR + RET_WIDTH
OFF_GR = OFF_VR + RET_WIDTH
IN_WIDTH = OFF_GR + RET_WIDTH
MIX_WIDTH = ATTN_WIDTH + RET_WIDTH

LANES = 128
HALF = RET_DK // 2
RG = 4
RGW = RG * RET_DK
N_RG = RET_HEADS // RG
KEY_PAD = 256
V7X_VMEM_LIMIT_BYTES = 56 * 1024 * 1024
FF_CHUNK = 256
MAX_ROW_TILE = 512
ROW_SPLIT = 2
MAX_TILE_CHUNKS = 8

BF16 = jnp.bfloat16
F32 = jnp.float32

assert RGW == 2 * LANES and KV_WIDTH == LANES and GROUP == 4 and ATTN_KV_HEADS == 2


def _row_tile(n, cap=MAX_ROW_TILE):
    t = cap
    while t > 8 and n % t:
        t //= 2
    assert n % t == 0, (n, t)
    return t


def _largest_divisor(n, cap):
    return max(d for d in range(1, cap + 1) if n % d == 0)


def _rms(x):
    return x * lax.rsqrt(jnp.mean(x * x, axis=-1, keepdims=True) + EPS)


def _silu(x):
    return x / (1.0 + jnp.exp(-x))


def _compiler_params(n_axes):
    return pltpu.CompilerParams(dimension_semantics=("arbitrary",) * n_axes,
                                vmem_limit_bytes=V7X_VMEM_LIMIT_BYTES)


def _resident(shape):
    return pl.BlockSpec(shape, lambda *_: (0,) * len(shape), pipeline_mode=pl.Buffered(1))


def _row_halves(tm):
    n_split = ROW_SPLIT if tm % (ROW_SPLIT * 16) == 0 else 1
    return [slice(r * tm // n_split, (r + 1) * tm // n_split) for r in range(n_split)]


def _token_units(x_of, mix_ref, p, o_ref, z_ref, h_ref, xn_ref, ff_chunk):
    tm, d_ff = h_ref.shape
    halves = _row_halves(tm)
    if mix_ref is not None:
        for rows in halves:
            y = jnp.dot(mix_ref[rows, :], p["wmix"][...], preferred_element_type=F32)
            o_ref[rows, :] = x_of(rows) + _rms(y) * p["gmix"][...]
            yield
        resid = lambda rows: o_ref[rows, :]
    else:
        resid = x_of
    for rows in halves:
        xn_ref[rows, :] = (_rms(resid(rows)) * p["gpre"][...]).astype(BF16)
    for j in range(d_ff // ff_chunk):
        lo = j * ff_chunk
        xn = xn_ref[...]
        gate = jnp.dot(xn, p["win"][:, lo:lo + ff_chunk], preferred_element_type=F32)
        yield
        up = jnp.dot(xn, p["win"][:, d_ff + lo:d_ff + lo + ff_chunk], preferred_element_type=F32)
        h_ref[:, lo:lo + ff_chunk] = (_silu(gate) * up).astype(BF16)
        yield
    ys = []
    for rows in halves:
        ys.append(jnp.dot(h_ref[rows, :], p["wout"][...], preferred_element_type=F32))
        yield
    for rows, y in zip(halves, ys):
        xr = resid(rows) + 0.5 * (_rms(y) * p["gpost"][...])
        if z_ref is not None:
            xn = (_rms(xr) * p["gproj"][...]).astype(BF16)
            z_ref[rows, :] = jnp.dot(xn, p["wproj"][...], preferred_element_type=F32).astype(BF16)
        o_ref[rows, :] = _rms(xr) * p["gfin"][...] if "gfin" in p else xr
        yield


def _n_token_units(tm, d_ff, ff_chunk, has_mix):
    return (3 if has_mix else 2) * len(_row_halves(tm)) + 2 * (d_ff // ff_chunk)


def _token_kernel(*refs, names, ff_chunk, n_main_tiles, split_in, has_mix, has_proj):
    refs = list(refs)
    take = lambda n: [refs.pop(0) for _ in range(n)]
    if split_in:
        xm_ref, xt_ref = take(2)
        in_main = pl.program_id(0) < n_main_tiles
        x_of = lambda rows: jnp.where(in_main, xm_ref[rows, :], xt_ref[rows, :])
    else:
        (x_ref,) = take(1)
        x_of = lambda rows: x_ref[rows, :]
    mix_ref = take(1)[0] if has_mix else None
    p = dict(zip(names, take(len(names))))
    (o_ref,) = take(1)
    z_ref = take(1)[0] if has_proj else None
    h_ref, xn_ref = take(2)
    assert not refs
    for _ in _token_units(x_of, mix_ref, p, o_ref, z_ref, h_ref, xn_ref, ff_chunk):
        pass


def _ffn_chunk(d_ff):
    return FF_CHUNK if d_ff % FF_CHUNK == 0 else d_ff


def _weight_args(d, ffn, mix=None, proj=None, final=None):
    g_pre, w_in, w_out, g_post = ffn
    named = []
    if mix is not None:
        named += [("wmix", mix[0]), ("gmix", mix[1].reshape(1, d))]
    named += [("gpre", g_pre.reshape(1, d)), ("win", w_in), ("wout", w_out), ("gpost", g_post.reshape(1, d))]
    if proj is not None:
        named += [("gproj", proj[0].reshape(1, d)), ("wproj", proj[1])]
    if final is not None:
        named.append(("gfin", final.reshape(1, d)))
    names, arrays = zip(*named)
    return names, list(arrays), [_resident(a.shape) for a in arrays]


def _token_step(x, ffn, tm, *, n_out, first_tile=0, mix=None, proj=None, final=None):
    split_in = isinstance(x, tuple)
    d = x[0].shape[1] if split_in else x.shape[1]
    d_ff = ffn[2].shape[0]
    n_main_tiles = x[0].shape[0] // tm if split_in else 0
    args, in_specs = [], []
    if split_in:
        assert x[0].shape[0] % tm == 0 and x[1].shape[0] % tm == 0
        args += list(x)
        in_specs += [pl.BlockSpec((tm, d), lambda i: (jnp.minimum(i, n_main_tiles - 1), 0)),
                     pl.BlockSpec((tm, d), lambda i: (jnp.maximum(i - n_main_tiles, 0), 0))]
    else:
        args.append(x)
        in_specs.append(pl.BlockSpec((tm, d), lambda i: (first_tile + i, 0)))
    if mix is not None:
        args.append(mix[0])
        in_specs.append(pl.BlockSpec((tm, MIX_WIDTH), lambda i: (i, 0)))
    names, w_arrays, w_specs = _weight_args(d, ffn, mix[1:] if mix is not None else None, proj, final)
    out_shape = [jax.ShapeDtypeStruct((n_out, d), F32)]
    out_specs = [pl.BlockSpec((tm, d), lambda i: (i, 0))]
    if proj is not None:
        out_shape.append(jax.ShapeDtypeStruct((n_out, IN_WIDTH), BF16))
        out_specs.append(pl.BlockSpec((tm, IN_WIDTH), lambda i: (i, 0)))
    outs = pl.pallas_call(
        functools.partial(_token_kernel, names=names, ff_chunk=_ffn_chunk(d_ff), n_main_tiles=n_main_tiles,
                          split_in=split_in, has_mix=mix is not None, has_proj=proj is not None),
        grid=(n_out // tm,),
        in_specs=in_specs + w_specs,
        out_specs=out_specs,
        out_shape=out_shape,
        scratch_shapes=[pltpu.VMEM((tm, d_ff), BF16), pltpu.VMEM((tm, d), BF16)],
        compiler_params=_compiler_params(1),
        name="token_step",
    )(*args, *w_arrays)
    return outs[0] if len(outs) == 1 else tuple(outs)


def _permute_ret_columns(w):
    d = w.shape[0]
    return w.reshape(d, N_RG, RG, 2, HALF).transpose(0, 1, 3, 2, 4).reshape(d, RET_WIDTH)


def _projection_weight(w):
    parts = [w[:, :OFF_QR], _permute_ret_columns(w[:, OFF_QR:OFF_KR]), _permute_ret_columns(w[:, OFF_KR:OFF_VR]),
             w[:, OFF_VR:]]
    return jnp.concatenate(parts, axis=1).astype(BF16)


def _mixer_tables(t, s_keys):
    log_g = jnp.log1p(-jnp.exp2(-5.0 - jnp.arange(RET_HEADS, dtype=F32)))
    freqs = ROPE_BASE ** (-jnp.arange(HALF, dtype=F32) / HALF)
    pos = jnp.arange(t, dtype=F32)
    ang = pos[:, None] * freqs[None, :]
    cos, sin = jnp.tile(jnp.cos(ang), (1, RG)), jnp.tile(jnp.sin(ang), (1, RG))
    qs = RET_DK ** -0.5
    rot = jnp.stack([cos * qs, sin * qs, cos, sin])

    head = lambda g: slice(g * RG, (g + 1) * RG)
    diff = pos[:, None] - pos[None, :]
    intra = jnp.where(diff >= 0, jnp.exp(log_g[:, None, None] * jnp.maximum(diff, 0.0)), 0.0)
    kv_dec = jnp.exp((t - pos)[:, None] * log_g[None, :])
    q_dec = jnp.exp(pos[:, None] * log_g[None, :])
    dec = []
    for g in range(N_RG):
        kd = jnp.tile(jnp.repeat(kv_dec[:, head(g)], HALF, axis=1), (1, 2))
        qd = jnp.repeat(q_dec[:, head(g)], RET_DV, axis=1)
        dec += [kd, qd]
    dec = jnp.stack(dec)
    dtab = jnp.stack([jnp.concatenate(list(intra[head(g)]), axis=1) for g in range(N_RG)])

    g_t = jnp.exp(t * log_g)
    shift = -t * freqs
    stab = [jnp.broadcast_to(jnp.repeat(g_t[head(g)], RET_DV)[None, :], (HALF, RGW)) for g in range(N_RG)]
    stab += [jnp.broadcast_to(jnp.cos(shift)[:, None], (HALF, RGW)),
             jnp.broadcast_to(jnp.sin(shift)[:, None], (HALF, RGW))]
    stab = jnp.stack(stab)

    r4 = np.arange(RG * t)[:, None] // t
    lane = np.arange(RGW)[None, :]
    mask_k = jnp.asarray(r4 == (lane % LANES) // HALF, BF16)
    mask_v = jnp.asarray(r4 == lane // RET_DV, BF16)
    rs = np.arange(LANES)[:, None] // HALF
    mask_s = jnp.asarray(rs == lane // RET_DV, F32)
    seg = jnp.asarray(np.arange(RGW)[:, None] // RET_DV == lane // RET_DV, BF16)

    rowk = np.arange(KEY_PAD)[:, None] < s_keys
    lane_v = np.arange(LANES)[None, :] // HEAD_DIM
    v_keep = jnp.stack([jnp.asarray(rowk & (lane_v == p), BF16) for p in range(2)])
    v_ones = jnp.stack([jnp.asarray(np.broadcast_to(lane_v != p, (KEY_PAD, LANES)), BF16) for p in range(2)])
    return dict(rot=rot, dec=dec, dtab=dtab, stab=stab, mask_k=mask_k, mask_v=mask_v, mask_s=mask_s, seg=seg,
                v_keep=v_keep, v_ones=v_ones)


_TABLE_ORDER = ("rot", "dec", "dtab", "stab", "mask_k", "mask_v", "mask_s", "seg", "v_keep", "v_ones")


def _fill_table(sinks_ref, fill_ref, t, s_keys):
    col = lax.broadcasted_iota(jnp.int32, (t, KEY_PAD), 1)
    for kv in range(ATTN_KV_HEADS):
        for j in range(GROUP):
            fill_ref[kv, j * t:(j + 1) * t, :] = jnp.where(col == s_keys, sinks_ref[kv * GROUP + j], NEG)


def _attn_scores(q, kwin):
    t = q.shape[0]
    lower = lax.broadcasted_iota(jnp.int32, (t, LANES), 1) < HEAD_DIM
    out = []
    for kv in range(ATTN_KV_HEADS):
        keep = lower if kv == 0 else ~lower
        parts = []
        for j in range(2):
            blk = q[:, (2 * kv + j) * LANES:(2 * kv + j + 1) * LANES].astype(F32) * ATTN_SCALE
            rolled = pltpu.roll(blk, HEAD_DIM, axis=1)
            lo_head, hi_head = (blk, rolled) if kv == 0 else (rolled, blk)
            parts += [jnp.where(keep, lo_head, 0.0), jnp.where(keep, hi_head, 0.0)]
        qs = jnp.concatenate(parts, axis=0).astype(BF16)
        out.append(lax.dot_general(qs, kwin, (((1,), (1,)), ((), ())), preferred_element_type=F32))
    return out


def _attn_values(scores, vwin, valid, fill_ref, tb):
    out = []
    for kv, s in enumerate(scores):
        s = jnp.where(valid, s, fill_ref[kv])
        p = jnp.exp(s - jnp.max(s, axis=-1, keepdims=True))
        vext = vwin * tb["v_keep"][kv] + tb["v_ones"][kv]
        out.append(jnp.dot(p.astype(BF16), vext, preferred_element_type=F32))
    return out


def _attn_normalise(results, t):
    lower = lax.broadcasted_iota(jnp.int32, (t, LANES), 1) < HEAD_DIM
    cols = []
    for kv, res in enumerate(results):
        swapped = pltpu.roll(res, HEAD_DIM, axis=1)
        for j in range(2):
            lo_rows, hi_rows = slice(2 * j * t, (2 * j + 1) * t), slice((2 * j + 1) * t, (2 * j + 2) * t)
            if kv == 0:
                lo, hi = res[lo_rows] / swapped[lo_rows], swapped[hi_rows] / res[hi_rows]
            else:
                lo, hi = swapped[lo_rows] / res[lo_rows], res[hi_rows] / swapped[hi_rows]
            cols.append(jnp.where(lower, lo, hi))
    return jnp.concatenate(cols, axis=-1)


def _ret_scores(zcols, g, tb, row_keep):
    lo = g * RGW
    cq, sq, ck, sk = tb["rot"][0], tb["rot"][1], tb["rot"][2], tb["rot"][3]
    q1 = zcols(OFF_QR + lo, OFF_QR + lo + LANES).astype(F32)
    q2 = zcols(OFF_QR + lo + LANES, OFF_QR + lo + RGW).astype(F32)
    k1 = zcols(OFF_KR + lo, OFF_KR + lo + LANES).astype(F32)
    k2 = zcols(OFF_KR + lo + LANES, OFF_KR + lo + RGW).astype(F32)
    if row_keep is not None:
        k1, k2 = k1 * row_keep, k2 * row_keep
    q_rot = jnp.concatenate([q1 * cq - q2 * sq, q2 * cq + q1 * sq], axis=-1)
    k_rot = jnp.concatenate([k1 * ck - k2 * sk, k2 * ck + k1 * sk], axis=-1)
    qb, kb = q_rot.astype(BF16), k_rot.astype(BF16)
    kdb = (k_rot * tb["dec"][2 * g]).astype(BF16)
    vb = zcols(OFF_VR + lo, OFF_VR + lo + RGW)
    k_bd = jnp.concatenate([kb] * RG, axis=0) * tb["mask_k"][...]
    scores = lax.dot_general(qb, k_bd, (((1,), (1,)), ((), ())), preferred_element_type=F32)
    w_full = lax.dot_general(kdb, vb, (((0,), (0,)), ((), ())), preferred_element_type=F32)
    return qb, vb, scores, w_full


def _ret_outputs(g, qb, vb, scores, w_full, state_ref, tb):
    mask_s = tb["mask_s"][...]
    v_bd = jnp.concatenate([vb] * RG, axis=0) * tb["mask_v"][...]
    intra = jnp.dot((scores * tb["dtab"][g]).astype(BF16), v_bd, preferred_element_type=F32)
    c1, c2 = state_ref[g, 0:HALF, :], state_ref[g, HALF:2 * HALF, :]
    mask_sb = mask_s.astype(BF16)
    s_bd = jnp.concatenate([jnp.concatenate([c.astype(BF16)] * RG, axis=0) * mask_sb for c in (c1, c2)], axis=0)
    cross = jnp.dot(qb, s_bd, preferred_element_type=F32)
    w = []
    for half in range(2):
        wm = w_full[half * LANES:(half + 1) * LANES] * mask_s
        w.append(wm[0:HALF] + wm[HALF:2 * HALF] + wm[2 * HALF:3 * HALF] + wm[3 * HALF:4 * HALF])
    cos_s, sin_s = tb["stab"][N_RG], tb["stab"][N_RG + 1]
    a1 = tb["stab"][g] * c1 + w[0]
    a2 = tb["stab"][g] * c2 + w[1]
    state_ref[g, 0:HALF, :] = a1 * cos_s - a2 * sin_s
    state_ref[g, HALF:2 * HALF, :] = a2 * cos_s + a1 * sin_s
    return intra, cross


def _ret_square_sums(outs, tb):
    parts = []
    for o in outs:
        sq = o * o
        hi = sq.astype(BF16)
        parts += [hi, (sq - hi.astype(F32)).astype(BF16)]
    return jnp.dot(jnp.concatenate(parts, axis=0), tb["seg"][...], preferred_element_type=F32)


def _mixer_stages(zcols, kwin, vwin, valid, fill_ref, state_ref, tb, row_keep, emit_attn, emit_ret):
    attn_scores = _attn_scores(zcols(OFF_QA, OFF_QA + ATTN_WIDTH), kwin())
    ret = [_ret_scores(zcols, g, tb, row_keep) for g in range(N_RG)]
    t = ret[0][0].shape[0]
    yield
    attn_res = _attn_values(attn_scores, vwin(), valid, fill_ref, tb)
    pairs = [_ret_outputs(g, *ret[g], state_ref, tb) for g in range(N_RG)]
    yield
    emit_attn(_attn_normalise(attn_res, t).astype(BF16))
    outs = [intra + cross * tb["dec"][2 * g + 1] for g, (intra, cross) in enumerate(pairs)]
    ssq = _ret_square_sums(outs, tb)
    yield
    res = []
    for g, o in enumerate(outs):
        gate = zcols(OFF_GR + g * RGW, OFF_GR + (g + 1) * RGW).astype(F32)
        ss = ssq[2 * g * t:(2 * g + 1) * t] + ssq[(2 * g + 1) * t:(2 * g + 2) * t]
        res.append(o * lax.rsqrt(ss * (1.0 / RET_DV) + EPS) * _silu(gate))
    emit_ret(jnp.concatenate(res, axis=-1).astype(BF16))
    yield


N_STAGES = 4
FIRST_EMIT_STAGE = 2


def _stage_order(n_blocks):
    order = []
    for step in range(n_blocks + N_STAGES - 1):
        order += [step - stage for stage in reversed(range(N_STAGES)) if 0 <= step - stage < n_blocks]
    return order


def _run_blocks(blocks):
    for b in _stage_order(len(blocks)):
        next(blocks[b])


def _state_to_heads(state_ref, out_ref):
    for g in range(N_RG):
        for hh in range(RG):
            out_ref[0, g * RG + hh] = state_ref[g, :, hh * RET_DV:(hh + 1) * RET_DV]


def _prompt_kernel(sinks_ref, z_ref, zmeta_ref, x_ref, *rest, names, tile_chunks, n_g, n_tiles, lead, ff_chunk):
    rest = list(rest)
    take = lambda n: [rest.pop(0) for _ in range(n)]
    p = dict(zip(names, take(len(names))))
    tb = dict(zip(_TABLE_ORDER, take(len(_TABLE_ORDER))))
    o_ref, mixmeta_ref, klast_ref, vlast_ref, sfin_ref = take(5)
    mix_ref, h_ref, xn_ref, kctx_ref, vctx_ref, fill_ref, state_ref = take(7)
    assert not rest
    s = pl.program_id(0)
    g = lax.rem(jnp.minimum(s, n_tiles - 1), n_g)
    ctx = WINDOW_CHUNKS * CHUNK
    rows = tile_chunks * CHUNK
    s_keys = ctx + CHUNK
    col = lax.broadcasted_iota(jnp.int32, (1, KEY_PAD), 1)

    def put_keys(dst, z, n):
        kctx_ref[dst:dst + n, :] = z[:, OFF_KA:OFF_KA + KV_WIDTH]
        vctx_ref[dst:dst + n, :] = z[:, OFF_VA:OFF_VA + KV_WIDTH]

    def block(z_src, out_ref, r0, first_row, row_keep=None):
        rows_c = slice(r0, r0 + CHUNK)
        win = slice(r0, r0 + KEY_PAD)
        valid = (col < s_keys) & (col + (first_row - ctx) >= lead)

        def emit(lo):
            def store(v):
                out_ref[rows_c, lo:lo + v.shape[1]] = v
            return store

        return _mixer_stages(lambda lo, hi: z_src[rows_c, lo:hi], lambda: kctx_ref[win, :], lambda: vctx_ref[win, :],
                             valid, fill_ref, state_ref, tb, row_keep, emit(0), emit(ATTN_WIDTH))

    @pl.when(s == 0)
    def _():
        mix_ref[...] = jnp.zeros(mix_ref.shape, BF16)

    @pl.when(g == 0)
    def _():
        for r in (kctx_ref, vctx_ref):
            r[...] = jnp.zeros(r.shape, BF16)
        state_ref[...] = jnp.zeros(state_ref.shape, F32)
        _fill_table(sinks_ref, fill_ref, CHUNK, s_keys)
        zm = zmeta_ref[...]
        put_keys(ctx, zm, CHUNK)
        row_keep = (lax.broadcasted_iota(jnp.int32, (CHUNK, 1), 0) >= lead).astype(F32)
        _run_blocks([block(zmeta_ref, mixmeta_ref, 0, 0, row_keep)])
        put_keys(ctx - CHUNK, zm, CHUNK)

    units = _token_units(lambda r: x_ref[r, :], mix_ref, p, o_ref, None, h_ref, xn_ref, ff_chunk)
    n_units = _n_token_units(rows, h_ref.shape[1], ff_chunk, True)
    n_mix_reads = len(_row_halves(rows))

    put_keys(ctx, z_ref[...], rows)
    tile_row = (1 + g * tile_chunks) * CHUNK
    blocks = [block(z_ref, mix_ref, c * CHUNK, tile_row + c * CHUNK) for c in range(tile_chunks)]
    order = _stage_order(len(blocks))
    first_emit = order.index(0, FIRST_EMIT_STAGE)
    assert order[:first_emit].count(0) == FIRST_EMIT_STAGE
    issued = 0
    for i, b in enumerate(order):
        if i == first_emit:
            while issued < n_mix_reads:
                next(units)
                issued += 1
        next(blocks[b])
        target = max(issued, -(-n_units * (i + 1) // len(order)))
        for _ in range(target - issued):
            next(units)
        issued = target
    assert next(units, None) is None

    for r in (kctx_ref, vctx_ref):
        r[0:ctx, :] = r[rows:rows + ctx, :]

    @pl.when((g == n_g - 1) & (s < n_tiles))
    def _():
        klast_ref[0] = kctx_ref[0:ctx, :].astype(F32)
        vlast_ref[0] = vctx_ref[0:ctx, :].astype(F32)
        _state_to_heads(state_ref, sfin_ref)


def _prompt_step(z, x, sinks, tables, ffn, mix, batch, seq, lead, final=None):
    d = x.shape[1]
    d_ff = ffn[2].shape[0]
    n_main = batch * seq
    n_chunks = seq // CHUNK
    tile_chunks = _largest_divisor(n_chunks, MAX_TILE_CHUNKS)
    n_g = n_chunks // tile_chunks
    n_tiles = batch * n_g
    rows = tile_chunks * CHUNK
    ctx = WINDOW_CHUNKS * CHUNK
    assert rows >= ctx and n_main % CHUNK == 0
    meta_block0 = n_main // CHUNK
    tabs = [tables[k] for k in _TABLE_ORDER]
    names, w_arrays, w_specs = _weight_args(d, ffn, mix, None, final)
    state_shape = (RET_HEADS, RET_DK, RET_DV)
    ctx_rows = ctx + rows + KEY_PAD - (ctx + CHUNK)
    ctx_buf = pltpu.VMEM((ctx_rows, KV_WIDTH), BF16)
    cur = lambda s: jnp.minimum(s, n_tiles - 1)
    prev = lambda s: jnp.maximum(s - 1, 0)
    seq_of = lambda s: cur(s) // n_g
    return pl.pallas_call(
        functools.partial(_prompt_kernel, names=names, tile_chunks=tile_chunks, n_g=n_g, n_tiles=n_tiles, lead=lead,
                          ff_chunk=_ffn_chunk(d_ff)),
        grid=(n_tiles + 1,),
        in_specs=[pl.BlockSpec(memory_space=pltpu.SMEM),
                  pl.BlockSpec((rows, IN_WIDTH), lambda s: (cur(s), 0)),
                  pl.BlockSpec((CHUNK, IN_WIDTH), lambda s: (meta_block0 + seq_of(s), 0)),
                  pl.BlockSpec((rows, d), lambda s: (prev(s), 0))]
                 + w_specs + [_resident(t.shape) for t in tabs],
        out_specs=[pl.BlockSpec((rows, d), lambda s: (prev(s), 0)),
                   pl.BlockSpec((CHUNK, MIX_WIDTH), lambda s: (seq_of(s), 0)),
                   pl.BlockSpec((1, ctx, KV_WIDTH), lambda s: (seq_of(s), 0, 0)),
                   pl.BlockSpec((1, ctx, KV_WIDTH), lambda s: (seq_of(s), 0, 0)),
                   pl.BlockSpec((1,) + state_shape, lambda s: (seq_of(s), 0, 0, 0))],
        out_shape=[jax.ShapeDtypeStruct((n_main, d), F32),
                   jax.ShapeDtypeStruct((batch * CHUNK, MIX_WIDTH), BF16),
                   jax.ShapeDtypeStruct((batch, ctx, KV_WIDTH), F32),
                   jax.ShapeDtypeStruct((batch, ctx, KV_WIDTH), F32),
                   jax.ShapeDtypeStruct((batch,) + state_shape, F32)],
        scratch_shapes=[pltpu.VMEM((rows, MIX_WIDTH), BF16), pltpu.VMEM((rows, d_ff), BF16),
                        pltpu.VMEM((rows, d), BF16), ctx_buf, ctx_buf,
                        pltpu.VMEM((ATTN_KV_HEADS, GROUP * CHUNK, KEY_PAD), F32),
                        pltpu.VMEM((N_RG, RET_DK, RGW), F32)],
        compiler_params=_compiler_params(1),
        name="prompt_step",
    )(sinks, z, z, x, *w_arrays, *tabs)


def _sample_mixer_kernel(sinks_ref, z_ref, ck_ref, cv_ref, s0_ref, *rest, cache_rows):
    tb = dict(zip(_TABLE_ORDER, rest[:len(_TABLE_ORDER)]))
    mix_ref, knew_ref, vnew_ref, snew_ref, fill_ref, state_ref = rest[len(_TABLE_ORDER):]
    z = z_ref[...]
    t = z.shape[0]
    s_keys = cache_rows + t
    pad = jnp.zeros((KEY_PAD - s_keys, KV_WIDTH), F32)
    k_all = jnp.concatenate([ck_ref[0], z[:, OFF_KA:OFF_KA + KV_WIDTH].astype(F32)], axis=0)
    v_all = jnp.concatenate([cv_ref[0], z[:, OFF_VA:OFF_VA + KV_WIDTH].astype(F32)], axis=0)
    windows = [jnp.concatenate([a, pad], axis=0).astype(BF16) for a in (k_all, v_all)]
    _fill_table(sinks_ref, fill_ref, t, s_keys)
    for g in range(N_RG):
        for hh in range(RG):
            state_ref[g, :, hh * RET_DV:(hh + 1) * RET_DV] = s0_ref[0, g * RG + hh]
    col = lax.broadcasted_iota(jnp.int32, (1, KEY_PAD), 1)

    def emit(lo):
        def store(v):
            mix_ref[:, lo:lo + v.shape[1]] = v
        return store

    _run_blocks([_mixer_stages(lambda lo, hi: z_ref[:, lo:hi], lambda: windows[0], lambda: windows[1], col < s_keys,
                               fill_ref, state_ref, tb, None, emit(0), emit(ATTN_WIDTH))])
    knew_ref[0] = k_all[s_keys - cache_rows:]
    vnew_ref[0] = v_all[s_keys - cache_rows:]
    _state_to_heads(state_ref, snew_ref)


def _sample_mixer(z, sinks, tables, cache_k, cache_v, state, batch, t, row_offset):
    cache_rows = cache_k.shape[1]
    assert row_offset % t == 0 and cache_rows + t < KEY_PAD
    first_block = row_offset // t
    tabs = [tables[k] for k in _TABLE_ORDER]
    state_shape = (RET_HEADS, RET_DK, RET_DV)
    cache_spec = pl.BlockSpec((1, cache_rows, KV_WIDTH), lambda b: (b, 0, 0))
    state_spec = pl.BlockSpec((1,) + state_shape, lambda b: (b, 0, 0, 0))
    return pl.pallas_call(
        functools.partial(_sample_mixer_kernel, cache_rows=cache_rows),
        grid=(batch,),
        in_specs=[pl.BlockSpec(memory_space=pltpu.SMEM),
                  pl.BlockSpec((t, IN_WIDTH), lambda b: (first_block + b, 0)),
                  cache_spec, cache_spec, state_spec] + [_resident(x.shape) for x in tabs],
        out_specs=[pl.BlockSpec((t, MIX_WIDTH), lambda b: (b, 0)), cache_spec, cache_spec, state_spec],
        out_shape=[jax.ShapeDtypeStruct((batch * t, MIX_WIDTH), BF16),
                   jax.ShapeDtypeStruct(cache_k.shape, F32),
                   jax.ShapeDtypeStruct(cache_v.shape, F32),
                   jax.ShapeDtypeStruct(state.shape, F32)],
        scratch_shapes=[pltpu.VMEM((ATTN_KV_HEADS, GROUP * t, KEY_PAD), F32),
                        pltpu.VMEM((N_RG, RET_DK, RGW), F32)],
        compiler_params=_compiler_params(1),
        name="sample_mixer",
    )(sinks, z, cache_k, cache_v, state, *tabs)


def kernel(x_prompt, x_sample, cache_swa_k, cache_swa_v, state_ret, meta_tokens, w_in, w_out, attn_sinks,
           ffn1_w_in, ffn1_w_out, ffn2_w_in, ffn2_w_out, norm_ffn1_pre, norm_ffn1_post, norm_mix_pre,
           norm_mix_post, norm_ffn2_pre, norm_ffn2_post, final_norm):
    batch, seq, d = x_prompt.shape
    dec_batch, dec_seq, _ = x_sample.shape
    depth = w_in.shape[0]
    cache_rows = cache_swa_k.shape[2]
    lead = CHUNK - N_META
    n_main, n_meta, n_sample = batch * seq, batch * CHUNK, dec_batch * dec_seq
    n_tail = n_meta + n_sample
    tm = _row_tile(math.gcd(n_main, n_tail))
    assert w_in.shape[2] == IN_WIDTH and w_out.shape[1] == MIX_WIDTH
    assert seq % CHUNK == 0 and cache_rows == WINDOW_CHUNKS * CHUNK and seq >= cache_rows

    meta_chunk = jnp.concatenate([jnp.zeros((lead, d), x_prompt.dtype), meta_tokens.astype(x_prompt.dtype)], axis=0)
    x_tail = jnp.concatenate([jnp.broadcast_to(meta_chunk, (batch, CHUNK, d)).reshape(n_meta, d),
                              x_sample.reshape(n_sample, d)], axis=0)
    x = (x_prompt.reshape(n_main, d), x_tail)

    tables_p = _mixer_tables(CHUNK, WINDOW_CHUNKS * CHUNK + CHUNK)
    tables_s = _mixer_tables(dec_seq, cache_rows + dec_seq)
    cache_k = cache_swa_k.reshape(depth, dec_batch, cache_rows, KV_WIDTH)
    cache_v = cache_swa_v.reshape(depth, dec_batch, cache_rows, KV_WIDTH)

    pk, pv, ps, sk, sv, ss = [], [], [], [], [], []
    for l in range(depth):
        ffn1 = (norm_ffn1_pre[l], ffn1_w_in[l].astype(BF16), ffn1_w_out[l].astype(BF16), norm_ffn1_post[l])
        ffn2 = (norm_ffn2_pre[l], ffn2_w_in[l].astype(BF16), ffn2_w_out[l].astype(BF16), norm_ffn2_post[l])
        w_proj = _projection_weight(w_in[l])
        mix_w = (w_out[l].astype(BF16), norm_mix_post[l])
        final = final_norm if l == depth - 1 else None
        x1, z = _token_step(x, ffn1, tm, n_out=n_main + n_tail, proj=(norm_mix_pre[l], w_proj))
        x_main, mix_meta, k_p, v_p, s_p = _prompt_step(z, x1, attn_sinks[l], tables_p, ffn2, mix_w, batch, seq, lead,
                                                       final)
        mix_s, k_s, v_s, s_s = _sample_mixer(z, attn_sinks[l], tables_s, cache_k[l], cache_v[l], state_ret[l],
                                             dec_batch, dec_seq, n_main + n_meta)
        x_tail = _token_step(x1, ffn2, tm, n_out=n_tail, first_tile=n_main // tm,
                             mix=(jnp.concatenate([mix_meta, mix_s], axis=0),) + mix_w, final=final)
        x = (x_main, x_tail)
        pk.append(k_p); pv.append(v_p); ps.append(s_p)
        sk.append(k_s); sv.append(v_s); ss.append(s_s)

    y_main, y_tail = x
    y_prompt = y_main.reshape(batch, seq, d)
    y_sample = y_tail[n_meta:].reshape(dec_batch, dec_seq, d)
    kv_shape_p = (depth, batch, cache_rows, ATTN_KV_HEADS, HEAD_DIM)
    kv_shape_s = (depth, dec_batch, cache_rows, ATTN_KV_HEADS, HEAD_DIM)
    return (y_prompt, y_sample,
            jnp.stack(pk).reshape(kv_shape_p), jnp.stack(pv).reshape(kv_shape_p), jnp.stack(ps),
            jnp.stack(sk).reshape(kv_shape_s), jnp.stack(sv).reshape(kv_shape_s), jnp.stack(ss))
```

```python
import functools
import math

import jax
import jax.numpy as jnp
import numpy as np
from jax import lax
from jax.experimental import pallas as pl
from jax.experimental.pallas import tpu as pltpu

CHUNK = 64
N_META = 16
WINDOW_CHUNKS = 2
ATTN_HEADS = 8
ATTN_KV_HEADS = 2
HEAD_DIM = 64
GROUP = ATTN_HEADS // ATTN_KV_HEADS
ATTN_SCALE = HEAD_DIM ** -0.5
LOG2_E = math.log2(math.e)
RET_HEADS = 8
RET_DK = 64
RET_DV = 64
ATTN_WIDTH = ATTN_HEADS * HEAD_DIM
KV_WIDTH = ATTN_KV_HEADS * HEAD_DIM
RET_WIDTH = RET_HEADS * RET_DK
ROPE_BASE = 10000.0
EPS = 1e-6
NEG = -1e30

OFF_QA = 0
OFF_KA = OFF_QA + ATTN_WIDTH
OFF_VA = OFF_KA + KV_WIDTH
OFF_QR = OFF_VA + KV_WIDTH
OFF_KR = OFF_QR + RET_WIDTH
OFF_VR = OFF_KR + RET_WIDTH
OFF_GR = OFF_VR + RET_WIDTH
IN_WIDTH = OFF_GR + RET_WIDTH
MIX_WIDTH = ATTN_WIDTH + RET_WIDTH

LANES = 128
HALF = RET_DK // 2
RG = 4
RGW = RG * RET_DK
N_RG = RET_HEADS // RG
KEY_PAD = 256
V7X_VMEM_LIMIT_BYTES = 56 * 1024 * 1024
FF_CHUNK = 256
MAX_ROW_TILE = 512
ROW_SPLIT = 2
MAX_TILE_CHUNKS = 8

BF16 = jnp.bfloat16
F32 = jnp.float32

assert RGW == 2 * LANES and KV_WIDTH == LANES and GROUP == 4 and ATTN_KV_HEADS == 2


def _row_tile(n, cap=MAX_ROW_TILE):
    t = cap
    while t > 8 and n % t:
        t //= 2
    assert n % t == 0, (n, t)
    return t


def _largest_divisor(n, cap):
    return max(d for d in range(1, cap + 1) if n % d == 0)


def _rms(x):
    return x * lax.rsqrt(jnp.mean(x * x, axis=-1, keepdims=True) + EPS)


def _silu(x):
    return x / (1.0 + jnp.exp(-x))


def _compiler_params(n_axes):
    return pltpu.CompilerParams(dimension_semantics=("arbitrary",) * n_axes,
                                vmem_limit_bytes=V7X_VMEM_LIMIT_BYTES)


def _resident(shape):
    return pl.BlockSpec(shape, lambda *_: (0,) * len(shape), pipeline_mode=pl.Buffered(1))


def _row_halves(tm):
    n_split = ROW_SPLIT if tm % (ROW_SPLIT * 16) == 0 else 1
    return [slice(r * tm // n_split, (r + 1) * tm // n_split) for r in range(n_split)]


def _token_units(x_of, mix_ref, p, o_ref, z_ref, h_ref, xn_ref, ff_chunk):
    tm, d_ff = h_ref.shape
    halves = _row_halves(tm)
    if mix_ref is not None:
        for rows in halves:
            y = jnp.dot(mix_ref[rows, :], p["wmix"][...], preferred_element_type=F32)
            o_ref[rows, :] = x_of(rows) + _rms(y) * p["gmix"][...]
            yield
        resid = lambda rows: o_ref[rows, :]
    else:
        resid = x_of
    xn = jnp.concatenate([(_rms(resid(rows)) * p["gpre"][...]).astype(BF16) for rows in halves], axis=0)
    if xn_ref is not None:
        xn_ref[...] = xn
    for j in range(d_ff // ff_chunk):
        lo = j * ff_chunk
        if xn_ref is not None:
            xn = xn_ref[...]
        gate = jnp.dot(xn, p["win"][:, lo:lo + ff_chunk], preferred_element_type=F32)
        yield
        up = jnp.dot(xn, p["win"][:, d_ff + lo:d_ff + lo + ff_chunk], preferred_element_type=F32)
        h_ref[:, lo:lo + ff_chunk] = (_silu(gate) * up).astype(BF16)
        yield
    ys = []
    for rows in halves:
        ys.append(jnp.dot(h_ref[rows, :], p["wout"][...], preferred_element_type=F32))
        yield
    for rows, y in zip(halves, ys):
        xr = resid(rows) + 0.5 * (_rms(y) * p["gpost"][...])
        if z_ref is not None:
            xn = (_rms(xr) * p["gproj"][...]).astype(BF16)
            z_ref[rows, :] = jnp.dot(xn, p["wproj"][...], preferred_element_type=F32).astype(BF16)
        o_ref[rows, :] = _rms(xr) * p["gfin"][...] if "gfin" in p else xr
        yield


def _n_token_units(tm, d_ff, ff_chunk, has_mix):
    return (3 if has_mix else 2) * len(_row_halves(tm)) + 2 * (d_ff // ff_chunk)


def _token_kernel(*refs, names, ff_chunk, n_main_tiles, split_in, has_mix, has_proj):
    refs = list(refs)
    take = lambda n: [refs.pop(0) for _ in range(n)]
    if split_in:
        xm_ref, xt_ref = take(2)
        in_main = pl.program_id(0) < n_main_tiles
        x_of = lambda rows: jnp.where(in_main, xm_ref[rows, :], xt_ref[rows, :])
    else:
        (x_ref,) = take(1)
        x_of = lambda rows: x_ref[rows, :]
    mix_ref = take(1)[0] if has_mix else None
    p = dict(zip(names, take(len(names))))
    (o_ref,) = take(1)
    z_ref = take(1)[0] if has_proj else None
    (h_ref,) = take(1)
    assert not refs
    for _ in _token_units(x_of, mix_ref, p, o_ref, z_ref, h_ref, None, ff_chunk):
        pass


def _ffn_chunk(d_ff):
    return FF_CHUNK if d_ff % FF_CHUNK == 0 else d_ff


def _weight_args(d, ffn, mix=None, proj=None, final=None):
    g_pre, w_in, w_out, g_post = ffn
    named = []
    if mix is not None:
        named += [("wmix", mix[0]), ("gmix", mix[1].reshape(1, d))]
    named += [("gpre", g_pre.reshape(1, d)), ("win", w_in), ("wout", w_out), ("gpost", g_post.reshape(1, d))]
    if proj is not None:
        named += [("gproj", proj[0].reshape(1, d)), ("wproj", proj[1])]
    if final is not None:
        named.append(("gfin", final.reshape(1, d)))
    names, arrays = zip(*named)
    return names, list(arrays), [_resident(a.shape) for a in arrays]


def _token_step(x, ffn, tm, *, n_out, first_tile=0, mix=None, proj=None, final=None):
    split_in = isinstance(x, tuple)
    d = x[0].shape[1] if split_in else x.shape[1]
    d_ff = ffn[2].shape[0]
    n_main_tiles = x[0].shape[0] // tm if split_in else 0
    args, in_specs = [], []
    if split_in:
        assert x[0].shape[0] % tm == 0 and x[1].shape[0] % tm == 0
        args += list(x)
        in_specs += [pl.BlockSpec((tm, d), lambda i: (jnp.minimum(i, n_main_tiles - 1), 0)),
                     pl.BlockSpec((tm, d), lambda i: (jnp.maximum(i - n_main_tiles, 0), 0))]
    else:
        args.append(x)
        in_specs.append(pl.BlockSpec((tm, d), lambda i: (first_tile + i, 0)))
    if mix is not None:
        args.append(mix[0])
        in_specs.append(pl.BlockSpec((tm, MIX_WIDTH), lambda i: (i, 0)))
    names, w_arrays, w_specs = _weight_args(d, ffn, mix[1:] if mix is not None else None, proj, final)
    out_shape = [jax.ShapeDtypeStruct((n_out, d), F32)]
    out_specs = [pl.BlockSpec((tm, d), lambda i: (i, 0))]
    if proj is not None:
        out_shape.append(jax.ShapeDtypeStruct((n_out, IN_WIDTH), BF16))
        out_specs.append(pl.BlockSpec((tm, IN_WIDTH), lambda i: (i, 0)))
    outs = pl.pallas_call(
        functools.partial(_token_kernel, names=names, ff_chunk=_ffn_chunk(d_ff), n_main_tiles=n_main_tiles,
                          split_in=split_in, has_mix=mix is not None, has_proj=proj is not None),
        grid=(n_out // tm,),
        in_specs=in_specs + w_specs,
        out_specs=out_specs,
        out_shape=out_shape,
        scratch_shapes=[pltpu.VMEM((tm, d_ff), BF16)],
        compiler_params=_compiler_params(1),
        name="token_step",
    )(*args, *w_arrays)
    return outs[0] if len(outs) == 1 else tuple(outs)


def _permute_ret_columns(w):
    lead = w.shape[:-1]
    return jnp.swapaxes(w.reshape(lead + (N_RG, RG, 2, HALF)), -3, -2).reshape(lead + (RET_WIDTH,))


def _projection_weight(w):
    parts = [w[..., :OFF_QR], _permute_ret_columns(w[..., OFF_QR:OFF_KR]),
             _permute_ret_columns(w[..., OFF_KR:OFF_VR]), w[..., OFF_VR:]]
    return jnp.concatenate(parts, axis=-1).astype(BF16)


def _mixer_tables(t, s_keys):
    log_g = jnp.log1p(-jnp.exp2(-5.0 - jnp.arange(RET_HEADS, dtype=F32)))
    freqs = ROPE_BASE ** (-jnp.arange(HALF, dtype=F32) / HALF)
    pos = jnp.arange(t, dtype=F32)
    ang = pos[:, None] * freqs[None, :]
    cos, sin = jnp.tile(jnp.cos(ang), (1, RG)), jnp.tile(jnp.sin(ang), (1, RG))
    qs = RET_DK ** -0.5
    rot = jnp.stack([cos * qs, sin * qs, cos, sin])

    head = lambda g: slice(g * RG, (g + 1) * RG)
    diff = pos[:, None] - pos[None, :]
    intra = jnp.where(diff >= 0, jnp.exp(log_g[:, None, None] * jnp.maximum(diff, 0.0)), 0.0)
    kv_dec = jnp.exp((t - pos)[:, None] * log_g[None, :])
    q_dec = jnp.exp(pos[:, None] * log_g[None, :])
    dec = []
    for g in range(N_RG):
        kd = jnp.tile(jnp.repeat(kv_dec[:, head(g)], HALF, axis=1), (1, 2))
        qd = jnp.repeat(q_dec[:, head(g)], RET_DV, axis=1)
        dec += [kd, qd]
    dec = jnp.stack(dec)
    dtab = jnp.stack([jnp.concatenate(list(intra[head(g)]), axis=1) for g in range(N_RG)])

    g_t = jnp.exp(t * log_g)
    shift = -t * freqs
    stab = [jnp.broadcast_to(jnp.repeat(g_t[head(g)], RET_DV)[None, :], (HALF, RGW)) for g in range(N_RG)]
    stab += [jnp.broadcast_to(jnp.cos(shift)[:, None], (HALF, RGW)),
             jnp.broadcast_to(jnp.sin(shift)[:, None], (HALF, RGW))]
    stab = jnp.stack(stab)

    r4 = np.arange(RG * t)[:, None] // t
    lane = np.arange(RGW)[None, :]
    mask_k = jnp.asarray(r4 == (lane % LANES) // HALF, BF16)
    mask_v = jnp.asarray(r4 == lane // RET_DV, BF16)
    rs = np.arange(LANES)[:, None] // HALF
    mask_s = jnp.asarray(rs == lane // RET_DV, F32)
    seg = jnp.asarray(np.arange(RGW)[:, None] // RET_DV == lane // RET_DV, BF16)

    rowk = np.arange(KEY_PAD)[:, None] < s_keys
    lane_v = np.arange(LANES)[None, :] // HEAD_DIM
    v_keep = jnp.stack([jnp.asarray(rowk & (lane_v == p), BF16) for p in range(2)])
    v_ones = jnp.stack([jnp.asarray(np.broadcast_to(lane_v != p, (KEY_PAD, LANES)), BF16) for p in range(2)])
    return dict(rot=rot, dec=dec, dtab=dtab, stab=stab, mask_k=mask_k, mask_v=mask_v, mask_s=mask_s, seg=seg,
                v_keep=v_keep, v_ones=v_ones)


_TABLE_ORDER = ("rot", "dec", "dtab", "stab", "mask_k", "mask_v", "mask_s", "seg", "v_keep", "v_ones")


def _fill_table(sinks_ref, fill_ref, t, s_keys):
    col = lax.broadcasted_iota(jnp.int32, (t, KEY_PAD), 1)
    for kv in range(ATTN_KV_HEADS):
        for j in range(GROUP):
            fill_ref[kv, j * t:(j + 1) * t, :] = jnp.where(col == s_keys, sinks_ref[kv * GROUP + j] * LOG2_E, NEG)


def _attn_scores(q, kwin):
    t = q.shape[0]
    lower = lax.broadcasted_iota(jnp.int32, (t, LANES), 1) < HEAD_DIM
    out = []
    for kv in range(ATTN_KV_HEADS):
        keep = jnp.where(lower if kv == 0 else ~lower, ATTN_SCALE * LOG2_E, 0.0)
        parts = []
        for j in range(2):
            blk = q[:, (2 * kv + j) * LANES:(2 * kv + j + 1) * LANES].astype(F32)
            rolled = pltpu.roll(blk, HEAD_DIM, axis=1)
            lo_head, hi_head = (blk, rolled) if kv == 0 else (rolled, blk)
            parts += [lo_head * keep, hi_head * keep]
        qs = jnp.concatenate(parts, axis=0).astype(BF16)
        out.append(lax.dot_general(qs, kwin, (((1,), (1,)), ((), ())), preferred_element_type=F32))
    return out


def _attn_values(scores, vwin, valid, fill_ref, tb):
    out = []
    for kv, s in enumerate(scores):
        s = jnp.where(valid, s, fill_ref[kv])
        p = jnp.exp2(s - jnp.max(s, axis=-1, keepdims=True))
        vext = vwin * tb["v_keep"][kv] + tb["v_ones"][kv]
        out.append(jnp.dot(p.astype(BF16), vext, preferred_element_type=F32))
    return out


def _attn_normalise(results, t):
    lower = lax.broadcasted_iota(jnp.int32, (t, LANES), 1) < HEAD_DIM
    cols = []
    for kv, res in enumerate(results):
        swapped = pltpu.roll(res, HEAD_DIM, axis=1)
        for j in range(2):
            lo_rows, hi_rows = slice(2 * j * t, (2 * j + 1) * t), slice((2 * j + 1) * t, (2 * j + 2) * t)
            if kv == 0:
                lo, hi = res[lo_rows] / swapped[lo_rows], swapped[hi_rows] / res[hi_rows]
            else:
                lo, hi = swapped[lo_rows] / res[lo_rows], res[hi_rows] / swapped[hi_rows]
            cols.append(jnp.where(lower, lo, hi))
    return jnp.concatenate(cols, axis=-1)


def _ret_scores(zcols, g, tb, row_keep):
    lo = g * RGW
    cq, sq, ck, sk = tb["rot"][0], tb["rot"][1], tb["rot"][2], tb["rot"][3]
    q1 = zcols(OFF_QR + lo, OFF_QR + lo + LANES).astype(F32)
    q2 = zcols(OFF_QR + lo + LANES, OFF_QR + lo + RGW).astype(F32)
    k1 = zcols(OFF_KR + lo, OFF_KR + lo + LANES).astype(F32)
    k2 = zcols(OFF_KR + lo + LANES, OFF_KR + lo + RGW).astype(F32)
    if row_keep is not None:
        k1, k2 = k1 * row_keep, k2 * row_keep
    q_rot = jnp.concatenate([q1 * cq - q2 * sq, q2 * cq + q1 * sq], axis=-1)
    k_rot = jnp.concatenate([k1 * ck - k2 * sk, k2 * ck + k1 * sk], axis=-1)
    qb, kb = q_rot.astype(BF16), k_rot.astype(BF16)
    kdb = (k_rot * tb["dec"][2 * g]).astype(BF16)
    vb = zcols(OFF_VR + lo, OFF_VR + lo + RGW)
    k_bd = jnp.concatenate([kb] * RG, axis=0) * tb["mask_k"][...]
    scores = lax.dot_general(qb, k_bd, (((1,), (1,)), ((), ())), preferred_element_type=F32)
    w_full = lax.dot_general(kdb, vb, (((0,), (0,)), ((), ())), preferred_element_type=F32)
    return qb, vb, scores, w_full


def _ret_outputs(g, qb, vb, scores, w_full, state_ref, tb):
    mask_s = tb["mask_s"][...]
    v_bd = jnp.concatenate([vb] * RG, axis=0) * tb["mask_v"][...]
    intra = jnp.dot((scores * tb["dtab"][g]).astype(BF16), v_bd, preferred_element_type=F32)
    c1, c2 = state_ref[g, 0:HALF, :], state_ref[g, HALF:2 * HALF, :]
    mask_sb = mask_s.astype(BF16)
    s_bd = jnp.concatenate([jnp.concatenate([c.astype(BF16)] * RG, axis=0) * mask_sb for c in (c1, c2)], axis=0)
    cross = jnp.dot(qb, s_bd, preferred_element_type=F32)
    w = []
    for half in range(2):
        wm = w_full[half * LANES:(half + 1) * LANES] * mask_s
        w.append(wm[0:HALF] + wm[HALF:2 * HALF] + wm[2 * HALF:3 * HALF] + wm[3 * HALF:4 * HALF])
    cos_s, sin_s = tb["stab"][N_RG], tb["stab"][N_RG + 1]
    a1 = tb["stab"][g] * c1 + w[0]
    a2 = tb["stab"][g] * c2 + w[1]
    state_ref[g, 0:HALF, :] = a1 * cos_s - a2 * sin_s
    state_ref[g, HALF:2 * HALF, :] = a2 * cos_s + a1 * sin_s
    return intra, cross


def _ret_square_sums(outs, tb):
    parts = []
    for o in outs:
        sq = o * o
        hi = sq.astype(BF16)
        parts += [hi, (sq - hi.astype(F32)).astype(BF16)]
    return jnp.dot(jnp.concatenate(parts, axis=0), tb["seg"][...], preferred_element_type=F32)


def _mixer_stages(zcols, kwin, vwin, valid, fill_ref, state_ref, tb, row_keep, emit_attn, emit_ret):
    attn_scores = _attn_scores(zcols(OFF_QA, OFF_QA + ATTN_WIDTH), kwin())
    ret = [_ret_scores(zcols, g, tb, row_keep) for g in range(N_RG)]
    t = ret[0][0].shape[0]
    yield
    attn_res = _attn_values(attn_scores, vwin(), valid, fill_ref, tb)
    pairs = [_ret_outputs(g, *ret[g], state_ref, tb) for g in range(N_RG)]
    yield
    emit_attn(_attn_normalise(attn_res, t).astype(BF16))
    outs = [intra + cross * tb["dec"][2 * g + 1] for g, (intra, cross) in enumerate(pairs)]
    ssq = _ret_square_sums(outs, tb)
    yield
    res = []
    for g, o in enumerate(outs):
        gate = zcols(OFF_GR + g * RGW, OFF_GR + (g + 1) * RGW).astype(F32)
        ss = ssq[2 * g * t:(2 * g + 1) * t] + ssq[(2 * g + 1) * t:(2 * g + 2) * t]
        res.append(o * lax.rsqrt(ss * (1.0 / RET_DV) + EPS) * _silu(gate))
    emit_ret(jnp.concatenate(res, axis=-1).astype(BF16))
    yield


N_STAGES = 4
FIRST_EMIT_STAGE = 2

def _stage_order(n_blocks):
    order = []
    for step in range(n_blocks + N_STAGES - 1):
        order += [step - stage for stage in reversed(range(N_STAGES)) if 0 <= step - stage < n_blocks]
    return order


def _run_blocks(blocks):
    for b in _stage_order(len(blocks)):
        next(blocks[b])


def _state_to_heads(state_ref, out_ref):
    for g in range(N_RG):
        for hh in range(RG):
            out_ref[0, g * RG + hh] = state_ref[g, :, hh * RET_DV:(hh + 1) * RET_DV]


def _prompt_kernel(sinks_ref, z_ref, zmeta_ref, x_ref, *rest, names, tile_chunks, n_g, n_tiles, lead, ff_chunk):
    rest = list(rest)
    take = lambda n: [rest.pop(0) for _ in range(n)]
    p = dict(zip(names, take(len(names))))
    tb = dict(zip(_TABLE_ORDER, take(len(_TABLE_ORDER))))
    o_ref, mixmeta_ref, klast_ref, vlast_ref, sfin_ref = take(5)
    mix_ref, h_ref, xn_ref, kctx_ref, vctx_ref, fill_ref, state_ref = take(7)
    assert not rest
    s = pl.program_id(0)
    g = lax.rem(jnp.minimum(s, n_tiles - 1), n_g)
    ctx = WINDOW_CHUNKS * CHUNK
    rows = tile_chunks * CHUNK
    s_keys = ctx + CHUNK
    col = lax.broadcasted_iota(jnp.int32, (1, KEY_PAD), 1)

    def put_keys(dst, z, n):
        kctx_ref[dst:dst + n, :] = z[:, OFF_KA:OFF_KA + KV_WIDTH]
        vctx_ref[dst:dst + n, :] = z[:, OFF_VA:OFF_VA + KV_WIDTH]

    def block(z_src, out_ref, r0, first_row, row_keep=None):
        rows_c = slice(r0, r0 + CHUNK)
        win = slice(r0, r0 + KEY_PAD)
        valid = (col < s_keys) & (col + (first_row - ctx) >= lead)

        def emit(lo):
            def store(v):
                out_ref[rows_c, lo:lo + v.shape[1]] = v
            return store

        return _mixer_stages(lambda lo, hi: z_src[rows_c, lo:hi], lambda: kctx_ref[win, :], lambda: vctx_ref[win, :],
                             valid, fill_ref, state_ref, tb, row_keep, emit(0), emit(ATTN_WIDTH))

    @pl.when(s == 0)
    def _():
        mix_ref[...] = jnp.zeros(mix_ref.shape, BF16)

    @pl.when(g == 0)
    def _():
        for r in (kctx_ref, vctx_ref):
            r[...] = jnp.zeros(r.shape, BF16)
        state_ref[...] = jnp.zeros(state_ref.shape, F32)
        _fill_table(sinks_ref, fill_ref, CHUNK, s_keys)
        zm = zmeta_ref[...]
        put_keys(ctx, zm, CHUNK)
        row_keep = (lax.broadcasted_iota(jnp.int32, (CHUNK, 1), 0) >= lead).astype(F32)
        _run_blocks([block(zmeta_ref, mixmeta_ref, 0, 0, row_keep)])
        put_keys(ctx - CHUNK, zm, CHUNK)

    units = _token_units(lambda r: x_ref[r, :], mix_ref, p, o_ref, None, h_ref, xn_ref, ff_chunk)
    n_units = _n_token_units(rows, h_ref.shape[1], ff_chunk, True)
    n_mix_reads = len(_row_halves(rows))

    put_keys(ctx, z_ref[...], rows)
    tile_row = (1 + g * tile_chunks) * CHUNK
    blocks = [block(z_ref, mix_ref, c * CHUNK, tile_row + c * CHUNK) for c in range(tile_chunks)]
    order = _stage_order(len(blocks))
    first_emit = [i for i, b in enumerate(order) if b == 0][FIRST_EMIT_STAGE]
    issued = 0
    for i, b in enumerate(order):
        if i == first_emit:
            while issued < n_mix_reads:
                next(units)
                issued += 1
        next(blocks[b])
        target = max(issued, -(-n_units * (i + 1) // len(order)))
        for _ in range(target - issued):
            next(units)
        issued = target
    assert next(units, None) is None

    for r in (kctx_ref, vctx_ref):
        r[0:ctx, :] = r[rows:rows + ctx, :]

    @pl.when((g == n_g - 1) & (s < n_tiles))
    def _():
        klast_ref[0] = kctx_ref[0:ctx, :].astype(F32)
        vlast_ref[0] = vctx_ref[0:ctx, :].astype(F32)
        _state_to_heads(state_ref, sfin_ref)


def _prompt_step(z, x, sinks, tables, ffn, mix, batch, seq, lead, final=None):
    d = x.shape[1]
    d_ff = ffn[2].shape[0]
    n_main = batch * seq
    n_chunks = seq // CHUNK
    tile_chunks = _largest_divisor(n_chunks, MAX_TILE_CHUNKS)
    n_g = n_chunks // tile_chunks
    n_tiles = batch * n_g
    rows = tile_chunks * CHUNK
    ctx = WINDOW_CHUNKS * CHUNK
    assert rows >= ctx and n_main % CHUNK == 0
    meta_block0 = n_main // CHUNK
    tabs = [tables[k] for k in _TABLE_ORDER]
    names, w_arrays, w_specs = _weight_args(d, ffn, mix, None, final)
    state_shape = (RET_HEADS, RET_DK, RET_DV)
    ctx_rows = ctx + rows + KEY_PAD - (ctx + CHUNK)
    ctx_buf = pltpu.VMEM((ctx_rows, KV_WIDTH), BF16)
    cur = lambda s: jnp.minimum(s, n_tiles - 1)
    prev = lambda s: jnp.maximum(s - 1, 0)
    seq_of = lambda s: cur(s) // n_g
    return pl.pallas_call(
        functools.partial(_prompt_kernel, names=names, tile_chunks=tile_chunks, n_g=n_g, n_tiles=n_tiles, lead=lead,
                          ff_chunk=_ffn_chunk(d_ff)),
        grid=(n_tiles + 1,),
        in_specs=[pl.BlockSpec(memory_space=pltpu.SMEM),
                  pl.BlockSpec((rows, IN_WIDTH), lambda s: (cur(s), 0)),
                  pl.BlockSpec((CHUNK, IN_WIDTH), lambda s: (meta_block0 + seq_of(s), 0)),
                  pl.BlockSpec((rows, d), lambda s: (prev(s), 0))]
                 + w_specs + [_resident(t.shape) for t in tabs],
        out_specs=[pl.BlockSpec((rows, d), lambda s: (prev(s), 0)),
                   pl.BlockSpec((CHUNK, MIX_WIDTH), lambda s: (seq_of(s), 0)),
                   pl.BlockSpec((1, ctx, KV_WIDTH), lambda s: (seq_of(s), 0, 0)),
                   pl.BlockSpec((1, ctx, KV_WIDTH), lambda s: (seq_of(s), 0, 0)),
                   pl.BlockSpec((1,) + state_shape, lambda s: (seq_of(s), 0, 0, 0))],
        out_shape=[jax.ShapeDtypeStruct((n_main, d), F32),
                   jax.ShapeDtypeStruct((batch * CHUNK, MIX_WIDTH), BF16),
                   jax.ShapeDtypeStruct((batch, ctx, KV_WIDTH), F32),
                   jax.ShapeDtypeStruct((batch, ctx, KV_WIDTH), F32),
                   jax.ShapeDtypeStruct((batch,) + state_shape, F32)],
        scratch_shapes=[pltpu.VMEM((rows, MIX_WIDTH), BF16), pltpu.VMEM((rows, d_ff), BF16),
                        pltpu.VMEM((rows, d), BF16), ctx_buf, ctx_buf,
                        pltpu.VMEM((ATTN_KV_HEADS, GROUP * CHUNK, KEY_PAD), F32),
                        pltpu.VMEM((N_RG, RET_DK, RGW), F32)],
        compiler_params=_compiler_params(1),
        name="prompt_step",
    )(sinks, z, z, x, *w_arrays, *tabs)


def _sample_mixer_kernel(sinks_ref, z_ref, ck_ref, cv_ref, s0_ref, *rest, cache_rows):
    tb = dict(zip(_TABLE_ORDER, rest[:len(_TABLE_ORDER)]))
    mix_ref, knew_ref, vnew_ref, snew_ref, fill_ref, state_ref = rest[len(_TABLE_ORDER):]
    z = z_ref[...]
    t = z.shape[0]
    s_keys = cache_rows + t
    pad = jnp.zeros((KEY_PAD - s_keys, KV_WIDTH), F32)
    k_all = jnp.concatenate([ck_ref[0], z[:, OFF_KA:OFF_KA + KV_WIDTH].astype(F32)], axis=0)
    v_all = jnp.concatenate([cv_ref[0], z[:, OFF_VA:OFF_VA + KV_WIDTH].astype(F32)], axis=0)
    windows = [jnp.concatenate([a, pad], axis=0).astype(BF16) for a in (k_all, v_all)]
    _fill_table(sinks_ref, fill_ref, t, s_keys)
    for g in range(N_RG):
        for hh in range(RG):
            state_ref[g, :, hh * RET_DV:(hh + 1) * RET_DV] = s0_ref[0, g * RG + hh]
    col = lax.broadcasted_iota(jnp.int32, (1, KEY_PAD), 1)

    def emit(lo):
        def store(v):
            mix_ref[:, lo:lo + v.shape[1]] = v
        return store

    _run_blocks([_mixer_stages(lambda lo, hi: z_ref[:, lo:hi], lambda: windows[0], lambda: windows[1], col < s_keys,
                               fill_ref, state_ref, tb, None, emit(0), emit(ATTN_WIDTH))])
    knew_ref[0] = k_all[s_keys - cache_rows:]
    vnew_ref[0] = v_all[s_keys - cache_rows:]
    _state_to_heads(state_ref, snew_ref)


def _sample_mixer(z, sinks, tables, cache_k, cache_v, state, batch, t, row_offset):
    cache_rows = cache_k.shape[1]
    assert row_offset % t == 0 and cache_rows + t < KEY_PAD
    first_block = row_offset // t
    tabs = [tables[k] for k in _TABLE_ORDER]
    state_shape = (RET_HEADS, RET_DK, RET_DV)
    cache_spec = pl.BlockSpec((1, cache_rows, KV_WIDTH), lambda b: (b, 0, 0))
    state_spec = pl.BlockSpec((1,) + state_shape, lambda b: (b, 0, 0, 0))
    return pl.pallas_call(
        functools.partial(_sample_mixer_kernel, cache_rows=cache_rows),
        grid=(batch,),
        in_specs=[pl.BlockSpec(memory_space=pltpu.SMEM),
                  pl.BlockSpec((t, IN_WIDTH), lambda b: (first_block + b, 0)),
                  cache_spec, cache_spec, state_spec] + [_resident(x.shape) for x in tabs],
        out_specs=[pl.BlockSpec((t, MIX_WIDTH), lambda b: (b, 0)), cache_spec, cache_spec, state_spec],
        out_shape=[jax.ShapeDtypeStruct((batch * t, MIX_WIDTH), BF16),
                   jax.ShapeDtypeStruct(cache_k.shape, F32),
                   jax.ShapeDtypeStruct(cache_v.shape, F32),
                   jax.ShapeDtypeStruct(state.shape, F32)],
        scratch_shapes=[pltpu.VMEM((ATTN_KV_HEADS, GROUP * t, KEY_PAD), F32),
                        pltpu.VMEM((N_RG, RET_DK, RGW), F32)],
        compiler_params=_compiler_params(1),
        name="sample_mixer",
    )(sinks, z, cache_k, cache_v, state, *tabs)


def kernel(x_prompt, x_sample, cache_swa_k, cache_swa_v, state_ret, meta_tokens, w_in, w_out, attn_sinks,
           ffn1_w_in, ffn1_w_out, ffn2_w_in, ffn2_w_out, norm_ffn1_pre, norm_ffn1_post, norm_mix_pre,
           norm_mix_post, norm_ffn2_pre, norm_ffn2_post, final_norm):
    batch, seq, d = x_prompt.shape
    dec_batch, dec_seq, _ = x_sample.shape
    depth = w_in.shape[0]
    cache_rows = cache_swa_k.shape[2]
    lead = CHUNK - N_META
    n_main, n_meta, n_sample = batch * seq, batch * CHUNK, dec_batch * dec_seq
    n_tail = n_meta + n_sample
    tm = _row_tile(math.gcd(n_main, n_tail))
    assert w_in.shape[2] == IN_WIDTH and w_out.shape[1] == MIX_WIDTH
    assert seq % CHUNK == 0 and cache_rows == WINDOW_CHUNKS * CHUNK and seq >= cache_rows

    meta_chunk = jnp.concatenate([jnp.zeros((lead, d), x_prompt.dtype), meta_tokens.astype(x_prompt.dtype)], axis=0)
    x_tail = jnp.concatenate([jnp.broadcast_to(meta_chunk, (batch, CHUNK, d)).reshape(n_meta, d),
                              x_sample.reshape(n_sample, d)], axis=0)
    x = (x_prompt.reshape(n_main, d), x_tail)

    tables_p = _mixer_tables(CHUNK, WINDOW_CHUNKS * CHUNK + CHUNK)
    tables_s = _mixer_tables(dec_seq, cache_rows + dec_seq)
    cache_k = cache_swa_k.reshape(depth, dec_batch, cache_rows, KV_WIDTH)
    cache_v = cache_swa_v.reshape(depth, dec_batch, cache_rows, KV_WIDTH)

    w_proj, w_mix = _projection_weight(w_in), w_out.astype(BF16)
    w1_in, w1_out = ffn1_w_in.astype(BF16), ffn1_w_out.astype(BF16)
    w2_in, w2_out = ffn2_w_in.astype(BF16), ffn2_w_out.astype(BF16)

    pk, pv, ps, sk, sv, ss = [], [], [], [], [], []
    for l in range(depth):
        ffn1 = (norm_ffn1_pre[l], w1_in[l], w1_out[l], norm_ffn1_post[l])
        ffn2 = (norm_ffn2_pre[l], w2_in[l], w2_out[l], norm_ffn2_post[l])
        mix_w = (w_mix[l], norm_mix_post[l])
        final = final_norm if l == depth - 1 else None
        x1, z = _token_step(x, ffn1, tm, n_out=n_main + n_tail, proj=(norm_mix_pre[l], w_proj[l]))
        x_main, mix_meta, k_p, v_p, s_p = _prompt_step(z, x1, attn_sinks[l], tables_p, ffn2, mix_w, batch, seq, lead,
                                                       final)
        mix_s, k_s, v_s, s_s = _sample_mixer(z, attn_sinks[l], tables_s, cache_k[l], cache_v[l], state_ret[l],
                                             dec_batch, dec_seq, n_main + n_meta)
        x_tail = _token_step(x1, ffn2, tm, n_out=n_tail, first_tile=n_main // tm,
                             mix=(jnp.concatenate([mix_meta, mix_s], axis=0),) + mix_w, final=final)
        x = (x_main, x_tail)
        pk.append(k_p); pv.append(v_p); ps.append(s_p)
        sk.append(k_s); sv.append(v_s); ss.append(s_s)

    y_main, y_tail = x
    y_prompt = y_main.reshape(batch, seq, d)
    y_sample = y_tail[n_meta:].reshape(dec_batch, dec_seq, d)
    kv_shape_p = (depth, batch, cache_rows, ATTN_KV_HEADS, HEAD_DIM)
    kv_shape_s = (depth, dec_batch, cache_rows, ATTN_KV_HEADS, HEAD_DIM)
    return (y_prompt, y_sample,
            jnp.stack(pk).reshape(kv_shape_p), jnp.stack(pv).reshape(kv_shape_p), jnp.stack(ps),
            jnp.stack(sk).reshape(kv_shape_s), jnp.stack(sv).reshape(kv_shape_s), jnp.stack(ss))
```

```python
import functools
import math

import jax
import jax.numpy as jnp
import numpy as np
from jax import lax
from jax.experimental import pallas as pl
from jax.experimental.pallas import tpu as pltpu

CHUNK = 64
N_META = 16
WINDOW_CHUNKS = 2
ATTN_HEADS = 8
ATTN_KV_HEADS = 2
HEAD_DIM = 64
GROUP = ATTN_HEADS // ATTN_KV_HEADS
ATTN_SCALE = HEAD_DIM ** -0.5
LOG2_E = math.log2(math.e)
RET_HEADS = 8
RET_DK = 64
RET_DV = 64
ATTN_WIDTH = ATTN_HEADS * HEAD_DIM
KV_WIDTH = ATTN_KV_HEADS * HEAD_DIM
RET_WIDTH = RET_HEADS * RET_DK
ROPE_BASE = 10000.0
EPS = 1e-6
NEG = -1e30

OFF_QA = 0
OFF_KA = OFF_QA + ATTN_WIDTH
OFF_VA = OFF_KA + KV_WIDTH
OFF_QR = OFF_VA + KV_WIDTH
OFF_KR = OFF_QR + RET_WIDTH
OFF_VR = OFF_KR + RET_WIDTH
OFF_GR = OFF_VR + RET_WIDTH
IN_WIDTH = OFF_GR + RET_WIDTH
MIX_WIDTH = ATTN_WIDTH + RET_WIDTH

LANES = 128
HALF = RET_DK // 2
RG = 4
RGW = RG * RET_DK
N_RG = RET_HEADS // RG
KEY_PAD = 256
V7X_VMEM_LIMIT_BYTES = 56 * 1024 * 1024
FF_CHUNK = 256
MAX_ROW_TILE = 512
ROW_SPLIT = 2
MAX_TILE_CHUNKS = 8

BF16 = jnp.bfloat16
F32 = jnp.float32

assert RGW == 2 * LANES and KV_WIDTH == LANES and GROUP == 4 and ATTN_KV_HEADS == 2


def _row_tile(n, cap=MAX_ROW_TILE):
    t = cap
    while t > 8 and n % t:
        t //= 2
    assert n % t == 0, (n, t)
    return t


def _largest_divisor(n, cap):
    return max(d for d in range(1, cap + 1) if n % d == 0)


def _rms(x):
    return x * lax.rsqrt(jnp.mean(x * x, axis=-1, keepdims=True) + EPS)


def _silu(x):
    return x / (1.0 + jnp.exp(-x))


def _compiler_params(n_axes):
    return pltpu.CompilerParams(dimension_semantics=("arbitrary",) * n_axes,
                                vmem_limit_bytes=V7X_VMEM_LIMIT_BYTES)


def _resident(shape):
    return pl.BlockSpec(shape, lambda *_: (0,) * len(shape), pipeline_mode=pl.Buffered(1))


def _row_halves(tm):
    n_split = ROW_SPLIT if tm % (ROW_SPLIT * 16) == 0 else 1
    return [slice(r * tm // n_split, (r + 1) * tm // n_split) for r in range(n_split)]


def _token_units(x_of, mix_ref, p, o_ref, z_ref, h_ref, xn_ref, ff_chunk):
    tm, d_ff = h_ref.shape
    halves = _row_halves(tm)
    if mix_ref is not None:
        for rows in halves:
            y = jnp.dot(mix_ref[rows, :], p["wmix"][...], preferred_element_type=F32)
            o_ref[rows, :] = x_of(rows) + _rms(y) * p["gmix"][...]
            yield
        resid = lambda rows: o_ref[rows, :]
    else:
        resid = x_of
    xn = jnp.concatenate([(_rms(resid(rows)) * p["gpre"][...]).astype(BF16) for rows in halves], axis=0)
    if xn_ref is not None:
        xn_ref[...] = xn
    for j in range(d_ff // ff_chunk):
        lo = j * ff_chunk
        if xn_ref is not None:
            xn = xn_ref[...]
        gate = jnp.dot(xn, p["win"][:, lo:lo + ff_chunk], preferred_element_type=F32)
        yield
        up = jnp.dot(xn, p["win"][:, d_ff + lo:d_ff + lo + ff_chunk], preferred_element_type=F32)
        h_ref[:, lo:lo + ff_chunk] = (_silu(gate) * up).astype(BF16)
        yield
    ys = []
    for rows in halves:
        ys.append(jnp.dot(h_ref[rows, :], p["wout"][...], preferred_element_type=F32))
        yield
    for rows, y in zip(halves, ys):
        xr = resid(rows) + 0.5 * (_rms(y) * p["gpost"][...])
        if z_ref is not None:
            xn = (_rms(xr) * p["gproj"][...]).astype(BF16)
            z_ref[rows, :] = jnp.dot(xn, p["wproj"][...], preferred_element_type=F32).astype(BF16)
        o_ref[rows, :] = _rms(xr) * p["gfin"][...] if "gfin" in p else xr
        yield


def _n_token_units(tm, d_ff, ff_chunk, has_mix):
    return (3 if has_mix else 2) * len(_row_halves(tm)) + 2 * (d_ff // ff_chunk)


def _token_kernel(*refs, names, ff_chunk, n_main_tiles, split_in, has_mix, has_proj):
    refs = list(refs)
    take = lambda n: [refs.pop(0) for _ in range(n)]
    if split_in:
        xm_ref, xt_ref = take(2)
        in_main = pl.program_id(0) < n_main_tiles
        x_of = lambda rows: jnp.where(in_main, xm_ref[rows, :], xt_ref[rows, :])
    else:
        (x_ref,) = take(1)
        x_of = lambda rows: x_ref[rows, :]
    mix_ref = take(1)[0] if has_mix else None
    p = dict(zip(names, take(len(names))))
    (o_ref,) = take(1)
    z_ref = take(1)[0] if has_proj else None
    (h_ref,) = take(1)
    assert not refs
    for _ in _token_units(x_of, mix_ref, p, o_ref, z_ref, h_ref, None, ff_chunk):
        pass


def _ffn_chunk(d_ff):
    return FF_CHUNK if d_ff % FF_CHUNK == 0 else d_ff


def _weight_args(d, ffn, mix=None, proj=None, final=None):
    g_pre, w_in, w_out, g_post = ffn
    named = []
    if mix is not None:
        named += [("wmix", mix[0]), ("gmix", mix[1].reshape(1, d))]
    named += [("gpre", g_pre.reshape(1, d)), ("win", w_in), ("wout", w_out), ("gpost", g_post.reshape(1, d))]
    if proj is not None:
        named += [("gproj", proj[0].reshape(1, d)), ("wproj", proj[1])]
    if final is not None:
        named.append(("gfin", final.reshape(1, d)))
    names, arrays = zip(*named)
    return names, list(arrays), [_resident(a.shape) for a in arrays]


def _token_step(x, ffn, tm, *, n_out, first_tile=0, mix=None, proj=None, final=None):
    split_in = isinstance(x, tuple)
    d = x[0].shape[1] if split_in else x.shape[1]
    d_ff = ffn[2].shape[0]
    n_main_tiles = x[0].shape[0] // tm if split_in else 0
    args, in_specs = [], []
    if split_in:
        assert x[0].shape[0] % tm == 0 and x[1].shape[0] % tm == 0
        args += list(x)
        in_specs += [pl.BlockSpec((tm, d), lambda i: (jnp.minimum(i, n_main_tiles - 1), 0)),
                     pl.BlockSpec((tm, d), lambda i: (jnp.maximum(i - n_main_tiles, 0), 0))]
    else:
        args.append(x)
        in_specs.append(pl.BlockSpec((tm, d), lambda i: (first_tile + i, 0)))
    if mix is not None:
        args.append(mix[0])
        in_specs.append(pl.BlockSpec((tm, MIX_WIDTH), lambda i: (i, 0)))
    names, w_arrays, w_specs = _weight_args(d, ffn, mix[1:] if mix is not None else None, proj, final)
    out_shape = [jax.ShapeDtypeStruct((n_out, d), F32)]
    out_specs = [pl.BlockSpec((tm, d), lambda i: (i, 0))]
    if proj is not None:
        out_shape.append(jax.ShapeDtypeStruct((n_out, IN_WIDTH), BF16))
        out_specs.append(pl.BlockSpec((tm, IN_WIDTH), lambda i: (i, 0)))
    outs = pl.pallas_call(
        functools.partial(_token_kernel, names=names, ff_chunk=_ffn_chunk(d_ff), n_main_tiles=n_main_tiles,
                          split_in=split_in, has_mix=mix is not None, has_proj=proj is not None),
        grid=(n_out // tm,),
        in_specs=in_specs + w_specs,
        out_specs=out_specs,
        out_shape=out_shape,
        scratch_shapes=[pltpu.VMEM((tm, d_ff), BF16)],
        compiler_params=_compiler_params(1),
        name="token_step",
    )(*args, *w_arrays)
    return outs[0] if len(outs) == 1 else tuple(outs)


def _permute_ret_columns(w):
    lead = w.shape[:-1]
    return jnp.swapaxes(w.reshape(lead + (N_RG, RG, 2, HALF)), -3, -2).reshape(lead + (RET_WIDTH,))


def _projection_weight(w):
    parts = [w[..., :OFF_QR], _permute_ret_columns(w[..., OFF_QR:OFF_KR]),
             _permute_ret_columns(w[..., OFF_KR:OFF_VR]), w[..., OFF_VR:]]
    return jnp.concatenate(parts, axis=-1).astype(BF16)


def _mixer_tables(t, s_keys):
    log_g = jnp.log1p(-jnp.exp2(-5.0 - jnp.arange(RET_HEADS, dtype=F32)))
    freqs = ROPE_BASE ** (-jnp.arange(HALF, dtype=F32) / HALF)
    pos = jnp.arange(t, dtype=F32)
    ang = pos[:, None] * freqs[None, :]
    cos, sin = jnp.tile(jnp.cos(ang), (1, RG)), jnp.tile(jnp.sin(ang), (1, RG))
    qs = RET_DK ** -0.5
    rot = jnp.stack([cos * qs, sin * qs, cos, sin])

    head = lambda g: slice(g * RG, (g + 1) * RG)
    diff = pos[:, None] - pos[None, :]
    intra = jnp.where(diff >= 0, jnp.exp(log_g[:, None, None] * jnp.maximum(diff, 0.0)), 0.0)
    kv_dec = jnp.exp((t - pos)[:, None] * log_g[None, :])
    q_dec = jnp.exp(pos[:, None] * log_g[None, :])
    dec = []
    for g in range(N_RG):
        kd = jnp.tile(jnp.repeat(kv_dec[:, head(g)], HALF, axis=1), (1, 2))
        qd = jnp.repeat(q_dec[:, head(g)], RET_DV, axis=1)
        dec += [kd, qd]
    dec = jnp.stack(dec)
    dtab = jnp.stack([jnp.concatenate(list(intra[head(g)]), axis=1) for g in range(N_RG)])

    g_t = jnp.exp(t * log_g)
    shift = -t * freqs
    stab = [jnp.broadcast_to(jnp.repeat(g_t[head(g)], RET_DV)[None, :], (HALF, RGW)) for g in range(N_RG)]
    stab += [jnp.broadcast_to(jnp.cos(shift)[:, None], (HALF, RGW)),
             jnp.broadcast_to(jnp.sin(shift)[:, None], (HALF, RGW))]
    stab = jnp.stack(stab)

    r4 = np.arange(RG * t)[:, None] // t
    lane = np.arange(RGW)[None, :]
    mask_k = jnp.asarray(r4 == (lane % LANES) // HALF, BF16)
    mask_v = jnp.asarray(r4 == lane // RET_DV, BF16)
    rs = np.arange(LANES)[:, None] // HALF
    mask_s = jnp.asarray(rs == lane // RET_DV, F32)
    seg = jnp.asarray(np.arange(RGW)[:, None] // RET_DV == lane // RET_DV, BF16)

    rowk = np.arange(KEY_PAD)[:, None] < s_keys
    lane_v = np.arange(LANES)[None, :] // HEAD_DIM
    v_keep = jnp.stack([jnp.asarray(rowk & (lane_v == p), BF16) for p in range(2)])
    v_ones = jnp.stack([jnp.asarray(np.broadcast_to(lane_v != p, (KEY_PAD, LANES)), BF16) for p in range(2)])
    return dict(rot=rot, dec=dec, dtab=dtab, stab=stab, mask_k=mask_k, mask_v=mask_v, mask_s=mask_s, seg=seg,
                v_keep=v_keep, v_ones=v_ones)


_TABLE_ORDER = ("rot", "dec", "dtab", "stab", "mask_k", "mask_v", "mask_s", "seg", "v_keep", "v_ones")


def _fill_table(sinks_ref, fill_ref, t, s_keys):
    col = lax.broadcasted_iota(jnp.int32, (t, KEY_PAD), 1)
    for kv in range(ATTN_KV_HEADS):
        for j in range(GROUP):
            fill_ref[kv, j * t:(j + 1) * t, :] = jnp.where(col == s_keys, sinks_ref[kv * GROUP + j] * LOG2_E, NEG)


def _attn_scores(q, kwin):
    t = q.shape[0]
    lower = lax.broadcasted_iota(jnp.int32, (t, LANES), 1) < HEAD_DIM
    out = []
    for kv in range(ATTN_KV_HEADS):
        keep = jnp.where(lower if kv == 0 else ~lower, ATTN_SCALE * LOG2_E, 0.0)
        parts = []
        for j in range(2):
            blk = q[:, (2 * kv + j) * LANES:(2 * kv + j + 1) * LANES].astype(F32)
            rolled = pltpu.roll(blk, HEAD_DIM, axis=1)
            lo_head, hi_head = (blk, rolled) if kv == 0 else (rolled, blk)
            parts += [lo_head * keep, hi_head * keep]
        qs = jnp.concatenate(parts, axis=0).astype(BF16)
        out.append(lax.dot_general(qs, kwin, (((1,), (1,)), ((), ())), preferred_element_type=F32))
    return out


def _attn_values(scores, vwin, valid, fill_ref, tb):
    out = []
    for kv, s in enumerate(scores):
        s = jnp.where(valid, s, fill_ref[kv])
        p = jnp.exp2(s - jnp.max(s, axis=-1, keepdims=True))
        vext = vwin * tb["v_keep"][kv] + tb["v_ones"][kv]
        out.append(jnp.dot(p.astype(BF16), vext, preferred_element_type=F32))
    return out


def _attn_normalise(results, t):
    lower = lax.broadcasted_iota(jnp.int32, (t, LANES), 1) < HEAD_DIM
    cols = []
    for kv, res in enumerate(results):
        swapped = pltpu.roll(res, HEAD_DIM, axis=1)
        for j in range(2):
            lo_rows, hi_rows = slice(2 * j * t, (2 * j + 1) * t), slice((2 * j + 1) * t, (2 * j + 2) * t)
            if kv == 0:
                lo, hi = res[lo_rows] / swapped[lo_rows], swapped[hi_rows] / res[hi_rows]
            else:
                lo, hi = swapped[lo_rows] / res[lo_rows], res[hi_rows] / swapped[hi_rows]
            cols.append(jnp.where(lower, lo, hi))
    return jnp.concatenate(cols, axis=-1)


def _ret_scores(zcols, g, tb, row_keep):
    lo = g * RGW
    cq, sq, ck, sk = tb["rot"][0], tb["rot"][1], tb["rot"][2], tb["rot"][3]
    q1 = zcols(OFF_QR + lo, OFF_QR + lo + LANES).astype(F32)
    q2 = zcols(OFF_QR + lo + LANES, OFF_QR + lo + RGW).astype(F32)
    k1 = zcols(OFF_KR + lo, OFF_KR + lo + LANES).astype(F32)
    k2 = zcols(OFF_KR + lo + LANES, OFF_KR + lo + RGW).astype(F32)
    if row_keep is not None:
        k1, k2 = k1 * row_keep, k2 * row_keep
    q_rot = jnp.concatenate([q1 * cq - q2 * sq, q2 * cq + q1 * sq], axis=-1)
    k_rot = jnp.concatenate([k1 * ck - k2 * sk, k2 * ck + k1 * sk], axis=-1)
    qb, kb = q_rot.astype(BF16), k_rot.astype(BF16)
    kdb = (k_rot * tb["dec"][2 * g]).astype(BF16)
    vb = zcols(OFF_VR + lo, OFF_VR + lo + RGW)
    k_bd = jnp.concatenate([kb] * RG, axis=0) * tb["mask_k"][...]
    scores = lax.dot_general(qb, k_bd, (((1,), (1,)), ((), ())), preferred_element_type=F32)
    w_full = lax.dot_general(kdb, vb, (((0,), (0,)), ((), ())), preferred_element_type=F32)
    return qb, vb, scores, w_full


def _ret_outputs(g, qb, vb, scores, w_full, state_ref, tb):
    mask_s = tb["mask_s"][...]
    v_bd = jnp.concatenate([vb] * RG, axis=0) * tb["mask_v"][...]
    intra = jnp.dot((scores * tb["dtab"][g]).astype(BF16), v_bd, preferred_element_type=F32)
    c1, c2 = state_ref[g, 0:HALF, :], state_ref[g, HALF:2 * HALF, :]
    mask_sb = mask_s.astype(BF16)
    s_bd = jnp.concatenate([jnp.concatenate([c.astype(BF16)] * RG, axis=0) * mask_sb for c in (c1, c2)], axis=0)
    cross = jnp.dot(qb, s_bd, preferred_element_type=F32)
    w = []
    for half in range(2):
        wm = w_full[half * LANES:(half + 1) * LANES] * mask_s
        w.append(wm[0:HALF] + wm[HALF:2 * HALF] + wm[2 * HALF:3 * HALF] + wm[3 * HALF:4 * HALF])
    cos_s, sin_s = tb["stab"][N_RG], tb["stab"][N_RG + 1]
    a1 = tb["stab"][g] * c1 + w[0]
    a2 = tb["stab"][g] * c2 + w[1]
    state_ref[g, 0:HALF, :] = a1 * cos_s - a2 * sin_s
    state_ref[g, HALF:2 * HALF, :] = a2 * cos_s + a1 * sin_s
    return intra, cross


def _ret_square_sums(outs, tb):
    parts = []
    for o in outs:
        sq = o * o
        hi = sq.astype(BF16)
        parts += [hi, (sq - hi.astype(F32)).astype(BF16)]
    return jnp.dot(jnp.concatenate(parts, axis=0), tb["seg"][...], preferred_element_type=F32)


def _mixer_stages(zcols, kwin, vwin, valid, fill_ref, state_ref, tb, row_keep, emit_attn, emit_ret):
    attn_scores = _attn_scores(zcols(OFF_QA, OFF_QA + ATTN_WIDTH), kwin())
    ret = [_ret_scores(zcols, g, tb, row_keep) for g in range(N_RG)]
    t = ret[0][0].shape[0]
    yield
    attn_res = _attn_values(attn_scores, vwin(), valid, fill_ref, tb)
    pairs = [_ret_outputs(g, *ret[g], state_ref, tb) for g in range(N_RG)]
    yield
    emit_attn(_attn_normalise(attn_res, t).astype(BF16))
    outs = [intra + cross * tb["dec"][2 * g + 1] for g, (intra, cross) in enumerate(pairs)]
    ssq = _ret_square_sums(outs, tb)
    yield
    res = []
    for g, o in enumerate(outs):
        gate = zcols(OFF_GR + g * RGW, OFF_GR + (g + 1) * RGW).astype(F32)
        ss = ssq[2 * g * t:(2 * g + 1) * t] + ssq[(2 * g + 1) * t:(2 * g + 2) * t]
        res.append(o * lax.rsqrt(ss * (1.0 / RET_DV) + EPS) * _silu(gate))
    emit_ret(jnp.concatenate(res, axis=-1).astype(BF16))
    yield


N_STAGES = 4
FIRST_EMIT_STAGE = 2

def _stage_order(n_blocks):
    order = []
    for step in range(n_blocks + N_STAGES - 1):
        order += [step - stage for stage in reversed(range(N_STAGES)) if 0 <= step - stage < n_blocks]
    return order


def _run_blocks(blocks):
    for b in _stage_order(len(blocks)):
        next(blocks[b])


def _state_to_heads(state_ref, out_ref):
    for g in range(N_RG):
        for hh in range(RG):
            out_ref[0, g * RG + hh] = state_ref[g, :, hh * RET_DV:(hh + 1) * RET_DV]


def _prompt_kernel(sinks_ref, z_ref, zmeta_ref, x_ref, *rest, names, tile_chunks, n_g, n_tiles, lead, ff_chunk):
    rest = list(rest)
    take = lambda n: [rest.pop(0) for _ in range(n)]
    p = dict(zip(names, take(len(names))))
    tb = dict(zip(_TABLE_ORDER, take(len(_TABLE_ORDER))))
    o_ref, mixmeta_ref, klast_ref, vlast_ref, sfin_ref = take(5)
    mix_ref, h_ref, xn_ref, kctx_ref, vctx_ref, fill_ref, state_ref = take(7)
    assert not rest
    s = pl.program_id(0)
    g = lax.rem(jnp.minimum(s, n_tiles - 1), n_g)
    ctx = WINDOW_CHUNKS * CHUNK
    rows = tile_chunks * CHUNK
    s_keys = ctx + CHUNK
    col = lax.broadcasted_iota(jnp.int32, (1, KEY_PAD), 1)

    def put_keys(dst, z, n):
        kctx_ref[dst:dst + n, :] = z[:, OFF_KA:OFF_KA + KV_WIDTH]
        vctx_ref[dst:dst + n, :] = z[:, OFF_VA:OFF_VA + KV_WIDTH]

    def block(z_src, out_ref, r0, first_row, row_keep=None):
        rows_c = slice(r0, r0 + CHUNK)
        win = slice(r0, r0 + KEY_PAD)
        valid = (col < s_keys) & (col + (first_row - ctx) >= lead)

        def emit(lo):
            def store(v):
                out_ref[rows_c, lo:lo + v.shape[1]] = v
            return store

        return _mixer_stages(lambda lo, hi: z_src[rows_c, lo:hi], lambda: kctx_ref[win, :], lambda: vctx_ref[win, :],
                             valid, fill_ref, state_ref, tb, row_keep, emit(0), emit(ATTN_WIDTH))

    @pl.when(s == 0)
    def _():
        mix_ref[...] = jnp.zeros(mix_ref.shape, BF16)

    @pl.when(g == 0)
    def _():
        for r in (kctx_ref, vctx_ref):
            r[...] = jnp.zeros(r.shape, BF16)
        state_ref[...] = jnp.zeros(state_ref.shape, F32)
        _fill_table(sinks_ref, fill_ref, CHUNK, s_keys)
        zm = zmeta_ref[...]
        put_keys(ctx, zm, CHUNK)
        row_keep = (lax.broadcasted_iota(jnp.int32, (CHUNK, 1), 0) >= lead).astype(F32)
        _run_blocks([block(zmeta_ref, mixmeta_ref, 0, 0, row_keep)])
        put_keys(ctx - CHUNK, zm, CHUNK)

    units = _token_units(lambda r: x_ref[r, :], mix_ref, p, o_ref, None, h_ref, xn_ref, ff_chunk)
    n_units = _n_token_units(rows, h_ref.shape[1], ff_chunk, True)
    n_mix_reads = len(_row_halves(rows))

    put_keys(ctx, z_ref[...], rows)
    tile_row = (1 + g * tile_chunks) * CHUNK
    blocks = [block(z_ref, mix_ref, c * CHUNK, tile_row + c * CHUNK) for c in range(tile_chunks)]
    order = _stage_order(len(blocks))
    first_emit = [i for i, b in enumerate(order) if b == 0][FIRST_EMIT_STAGE]
    issued = 0
    for i, b in enumerate(order):
        if i == first_emit:
            while issued < n_mix_reads:
                next(units)
                issued += 1
        next(blocks[b])
        target = max(issued, -(-n_units * (i + 1) // len(order)))
        for _ in range(target - issued):
            next(units)
        issued = target
    assert next(units, None) is None

    for r in (kctx_ref, vctx_ref):
        r[0:ctx, :] = r[rows:rows + ctx, :]

    @pl.when((g == n_g - 1) & (s < n_tiles))
    def _():
        klast_ref[0] = kctx_ref[0:ctx, :].astype(F32)
        vlast_ref[0] = vctx_ref[0:ctx, :].astype(F32)
        _state_to_heads(state_ref, sfin_ref)


def _prompt_step(z, x, sinks, tables, ffn, mix, batch, seq, lead, final=None):
    d = x.shape[1]
    d_ff = ffn[2].shape[0]
    n_main = batch * seq
    n_chunks = seq // CHUNK
    tile_chunks = _largest_divisor(n_chunks, MAX_TILE_CHUNKS)
    n_g = n_chunks // tile_chunks
    n_tiles = batch * n_g
    rows = tile_chunks * CHUNK
    ctx = WINDOW_CHUNKS * CHUNK
    assert rows >= ctx and n_main % CHUNK == 0
    meta_block0 = n_main // CHUNK
    tabs = [tables[k] for k in _TABLE_ORDER]
    names, w_arrays, w_specs = _weight_args(d, ffn, mix, None, final)
    state_shape = (RET_HEADS, RET_DK, RET_DV)
    ctx_rows = ctx + rows + KEY_PAD - (ctx + CHUNK)
    ctx_buf = pltpu.VMEM((ctx_rows, KV_WIDTH), BF16)
    cur = lambda s: jnp.minimum(s, n_tiles - 1)
    prev = lambda s: jnp.maximum(s - 1, 0)
    seq_of = lambda s: cur(s) // n_g
    return pl.pallas_call(
        functools.partial(_prompt_kernel, names=names, tile_chunks=tile_chunks, n_g=n_g, n_tiles=n_tiles, lead=lead,
                          ff_chunk=_ffn_chunk(d_ff)),
        grid=(n_tiles + 1,),
        in_specs=[pl.BlockSpec(memory_space=pltpu.SMEM),
                  pl.BlockSpec((rows, IN_WIDTH), lambda s: (cur(s), 0)),
                  pl.BlockSpec((CHUNK, IN_WIDTH), lambda s: (meta_block0, 0)),
                  pl.BlockSpec((rows, d), lambda s: (prev(s), 0))]
                 + w_specs + [_resident(t.shape) for t in tabs],
        out_specs=[pl.BlockSpec((rows, d), lambda s: (prev(s), 0)),
                   pl.BlockSpec((CHUNK, MIX_WIDTH), lambda s: (0, 0)),
                   pl.BlockSpec((1, ctx, KV_WIDTH), lambda s: (seq_of(s), 0, 0)),
                   pl.BlockSpec((1, ctx, KV_WIDTH), lambda s: (seq_of(s), 0, 0)),
                   pl.BlockSpec((1,) + state_shape, lambda s: (seq_of(s), 0, 0, 0))],
        out_shape=[jax.ShapeDtypeStruct((n_main, d), F32),
                   jax.ShapeDtypeStruct((CHUNK, MIX_WIDTH), BF16),
                   jax.ShapeDtypeStruct((batch, ctx, KV_WIDTH), F32),
                   jax.ShapeDtypeStruct((batch, ctx, KV_WIDTH), F32),
                   jax.ShapeDtypeStruct((batch,) + state_shape, F32)],
        scratch_shapes=[pltpu.VMEM((rows, MIX_WIDTH), BF16), pltpu.VMEM((rows, d_ff), BF16),
                        pltpu.VMEM((rows, d), BF16), ctx_buf, ctx_buf,
                        pltpu.VMEM((ATTN_KV_HEADS, GROUP * CHUNK, KEY_PAD), F32),
                        pltpu.VMEM((N_RG, RET_DK, RGW), F32)],
        compiler_params=_compiler_params(1),
        name="prompt_step",
    )(sinks, z, z, x, *w_arrays, *tabs)


def _sample_mixer_kernel(sinks_ref, z_ref, ck_ref, cv_ref, s0_ref, *rest, cache_rows):
    tb = dict(zip(_TABLE_ORDER, rest[:len(_TABLE_ORDER)]))
    mix_ref, knew_ref, vnew_ref, snew_ref, fill_ref, state_ref = rest[len(_TABLE_ORDER):]
    z = z_ref[...]
    t = z.shape[0]
    s_keys = cache_rows + t
    pad = jnp.zeros((KEY_PAD - s_keys, KV_WIDTH), F32)
    k_all = jnp.concatenate([ck_ref[0], z[:, OFF_KA:OFF_KA + KV_WIDTH].astype(F32)], axis=0)
    v_all = jnp.concatenate([cv_ref[0], z[:, OFF_VA:OFF_VA + KV_WIDTH].astype(F32)], axis=0)
    windows = [jnp.concatenate([a, pad], axis=0).astype(BF16) for a in (k_all, v_all)]
    _fill_table(sinks_ref, fill_ref, t, s_keys)
    for g in range(N_RG):
        for hh in range(RG):
            state_ref[g, :, hh * RET_DV:(hh + 1) * RET_DV] = s0_ref[0, g * RG + hh]
    col = lax.broadcasted_iota(jnp.int32, (1, KEY_PAD), 1)

    def emit(lo):
        def store(v):
            mix_ref[:, lo:lo + v.shape[1]] = v
        return store

    _run_blocks([_mixer_stages(lambda lo, hi: z_ref[:, lo:hi], lambda: windows[0], lambda: windows[1], col < s_keys,
                               fill_ref, state_ref, tb, None, emit(0), emit(ATTN_WIDTH))])
    knew_ref[0] = k_all[s_keys - cache_rows:]
    vnew_ref[0] = v_all[s_keys - cache_rows:]
    _state_to_heads(state_ref, snew_ref)


def _sample_mixer(z, sinks, tables, cache_k, cache_v, state, batch, t, row_offset):
    cache_rows = cache_k.shape[1]
    assert row_offset % t == 0 and cache_rows + t < KEY_PAD
    first_block = row_offset // t
    tabs = [tables[k] for k in _TABLE_ORDER]
    state_shape = (RET_HEADS, RET_DK, RET_DV)
    cache_spec = pl.BlockSpec((1, cache_rows, KV_WIDTH), lambda b: (b, 0, 0))
    state_spec = pl.BlockSpec((1,) + state_shape, lambda b: (b, 0, 0, 0))
    return pl.pallas_call(
        functools.partial(_sample_mixer_kernel, cache_rows=cache_rows),
        grid=(batch,),
        in_specs=[pl.BlockSpec(memory_space=pltpu.SMEM),
                  pl.BlockSpec((t, IN_WIDTH), lambda b: (first_block + b, 0)),
                  cache_spec, cache_spec, state_spec] + [_resident(x.shape) for x in tabs],
        out_specs=[pl.BlockSpec((t, MIX_WIDTH), lambda b: (b, 0)), cache_spec, cache_spec, state_spec],
        out_shape=[jax.ShapeDtypeStruct((batch * t, MIX_WIDTH), BF16),
                   jax.ShapeDtypeStruct(cache_k.shape, F32),
                   jax.ShapeDtypeStruct(cache_v.shape, F32),
                   jax.ShapeDtypeStruct(state.shape, F32)],
        scratch_shapes=[pltpu.VMEM((ATTN_KV_HEADS, GROUP * t, KEY_PAD), F32),
                        pltpu.VMEM((N_RG, RET_DK, RGW), F32)],
        compiler_params=_compiler_params(1),
        name="sample_mixer",
    )(sinks, z, cache_k, cache_v, state, *tabs)


def kernel(x_prompt, x_sample, cache_swa_k, cache_swa_v, state_ret, meta_tokens, w_in, w_out, attn_sinks,
           ffn1_w_in, ffn1_w_out, ffn2_w_in, ffn2_w_out, norm_ffn1_pre, norm_ffn1_post, norm_mix_pre,
           norm_mix_post, norm_ffn2_pre, norm_ffn2_post, final_norm):
    batch, seq, d = x_prompt.shape
    dec_batch, dec_seq, _ = x_sample.shape
    depth = w_in.shape[0]
    cache_rows = cache_swa_k.shape[2]
    lead = CHUNK - N_META
    n_main, n_meta, n_sample = batch * seq, CHUNK, dec_batch * dec_seq
    tm = _row_tile(n_main)
    n_tail = -(-(n_meta + n_sample) // tm) * tm
    n_pad = n_tail - n_meta - n_sample
    assert w_in.shape[2] == IN_WIDTH and w_out.shape[1] == MIX_WIDTH
    assert seq % CHUNK == 0 and cache_rows == WINDOW_CHUNKS * CHUNK and seq >= cache_rows

    x_tail = jnp.concatenate([jnp.zeros((lead, d), x_prompt.dtype), meta_tokens.astype(x_prompt.dtype),
                              x_sample.reshape(n_sample, d), jnp.zeros((n_pad, d), x_prompt.dtype)], axis=0)
    x = (x_prompt.reshape(n_main, d), x_tail)

    tables_p = _mixer_tables(CHUNK, WINDOW_CHUNKS * CHUNK + CHUNK)
    tables_s = _mixer_tables(dec_seq, cache_rows + dec_seq)
    cache_k = cache_swa_k.reshape(depth, dec_batch, cache_rows, KV_WIDTH)
    cache_v = cache_swa_v.reshape(depth, dec_batch, cache_rows, KV_WIDTH)

    w_proj, w_mix = _projection_weight(w_in), w_out.astype(BF16)
    w1_in, w1_out = ffn1_w_in.astype(BF16), ffn1_w_out.astype(BF16)
    w2_in, w2_out = ffn2_w_in.astype(BF16), ffn2_w_out.astype(BF16)

    pk, pv, ps, sk, sv, ss = [], [], [], [], [], []
    for l in range(depth):
        ffn1 = (norm_ffn1_pre[l], w1_in[l], w1_out[l], norm_ffn1_post[l])
        ffn2 = (norm_ffn2_pre[l], w2_in[l], w2_out[l], norm_ffn2_post[l])
        mix_w = (w_mix[l], norm_mix_post[l])
        final = final_norm if l == depth - 1 else None
        x1, z = _token_step(x, ffn1, tm, n_out=n_main + n_tail, proj=(norm_mix_pre[l], w_proj[l]))
        x_main, mix_meta, k_p, v_p, s_p = _prompt_step(z, x1, attn_sinks[l], tables_p, ffn2, mix_w, batch, seq, lead,
                                                       final)
        mix_s, k_s, v_s, s_s = _sample_mixer(z, attn_sinks[l], tables_s, cache_k[l], cache_v[l], state_ret[l],
                                             dec_batch, dec_seq, n_main + n_meta)
        mix_tail = jnp.concatenate([mix_meta, mix_s, jnp.zeros((n_pad, MIX_WIDTH), BF16)], axis=0)
        x_tail = _token_step(x1, ffn2, tm, n_out=n_tail, first_tile=n_main // tm, mix=(mix_tail,) + mix_w,
                             final=final)
        x = (x_main, x_tail)
        pk.append(k_p); pv.append(v_p); ps.append(s_p)
        sk.append(k_s); sv.append(v_s); ss.append(s_s)

    y_main, y_tail = x
    y_prompt = y_main.reshape(batch, seq, d)
    y_sample = y_tail[n_meta:n_meta + n_sample].reshape(dec_batch, dec_seq, d)
    kv_shape_p = (depth, batch, cache_rows, ATTN_KV_HEADS, HEAD_DIM)
    kv_shape_s = (depth, dec_batch, cache_rows, ATTN_KV_HEADS, HEAD_DIM)
    return (y_prompt, y_sample,
            jnp.stack(pk).reshape(kv_shape_p), jnp.stack(pv).reshape(kv_shape_p), jnp.stack(ps),
            jnp.stack(sk).reshape(kv_shape_s), jnp.stack(sv).reshape(kv_shape_s), jnp.stack(ss))
```

```python
import functools
import math
from typing import NamedTuple

import jax
import jax.numpy as jnp
import numpy as np
from jax import lax
from jax.experimental import pallas as pl
from jax.experimental.pallas import tpu as pltpu

CHUNK = 64
N_META = 16
WINDOW_CHUNKS = 2
ATTN_HEADS = 8
ATTN_KV_HEADS = 2
HEAD_DIM = 64
GROUP = ATTN_HEADS // ATTN_KV_HEADS
ATTN_SCALE = HEAD_DIM ** -0.5
LOG2_E = math.log2(math.e)
RET_HEADS = 8
RET_DK = 64
RET_DV = 64
ATTN_WIDTH = ATTN_HEADS * HEAD_DIM
KV_WIDTH = ATTN_KV_HEADS * HEAD_DIM
RET_WIDTH = RET_HEADS * RET_DK
ROPE_BASE = 10000.0
EPS = 1e-6
NEG = -1e30

OFF_QA = 0
OFF_KA = OFF_QA + ATTN_WIDTH
OFF_VA = OFF_KA + KV_WIDTH
OFF_QR = OFF_VA + KV_WIDTH
OFF_KR = OFF_QR + RET_WIDTH
OFF_VR = OFF_KR + RET_WIDTH
OFF_GR = OFF_VR + RET_WIDTH
IN_WIDTH = OFF_GR + RET_WIDTH
MIX_WIDTH = ATTN_WIDTH + RET_WIDTH

LANES = 128
HALF = RET_DK // 2
RG = 4
RGW = RG * RET_DK
N_RG = RET_HEADS // RG
KEY_PAD = 256
V7X_VMEM_LIMIT_BYTES = 56 * 1024 * 1024
FF_CHUNK = 256
MAX_ROW_TILE = 512
ROW_SPLIT = 2
MAX_TILE_CHUNKS = 8

BF16 = jnp.bfloat16
F32 = jnp.float32

assert RGW == 2 * LANES and KV_WIDTH == LANES and GROUP == 4 and ATTN_KV_HEADS == 2


def _row_tile(n, cap=MAX_ROW_TILE):
    t = cap
    while t > 8 and n % t:
        t //= 2
    assert n % t == 0, (n, t)
    return t


def _largest_divisor(n, cap):
    return max(d for d in range(1, cap + 1) if n % d == 0)


def _rms(x):
    return x * lax.rsqrt(jnp.mean(x * x, axis=-1, keepdims=True) + EPS)


def _silu(x):
    return x / (1.0 + jnp.exp(-x))


def _compiler_params(n_axes):
    return pltpu.CompilerParams(dimension_semantics=("arbitrary",) * n_axes,
                                vmem_limit_bytes=V7X_VMEM_LIMIT_BYTES)


def _resident(shape):
    return pl.BlockSpec(shape, lambda *_: (0,) * len(shape), pipeline_mode=pl.Buffered(1))


class _Layer(NamedTuple):
    stack: jax.Array
    index: int

    @property
    def shape(self):
        return self.stack.shape[1:]

    @property
    def spec(self):
        index = self.index
        return pl.BlockSpec((None,) + self.shape, lambda *_: (index,) + (0,) * len(self.shape),
                            pipeline_mode=pl.Buffered(1))


def _row_halves(tm):
    n_split = ROW_SPLIT if tm % (ROW_SPLIT * 16) == 0 else 1
    return [slice(r * tm // n_split, (r + 1) * tm // n_split) for r in range(n_split)]


def _token_units(x_of, mix_ref, p, o_ref, z_ref, h_ref, xn_ref, ff_chunk):
    tm, d_ff = h_ref.shape
    halves = _row_halves(tm)
    if mix_ref is not None:
        for rows in halves:
            y = jnp.dot(mix_ref[rows, :], p["wmix"][...], preferred_element_type=F32)
            o_ref[rows, :] = x_of(rows) + _rms(y) * p["gmix"][...]
            yield
        resid = lambda rows: o_ref[rows, :]
    else:
        resid = x_of
    xn = jnp.concatenate([(_rms(resid(rows)) * p["gpre"][...]).astype(BF16) for rows in halves], axis=0)
    if xn_ref is not None:
        xn_ref[...] = xn
    for j in range(d_ff // ff_chunk):
        lo = j * ff_chunk
        if xn_ref is not None:
            xn = xn_ref[...]
        gate = jnp.dot(xn, p["win"][:, lo:lo + ff_chunk], preferred_element_type=F32)
        yield
        up = jnp.dot(xn, p["win"][:, d_ff + lo:d_ff + lo + ff_chunk], preferred_element_type=F32)
        h_ref[:, lo:lo + ff_chunk] = (_silu(gate) * up).astype(BF16)
        yield
    ys = []
    for rows in halves:
        ys.append(jnp.dot(h_ref[rows, :], p["wout"][...], preferred_element_type=F32))
        yield
    for rows, y in zip(halves, ys):
        xr = resid(rows) + 0.5 * (_rms(y) * p["gpost"][...])
        if z_ref is not None:
            xn = (_rms(xr) * p["gproj"][...]).astype(BF16)
            z_ref[rows, :] = jnp.dot(xn, p["wproj"][...], preferred_element_type=F32).astype(BF16)
        o_ref[rows, :] = _rms(xr) * p["gfin"][...] if "gfin" in p else xr
        yield


def _n_token_units(tm, d_ff, ff_chunk, has_mix):
    return (3 if has_mix else 2) * len(_row_halves(tm)) + 2 * (d_ff // ff_chunk)


def _token_kernel(*refs, names, ff_chunk, n_main_tiles, split_in, has_mix, has_proj):
    refs = list(refs)
    take = lambda n: [refs.pop(0) for _ in range(n)]
    if split_in:
        xm_ref, xt_ref = take(2)
        in_main = pl.program_id(0) < n_main_tiles
        x_of = lambda rows: jnp.where(in_main, xm_ref[rows, :], xt_ref[rows, :])
    else:
        (x_ref,) = take(1)
        x_of = lambda rows: x_ref[rows, :]
    mix_ref = take(1)[0] if has_mix else None
    p = dict(zip(names, take(len(names))))
    (o_ref,) = take(1)
    z_ref = take(1)[0] if has_proj else None
    (h_ref,) = take(1)
    assert not refs
    for _ in _token_units(x_of, mix_ref, p, o_ref, z_ref, h_ref, None, ff_chunk):
        pass


def _ffn_chunk(d_ff):
    return FF_CHUNK if d_ff % FF_CHUNK == 0 else d_ff


def _weight_args(d, ffn, mix=None, proj=None, final=None):
    g_pre, w_in, w_out, g_post = ffn
    named = []
    if mix is not None:
        named += [("wmix", mix[0]), ("gmix", mix[1].reshape(1, d))]
    named += [("gpre", g_pre.reshape(1, d)), ("win", w_in), ("wout", w_out), ("gpost", g_post.reshape(1, d))]
    if proj is not None:
        named += [("gproj", proj[0].reshape(1, d)), ("wproj", proj[1])]
    if final is not None:
        named.append(("gfin", final.reshape(1, d)))
    names, arrays = zip(*named)
    specs = [a.spec if isinstance(a, _Layer) else _resident(a.shape) for a in arrays]
    return names, [a.stack if isinstance(a, _Layer) else a for a in arrays], specs


def _token_step(x, ffn, tm, *, n_out, first_tile=0, mix=None, proj=None, final=None):
    split_in = isinstance(x, tuple)
    d = x[0].shape[1] if split_in else x.shape[1]
    d_ff = ffn[2].shape[0]
    n_main_tiles = x[0].shape[0] // tm if split_in else 0
    args, in_specs = [], []
    if split_in:
        assert x[0].shape[0] % tm == 0 and x[1].shape[0] % tm == 0
        args += list(x)
        in_specs += [pl.BlockSpec((tm, d), lambda i: (jnp.minimum(i, n_main_tiles - 1), 0)),
                     pl.BlockSpec((tm, d), lambda i: (jnp.maximum(i - n_main_tiles, 0), 0))]
    else:
        args.append(x)
        in_specs.append(pl.BlockSpec((tm, d), lambda i: (first_tile + i, 0)))
    if mix is not None:
        args.append(mix[0])
        in_specs.append(pl.BlockSpec((tm, MIX_WIDTH), lambda i: (i, 0)))
    names, w_arrays, w_specs = _weight_args(d, ffn, mix[1:] if mix is not None else None, proj, final)
    out_shape = [jax.ShapeDtypeStruct((n_out, d), F32)]
    out_specs = [pl.BlockSpec((tm, d), lambda i: (i, 0))]
    if proj is not None:
        out_shape.append(jax.ShapeDtypeStruct((n_out, IN_WIDTH), BF16))
        out_specs.append(pl.BlockSpec((tm, IN_WIDTH), lambda i: (i, 0)))
    outs = pl.pallas_call(
        functools.partial(_token_kernel, names=names, ff_chunk=_ffn_chunk(d_ff), n_main_tiles=n_main_tiles,
                          split_in=split_in, has_mix=mix is not None, has_proj=proj is not None),
        grid=(n_out // tm,),
        in_specs=in_specs + w_specs,
        out_specs=out_specs,
        out_shape=out_shape,
        scratch_shapes=[pltpu.VMEM((tm, d_ff), BF16)],
        compiler_params=_compiler_params(1),
        name="token_step",
    )(*args, *w_arrays)
    return outs[0] if len(outs) == 1 else tuple(outs)


def _permute_ret_columns(w):
    lead = w.shape[:-1]
    return jnp.swapaxes(w.reshape(lead + (N_RG, RG, 2, HALF)), -3, -2).reshape(lead + (RET_WIDTH,))


def _projection_weight(w):
    parts = [w[..., :OFF_QR], _permute_ret_columns(w[..., OFF_QR:OFF_KR]),
             _permute_ret_columns(w[..., OFF_KR:OFF_VR]), w[..., OFF_VR:]]
    return jnp.concatenate(parts, axis=-1).astype(BF16)


def _mixer_tables(t, s_keys):
    f32 = np.float32
    log_g = np.log1p(-np.exp2(-5.0 - np.arange(RET_HEADS, dtype=f32))).astype(f32)
    freqs = (ROPE_BASE ** (-np.arange(HALF, dtype=f32) / HALF)).astype(f32)
    pos = np.arange(t, dtype=f32)
    ang = pos[:, None] * freqs[None, :]
    cos, sin = np.tile(np.cos(ang), (1, RG)), np.tile(np.sin(ang), (1, RG))
    qs = f32(RET_DK ** -0.5)
    rot = jnp.asarray(np.stack([cos * qs, sin * qs, cos, sin]), F32)

    head = lambda g: slice(g * RG, (g + 1) * RG)
    diff = pos[:, None] - pos[None, :]
    intra = np.where(diff >= 0, np.exp(log_g[:, None, None] * np.maximum(diff, 0)), 0).astype(f32)
    kv_dec = np.exp((t - pos)[:, None] * log_g[None, :])
    q_dec = np.exp(pos[:, None] * log_g[None, :])
    dec = []
    for g in range(N_RG):
        kd = np.tile(np.repeat(kv_dec[:, head(g)], HALF, axis=1), (1, 2))
        qd = np.repeat(q_dec[:, head(g)], RET_DV, axis=1)
        dec += [kd, qd]
    dec = jnp.asarray(np.stack(dec), F32)
    dtab = jnp.asarray(np.stack([np.concatenate(list(intra[head(g)]), axis=1) for g in range(N_RG)]), F32)

    g_t = np.exp(t * log_g)
    shift = -t * freqs
    stab = [np.broadcast_to(np.repeat(g_t[head(g)], RET_DV)[None, :], (HALF, RGW)) for g in range(N_RG)]
    stab += [np.broadcast_to(np.cos(shift)[:, None], (HALF, RGW)),
             np.broadcast_to(np.sin(shift)[:, None], (HALF, RGW))]
    stab = jnp.asarray(np.stack(stab), F32)

    r4 = np.arange(RG * t)[:, None] // t
    lane = np.arange(RGW)[None, :]
    mask_k = jnp.asarray(r4 == (lane % LANES) // HALF, BF16)
    mask_v = jnp.asarray(r4 == lane // RET_DV, BF16)
    rs = np.arange(LANES)[:, None] // HALF
    mask_s = jnp.asarray(rs == lane // RET_DV, F32)
    seg = jnp.asarray(np.arange(RGW)[:, None] // RET_DV == lane // RET_DV, BF16)

    rowk = np.arange(KEY_PAD)[:, None] < s_keys
    lane_v = np.arange(LANES)[None, :] // HEAD_DIM
    v_keep = jnp.stack([jnp.asarray(rowk & (lane_v == p), BF16) for p in range(2)])
    v_ones = jnp.stack([jnp.asarray(np.broadcast_to(lane_v != p, (KEY_PAD, LANES)), BF16) for p in range(2)])
    return dict(rot=rot, dec=dec, dtab=dtab, stab=stab, mask_k=mask_k, mask_v=mask_v, mask_s=mask_s, seg=seg,
                v_keep=v_keep, v_ones=v_ones)


_TABLE_ORDER = ("rot", "dec", "dtab", "stab", "mask_k", "mask_v", "mask_s", "seg", "v_keep", "v_ones")


def _fill_table(sinks_ref, fill_ref, t, s_keys):
    col = lax.broadcasted_iota(jnp.int32, (t, KEY_PAD), 1)
    for kv in range(ATTN_KV_HEADS):
        for j in range(GROUP):
            fill_ref[kv, j * t:(j + 1) * t, :] = jnp.where(col == s_keys, sinks_ref[kv * GROUP + j] * LOG2_E, NEG)


def _attn_scores(q, kwin):
    t = q.shape[0]
    lower = lax.broadcasted_iota(jnp.int32, (t, LANES), 1) < HEAD_DIM
    out = []
    for kv in range(ATTN_KV_HEADS):
        keep = jnp.where(lower if kv == 0 else ~lower, ATTN_SCALE * LOG2_E, 0.0)
        parts = []
        for j in range(2):
            blk = q[:, (2 * kv + j) * LANES:(2 * kv + j + 1) * LANES].astype(F32)
            rolled = pltpu.roll(blk, HEAD_DIM, axis=1)
            lo_head, hi_head = (blk, rolled) if kv == 0 else (rolled, blk)
            parts += [lo_head * keep, hi_head * keep]
        qs = jnp.concatenate(parts, axis=0).astype(BF16)
        out.append(lax.dot_general(qs, kwin, (((1,), (1,)), ((), ())), preferred_element_type=F32))
    return out


def _attn_values(scores, vwin, valid, fill_ref, tb):
    out = []
    for kv, s in enumerate(scores):
        s = jnp.where(valid, s, fill_ref[kv])
        p = jnp.exp2(s - jnp.max(s, axis=-1, keepdims=True))
        vext = vwin * tb["v_keep"][kv] + tb["v_ones"][kv]
        out.append(jnp.dot(p.astype(BF16), vext, preferred_element_type=F32))
    return out


def _attn_normalise(results, t):
    lower = lax.broadcasted_iota(jnp.int32, (t, LANES), 1) < HEAD_DIM
    cols = []
    for kv, res in enumerate(results):
        swapped = pltpu.roll(res, HEAD_DIM, axis=1)
        for j in range(2):
            lo_rows, hi_rows = slice(2 * j * t, (2 * j + 1) * t), slice((2 * j + 1) * t, (2 * j + 2) * t)
            if kv == 0:
                lo, hi = res[lo_rows] / swapped[lo_rows], swapped[hi_rows] / res[hi_rows]
            else:
                lo, hi = swapped[lo_rows] / res[lo_rows], res[hi_rows] / swapped[hi_rows]
            cols.append(jnp.where(lower, lo, hi))
    return jnp.concatenate(cols, axis=-1)


def _ret_scores(zcols, g, tb, row_keep):
    lo = g * RGW
    cq, sq, ck, sk = tb["rot"][0], tb["rot"][1], tb["rot"][2], tb["rot"][3]
    q1 = zcols(OFF_QR + lo, OFF_QR + lo + LANES).astype(F32)
    q2 = zcols(OFF_QR + lo + LANES, OFF_QR + lo + RGW).astype(F32)
    k1 = zcols(OFF_KR + lo, OFF_KR + lo + LANES).astype(F32)
    k2 = zcols(OFF_KR + lo + LANES, OFF_KR + lo + RGW).astype(F32)
    if row_keep is not None:
        k1, k2 = k1 * row_keep, k2 * row_keep
    q_rot = jnp.concatenate([q1 * cq - q2 * sq, q2 * cq + q1 * sq], axis=-1)
    k_rot = jnp.concatenate([k1 * ck - k2 * sk, k2 * ck + k1 * sk], axis=-1)
    qb, kb = q_rot.astype(BF16), k_rot.astype(BF16)
    kdb = (k_rot * tb["dec"][2 * g]).astype(BF16)
    vb = zcols(OFF_VR + lo, OFF_VR + lo + RGW)
    k_bd = jnp.concatenate([kb] * RG, axis=0) * tb["mask_k"][...]
    scores = lax.dot_general(qb, k_bd, (((1,), (1,)), ((), ())), preferred_element_type=F32)
    w_full = lax.dot_general(kdb, vb, (((0,), (0,)), ((), ())), preferred_element_type=F32)
    return qb, vb, scores, w_full


def _ret_outputs(g, qb, vb, scores, w_full, state_ref, tb):
    mask_s = tb["mask_s"][...]
    v_bd = jnp.concatenate([vb] * RG, axis=0) * tb["mask_v"][...]
    intra = jnp.dot((scores * tb["dtab"][g]).astype(BF16), v_bd, preferred_element_type=F32)
    c1, c2 = state_ref[g, 0:HALF, :], state_ref[g, HALF:2 * HALF, :]
    mask_sb = mask_s.astype(BF16)
    s_bd = jnp.concatenate([jnp.concatenate([c.astype(BF16)] * RG, axis=0) * mask_sb for c in (c1, c2)], axis=0)
    cross = jnp.dot(qb, s_bd, preferred_element_type=F32)
    w = []
    for half in range(2):
        wm = w_full[half * LANES:(half + 1) * LANES] * mask_s
        w.append(wm[0:HALF] + wm[HALF:2 * HALF] + wm[2 * HALF:3 * HALF] + wm[3 * HALF:4 * HALF])
    cos_s, sin_s = tb["stab"][N_RG], tb["stab"][N_RG + 1]
    a1 = tb["stab"][g] * c1 + w[0]
    a2 = tb["stab"][g] * c2 + w[1]
    state_ref[g, 0:HALF, :] = a1 * cos_s - a2 * sin_s
    state_ref[g, HALF:2 * HALF, :] = a2 * cos_s + a1 * sin_s
    return intra, cross


def _ret_square_sums(outs, tb):
    parts = []
    for o in outs:
        sq = o * o
        hi = sq.astype(BF16)
        parts += [hi, (sq - hi.astype(F32)).astype(BF16)]
    return jnp.dot(jnp.concatenate(parts, axis=0), tb["seg"][...], preferred_element_type=F32)


def _mixer_stages(zcols, kwin, vwin, valid, fill_ref, state_ref, tb, row_keep, emit_attn, emit_ret):
    attn_scores = _attn_scores(zcols(OFF_QA, OFF_QA + ATTN_WIDTH), kwin())
    ret = [_ret_scores(zcols, g, tb, row_keep) for g in range(N_RG)]
    t = ret[0][0].shape[0]
    yield
    attn_res = _attn_values(attn_scores, vwin(), valid, fill_ref, tb)
    pairs = [_ret_outputs(g, *ret[g], state_ref, tb) for g in range(N_RG)]
    yield
    emit_attn(_attn_normalise(attn_res, t).astype(BF16))
    outs = [intra + cross * tb["dec"][2 * g + 1] for g, (intra, cross) in enumerate(pairs)]
    ssq = _ret_square_sums(outs, tb)
    yield
    res = []
    for g, o in enumerate(outs):
        gate = zcols(OFF_GR + g * RGW, OFF_GR + (g + 1) * RGW).astype(F32)
        ss = ssq[2 * g * t:(2 * g + 1) * t] + ssq[(2 * g + 1) * t:(2 * g + 2) * t]
        res.append(o * lax.rsqrt(ss * (1.0 / RET_DV) + EPS) * _silu(gate))
    emit_ret(jnp.concatenate(res, axis=-1).astype(BF16))
    yield


N_STAGES = 4
FIRST_EMIT_STAGE = 2

def _stage_order(n_blocks):
    order = []
    for step in range(n_blocks + N_STAGES - 1):
        order += [step - stage for stage in reversed(range(N_STAGES)) if 0 <= step - stage < n_blocks]
    return order


def _run_blocks(blocks):
    for b in _stage_order(len(blocks)):
        next(blocks[b])


def _state_to_heads(state_ref, out_ref):
    for g in range(N_RG):
        for hh in range(RG):
            out_ref[0, g * RG + hh] = state_ref[g, :, hh * RET_DV:(hh + 1) * RET_DV]


def _prompt_kernel(sinks_ref, z_ref, zmeta_ref, x_ref, *rest, names, tile_chunks, n_g, n_tiles, lead, ff_chunk):
    rest = list(rest)
    take = lambda n: [rest.pop(0) for _ in range(n)]
    p = dict(zip(names, take(len(names))))
    tb = dict(zip(_TABLE_ORDER, take(len(_TABLE_ORDER))))
    o_ref, mixmeta_ref, klast_ref, vlast_ref, sfin_ref = take(5)
    mix_ref, h_ref, xn_ref, kctx_ref, vctx_ref, fill_ref, state_ref = take(7)
    assert not rest
    s = pl.program_id(0)
    g = lax.rem(jnp.minimum(s, n_tiles - 1), n_g)
    ctx = WINDOW_CHUNKS * CHUNK
    rows = tile_chunks * CHUNK
    s_keys = ctx + CHUNK
    col = lax.broadcasted_iota(jnp.int32, (1, KEY_PAD), 1)

    def put_keys(dst, z, n):
        kctx_ref[dst:dst + n, :] = z[:, OFF_KA:OFF_KA + KV_WIDTH]
        vctx_ref[dst:dst + n, :] = z[:, OFF_VA:OFF_VA + KV_WIDTH]

    def block(z_src, out_ref, r0, first_row, row_keep=None):
        rows_c = slice(r0, r0 + CHUNK)
        win = slice(r0, r0 + KEY_PAD)
        valid = (col < s_keys) & (col + (first_row - ctx) >= lead)

        def emit(lo):
            def store(v):
                out_ref[rows_c, lo:lo + v.shape[1]] = v
            return store

        return _mixer_stages(lambda lo, hi: z_src[rows_c, lo:hi], lambda: kctx_ref[win, :], lambda: vctx_ref[win, :],
                             valid, fill_ref, state_ref, tb, row_keep, emit(0), emit(ATTN_WIDTH))

    @pl.when(s == 0)
    def _():
        mix_ref[...] = jnp.zeros(mix_ref.shape, BF16)

    @pl.when(g == 0)
    def _():
        for r in (kctx_ref, vctx_ref):
            r[...] = jnp.zeros(r.shape, BF16)
        state_ref[...] = jnp.zeros(state_ref.shape, F32)
        _fill_table(sinks_ref, fill_ref, CHUNK, s_keys)
        zm = zmeta_ref[...]
        put_keys(ctx, zm, CHUNK)
        row_keep = (lax.broadcasted_iota(jnp.int32, (CHUNK, 1), 0) >= lead).astype(F32)
        _run_blocks([block(zmeta_ref, mixmeta_ref, 0, 0, row_keep)])
        put_keys(ctx - CHUNK, zm, CHUNK)

    units = _token_units(lambda r: x_ref[r, :], mix_ref, p, o_ref, None, h_ref, xn_ref, ff_chunk)
    n_units = _n_token_units(rows, h_ref.shape[1], ff_chunk, True)
    n_mix_reads = len(_row_halves(rows))

    put_keys(ctx, z_ref[...], rows)
    tile_row = (1 + g * tile_chunks) * CHUNK
    blocks = [block(z_ref, mix_ref, c * CHUNK, tile_row + c * CHUNK) for c in range(tile_chunks)]
    order = _stage_order(len(blocks))
    first_emit = [i for i, b in enumerate(order) if b == 0][FIRST_EMIT_STAGE]
    issued = 0
    for i, b in enumerate(order):
        if i == first_emit:
            while issued < n_mix_reads:
                next(units)
                issued += 1
        next(blocks[b])
        target = max(issued, -(-n_units * (i + 1) // len(order)))
        for _ in range(target - issued):
            next(units)
        issued = target
    assert next(units, None) is None

    for r in (kctx_ref, vctx_ref):
        r[0:ctx, :] = r[rows:rows + ctx, :]

    @pl.when((g == n_g - 1) & (s < n_tiles))
    def _():
        klast_ref[0] = kctx_ref[0:ctx, :].astype(F32)
        vlast_ref[0] = vctx_ref[0:ctx, :].astype(F32)
        _state_to_heads(state_ref, sfin_ref)


def _prompt_step(z, x, sinks, tables, ffn, mix, batch, seq, lead, final=None):
    d = x.shape[1]
    d_ff = ffn[2].shape[0]
    n_main = batch * seq
    n_chunks = seq // CHUNK
    tile_chunks = _largest_divisor(n_chunks, MAX_TILE_CHUNKS)
    n_g = n_chunks // tile_chunks
    n_tiles = batch * n_g
    rows = tile_chunks * CHUNK
    ctx = WINDOW_CHUNKS * CHUNK
    assert rows >= ctx and n_main % CHUNK == 0
    meta_block0 = n_main // CHUNK
    tabs = [tables[k] for k in _TABLE_ORDER]
    names, w_arrays, w_specs = _weight_args(d, ffn, mix, None, final)
    state_shape = (RET_HEADS, RET_DK, RET_DV)
    ctx_rows = ctx + rows + KEY_PAD - (ctx + CHUNK)
    ctx_buf = pltpu.VMEM((ctx_rows, KV_WIDTH), BF16)
    cur = lambda s: jnp.minimum(s, n_tiles - 1)
    prev = lambda s: jnp.maximum(s - 1, 0)
    seq_of = lambda s: cur(s) // n_g
    return pl.pallas_call(
        functools.partial(_prompt_kernel, names=names, tile_chunks=tile_chunks, n_g=n_g, n_tiles=n_tiles, lead=lead,
                          ff_chunk=_ffn_chunk(d_ff)),
        grid=(n_tiles + 1,),
        in_specs=[pl.BlockSpec(memory_space=pltpu.SMEM),
                  pl.BlockSpec((rows, IN_WIDTH), lambda s: (cur(s), 0)),
                  pl.BlockSpec((CHUNK, IN_WIDTH), lambda s: (meta_block0, 0)),
                  pl.BlockSpec((rows, d), lambda s: (prev(s), 0))]
                 + w_specs + [_resident(t.shape) for t in tabs],
        out_specs=[pl.BlockSpec((rows, d), lambda s: (prev(s), 0)),
                   pl.BlockSpec((CHUNK, MIX_WIDTH), lambda s: (0, 0)),
                   pl.BlockSpec((1, ctx, KV_WIDTH), lambda s: (seq_of(s), 0, 0)),
                   pl.BlockSpec((1, ctx, KV_WIDTH), lambda s: (seq_of(s), 0, 0)),
                   pl.BlockSpec((1,) + state_shape, lambda s: (seq_of(s), 0, 0, 0))],
        out_shape=[jax.ShapeDtypeStruct((n_main, d), F32),
                   jax.ShapeDtypeStruct((CHUNK, MIX_WIDTH), BF16),
                   jax.ShapeDtypeStruct((batch, ctx, KV_WIDTH), F32),
                   jax.ShapeDtypeStruct((batch, ctx, KV_WIDTH), F32),
                   jax.ShapeDtypeStruct((batch,) + state_shape, F32)],
        scratch_shapes=[pltpu.VMEM((rows, MIX_WIDTH), BF16), pltpu.VMEM((rows, d_ff), BF16),
                        pltpu.VMEM((rows, d), BF16), ctx_buf, ctx_buf,
                        pltpu.VMEM((ATTN_KV_HEADS, GROUP * CHUNK, KEY_PAD), F32),
                        pltpu.VMEM((N_RG, RET_DK, RGW), F32)],
        compiler_params=_compiler_params(1),
        name="prompt_step",
    )(sinks, z, z, x, *w_arrays, *tabs)


def _sample_mixer_kernel(sinks_ref, z_ref, ck_ref, cv_ref, s0_ref, *rest, cache_rows):
    tb = dict(zip(_TABLE_ORDER, rest[:len(_TABLE_ORDER)]))
    mix_ref, knew_ref, vnew_ref, snew_ref, fill_ref, state_ref = rest[len(_TABLE_ORDER):]
    z = z_ref[...]
    t = z.shape[0]
    s_keys = cache_rows + t
    pad = jnp.zeros((KEY_PAD - s_keys, KV_WIDTH), F32)
    k_all = jnp.concatenate([ck_ref[0], z[:, OFF_KA:OFF_KA + KV_WIDTH].astype(F32)], axis=0)
    v_all = jnp.concatenate([cv_ref[0], z[:, OFF_VA:OFF_VA + KV_WIDTH].astype(F32)], axis=0)
    windows = [jnp.concatenate([a, pad], axis=0).astype(BF16) for a in (k_all, v_all)]
    _fill_table(sinks_ref, fill_ref, t, s_keys)
    for g in range(N_RG):
        for hh in range(RG):
            state_ref[g, :, hh * RET_DV:(hh + 1) * RET_DV] = s0_ref[0, g * RG + hh]
    col = lax.broadcasted_iota(jnp.int32, (1, KEY_PAD), 1)

    def emit(lo):
        def store(v):
            mix_ref[:, lo:lo + v.shape[1]] = v
        return store

    _run_blocks([_mixer_stages(lambda lo, hi: z_ref[:, lo:hi], lambda: windows[0], lambda: windows[1], col < s_keys,
                               fill_ref, state_ref, tb, None, emit(0), emit(ATTN_WIDTH))])
    knew_ref[0] = k_all[s_keys - cache_rows:]
    vnew_ref[0] = v_all[s_keys - cache_rows:]
    _state_to_heads(state_ref, snew_ref)


def _sample_mixer(z, sinks, tables, cache_k, cache_v, state, batch, t, row_offset):
    cache_rows = cache_k.shape[1]
    assert row_offset % t == 0 and cache_rows + t < KEY_PAD
    first_block = row_offset // t
    tabs = [tables[k] for k in _TABLE_ORDER]
    state_shape = (RET_HEADS, RET_DK, RET_DV)
    cache_spec = pl.BlockSpec((1, cache_rows, KV_WIDTH), lambda b: (b, 0, 0))
    state_spec = pl.BlockSpec((1,) + state_shape, lambda b: (b, 0, 0, 0))
    return pl.pallas_call(
        functools.partial(_sample_mixer_kernel, cache_rows=cache_rows),
        grid=(batch,),
        in_specs=[pl.BlockSpec(memory_space=pltpu.SMEM),
                  pl.BlockSpec((t, IN_WIDTH), lambda b: (first_block + b, 0)),
                  cache_spec, cache_spec, state_spec] + [_resident(x.shape) for x in tabs],
        out_specs=[pl.BlockSpec((t, MIX_WIDTH), lambda b: (b, 0)), cache_spec, cache_spec, state_spec],
        out_shape=[jax.ShapeDtypeStruct((batch * t, MIX_WIDTH), BF16),
                   jax.ShapeDtypeStruct(cache_k.shape, F32),
                   jax.ShapeDtypeStruct(cache_v.shape, F32),
                   jax.ShapeDtypeStruct(state.shape, F32)],
        scratch_shapes=[pltpu.VMEM((ATTN_KV_HEADS, GROUP * t, KEY_PAD), F32),
                        pltpu.VMEM((N_RG, RET_DK, RGW), F32)],
        compiler_params=_compiler_params(1),
        name="sample_mixer",
    )(sinks, z, cache_k, cache_v, state, *tabs)


def kernel(x_prompt, x_sample, cache_swa_k, cache_swa_v, state_ret, meta_tokens, w_in, w_out, attn_sinks,
           ffn1_w_in, ffn1_w_out, ffn2_w_in, ffn2_w_out, norm_ffn1_pre, norm_ffn1_post, norm_mix_pre,
           norm_mix_post, norm_ffn2_pre, norm_ffn2_post, final_norm):
    batch, seq, d = x_prompt.shape
    dec_batch, dec_seq, _ = x_sample.shape
    depth = w_in.shape[0]
    cache_rows = cache_swa_k.shape[2]
    lead = CHUNK - N_META
    n_main, n_meta, n_sample = batch * seq, CHUNK, dec_batch * dec_seq
    tm = _row_tile(n_main)
    n_tail = -(-(n_meta + n_sample) // tm) * tm
    n_pad = n_tail - n_meta - n_sample
    assert w_in.shape[2] == IN_WIDTH and w_out.shape[1] == MIX_WIDTH
    assert seq % CHUNK == 0 and cache_rows == WINDOW_CHUNKS * CHUNK and seq >= cache_rows

    x_tail = jnp.concatenate([jnp.zeros((lead, d), x_prompt.dtype), meta_tokens.astype(x_prompt.dtype),
                              x_sample.reshape(n_sample, d), jnp.zeros((n_pad, d), x_prompt.dtype)], axis=0)
    x = (x_prompt.reshape(n_main, d), x_tail)

    tables_p = _mixer_tables(CHUNK, WINDOW_CHUNKS * CHUNK + CHUNK)
    tables_s = _mixer_tables(dec_seq, cache_rows + dec_seq)
    cache_k = cache_swa_k.reshape(depth, dec_batch, cache_rows, KV_WIDTH)
    cache_v = cache_swa_v.reshape(depth, dec_batch, cache_rows, KV_WIDTH)

    w_proj, w_mix = _projection_weight(w_in), w_out.astype(BF16)
    w1_in, w1_out = ffn1_w_in.astype(BF16), ffn1_w_out.astype(BF16)
    w2_in, w2_out = ffn2_w_in.astype(BF16), ffn2_w_out.astype(BF16)

    pk, pv, ps, sk, sv, ss = [], [], [], [], [], []
    for l in range(depth):
        ffn1 = (norm_ffn1_pre[l], _Layer(w1_in, l), _Layer(w1_out, l), norm_ffn1_post[l])
        ffn2 = (norm_ffn2_pre[l], _Layer(w2_in, l), _Layer(w2_out, l), norm_ffn2_post[l])
        mix_w = (_Layer(w_mix, l), norm_mix_post[l])
        final = final_norm if l == depth - 1 else None
        x1, z = _token_step(x, ffn1, tm, n_out=n_main + n_tail, proj=(norm_mix_pre[l], _Layer(w_proj, l)))
        x_main, mix_meta, k_p, v_p, s_p = _prompt_step(z, x1, attn_sinks[l], tables_p, ffn2, mix_w, batch, seq, lead,
                                                       final)
        mix_s, k_s, v_s, s_s = _sample_mixer(z, attn_sinks[l], tables_s, cache_k[l], cache_v[l], state_ret[l],
                                             dec_batch, dec_seq, n_main + n_meta)
        mix_tail = jnp.concatenate([mix_meta, mix_s, jnp.zeros((n_pad, MIX_WIDTH), BF16)], axis=0)
        x_tail = _token_step(x1, ffn2, tm, n_out=n_tail, first_tile=n_main // tm, mix=(mix_tail,) + mix_w,
                             final=final)
        x = (x_main, x_tail)
        pk.append(k_p); pv.append(v_p); ps.append(s_p)
        sk.append(k_s); sv.append(v_s); ss.append(s_s)

    y_main, y_tail = x
    y_prompt = y_main.reshape(batch, seq, d)
    y_sample = y_tail[n_meta:n_meta + n_sample].reshape(dec_batch, dec_seq, d)
    kv_shape_p = (depth, batch, cache_rows, ATTN_KV_HEADS, HEAD_DIM)
    kv_shape_s = (depth, dec_batch, cache_rows, ATTN_KV_HEADS, HEAD_DIM)
    return (y_prompt, y_sample,
            jnp.stack(pk).reshape(kv_shape_p), jnp.stack(pv).reshape(kv_shape_p), jnp.stack(ps),
            jnp.stack(sk).reshape(kv_shape_s), jnp.stack(sv).reshape(kv_shape_s), jnp.stack(ss))
```

```python
import functools
import math
from typing import NamedTuple

import jax
import jax.numpy as jnp
import numpy as np
from jax import lax
from jax.experimental import pallas as pl
from jax.experimental.pallas import tpu as pltpu

CHUNK = 64
N_META = 16
WINDOW_CHUNKS = 2
ATTN_HEADS = 8
ATTN_KV_HEADS = 2
HEAD_DIM = 64
GROUP = ATTN_HEADS // ATTN_KV_HEADS
ATTN_SCALE = HEAD_DIM ** -0.5
LOG2_E = math.log2(math.e)
RET_HEADS = 8
RET_DK = 64
RET_DV = 64
ATTN_WIDTH = ATTN_HEADS * HEAD_DIM
KV_WIDTH = ATTN_KV_HEADS * HEAD_DIM
RET_WIDTH = RET_HEADS * RET_DK
ROPE_BASE = 10000.0
EPS = 1e-6
NEG = -1e30

OFF_QA = 0
OFF_KA = OFF_QA + ATTN_WIDTH
OFF_VA = OFF_KA + KV_WIDTH
OFF_QR = OFF_VA + KV_WIDTH
OFF_KR = OFF_QR + RET_WIDTH
OFF_VR = OFF_KR + RET_WIDTH
OFF_GR = OFF_VR + RET_WIDTH
IN_WIDTH = OFF_GR + RET_WIDTH
MIX_WIDTH = ATTN_WIDTH + RET_WIDTH

LANES = 128
HALF = RET_DK // 2
RG = 4
RGW = RG * RET_DK
N_RG = RET_HEADS // RG
KEY_PAD = 256
V7X_VMEM_LIMIT_BYTES = 56 * 1024 * 1024
FF_CHUNK = 256
MAX_ROW_TILE = 512
ROW_SPLIT = 2
MAX_TILE_CHUNKS = 8

BF16 = jnp.bfloat16
F32 = jnp.float32

assert RGW == 2 * LANES and KV_WIDTH == LANES and GROUP == 4 and ATTN_KV_HEADS == 2


def _row_tile(n, cap=MAX_ROW_TILE):
    t = cap
    while t > 8 and n % t:
        t //= 2
    assert n % t == 0, (n, t)
    return t


def _largest_divisor(n, cap):
    return max(d for d in range(1, cap + 1) if n % d == 0)


def _rms(x):
    return x * lax.rsqrt(jnp.mean(x * x, axis=-1, keepdims=True) + EPS)


def _silu(x):
    return x / (1.0 + jnp.exp(-x))


def _compiler_params(n_axes):
    return pltpu.CompilerParams(dimension_semantics=("arbitrary",) * n_axes,
                                vmem_limit_bytes=V7X_VMEM_LIMIT_BYTES)


def _resident(shape):
    return pl.BlockSpec(shape, lambda *_: (0,) * len(shape), pipeline_mode=pl.Buffered(1))


class _Layer(NamedTuple):
    stack: jax.Array
    index: int

    @property
    def shape(self):
        return self.stack.shape[1:]

    @property
    def spec(self):
        index = self.index
        return pl.BlockSpec((None,) + self.shape, lambda *_: (index,) + (0,) * len(self.shape),
                            pipeline_mode=pl.Buffered(1))


def _row_halves(tm):
    n_split = ROW_SPLIT if tm % (ROW_SPLIT * 16) == 0 else 1
    return [slice(r * tm // n_split, (r + 1) * tm // n_split) for r in range(n_split)]


def _token_units(x_of, mix_ref, p, o_ref, z_ref, h_ref, xn_ref, ff_chunk):
    tm, d_ff = h_ref.shape
    halves = _row_halves(tm)
    if mix_ref is not None:
        for rows in halves:
            y = jnp.dot(mix_ref[rows, :], p["wmix"][...], preferred_element_type=F32)
            o_ref[rows, :] = x_of(rows) + _rms(y) * p["gmix"][...]
            yield
        resid = lambda rows: o_ref[rows, :]
    else:
        resid = x_of
    xn = jnp.concatenate([(_rms(resid(rows)) * p["gpre"][...]).astype(BF16) for rows in halves], axis=0)
    if xn_ref is not None:
        xn_ref[...] = xn
    for j in range(d_ff // ff_chunk):
        lo = j * ff_chunk
        if xn_ref is not None:
            xn = xn_ref[...]
        gate = jnp.dot(xn, p["win"][:, lo:lo + ff_chunk], preferred_element_type=F32)
        yield
        up = jnp.dot(xn, p["win"][:, d_ff + lo:d_ff + lo + ff_chunk], preferred_element_type=F32)
        h_ref[:, lo:lo + ff_chunk] = (_silu(gate) * up).astype(BF16)
        yield
    ys = []
    for rows in halves:
        ys.append(jnp.dot(h_ref[rows, :], p["wout"][...], preferred_element_type=F32))
        yield
    for rows, y in zip(halves, ys):
        xr = resid(rows) + 0.5 * (_rms(y) * p["gpost"][...])
        if z_ref is not None:
            xn = (_rms(xr) * p["gproj"][...]).astype(BF16)
            z_ref[rows, :] = jnp.dot(xn, p["wproj"][...], preferred_element_type=F32).astype(BF16)
        o_ref[rows, :] = _rms(xr) * p["gfin"][...] if "gfin" in p else xr
        yield


def _n_token_units(tm, d_ff, ff_chunk, has_mix):
    return (3 if has_mix else 2) * len(_row_halves(tm)) + 2 * (d_ff // ff_chunk)


def _token_kernel(*refs, names, ff_chunk, n_main_tiles, split_in, has_mix, has_proj):
    refs = list(refs)
    take = lambda n: [refs.pop(0) for _ in range(n)]
    if split_in:
        xm_ref, xt_ref = take(2)
        in_main = pl.program_id(0) < n_main_tiles
        x_of = lambda rows: jnp.where(in_main, xm_ref[rows, :], xt_ref[rows, :])
    else:
        (x_ref,) = take(1)
        x_of = lambda rows: x_ref[rows, :]
    mix_ref = take(1)[0] if has_mix else None
    p = dict(zip(names, take(len(names))))
    (o_ref,) = take(1)
    z_ref = take(1)[0] if has_proj else None
    (h_ref,) = take(1)
    assert not refs
    for _ in _token_units(x_of, mix_ref, p, o_ref, z_ref, h_ref, None, ff_chunk):
        pass


def _ffn_chunk(d_ff):
    return FF_CHUNK if d_ff % FF_CHUNK == 0 else d_ff


def _weight_args(d, ffn, mix=None, proj=None, final=None):
    g_pre, w_in, w_out, g_post = ffn
    named = []
    if mix is not None:
        named += [("wmix", mix[0]), ("gmix", mix[1].reshape(1, d))]
    named += [("gpre", g_pre.reshape(1, d)), ("win", w_in), ("wout", w_out), ("gpost", g_post.reshape(1, d))]
    if proj is not None:
        named += [("gproj", proj[0].reshape(1, d)), ("wproj", proj[1])]
    if final is not None:
        named.append(("gfin", final.reshape(1, d)))
    names, arrays = zip(*named)
    specs = [a.spec if isinstance(a, _Layer) else _resident(a.shape) for a in arrays]
    return names, [a.stack if isinstance(a, _Layer) else a for a in arrays], specs


def _token_step(x, ffn, tm, *, n_out, first_tile=0, mix=None, proj=None, final=None):
    split_in = isinstance(x, tuple)
    d = x[0].shape[1] if split_in else x.shape[1]
    d_ff = ffn[2].shape[0]
    n_main_tiles = x[0].shape[0] // tm if split_in else 0
    args, in_specs = [], []
    if split_in:
        assert x[0].shape[0] % tm == 0 and x[1].shape[0] % tm == 0
        args += list(x)
        in_specs += [pl.BlockSpec((tm, d), lambda i: (jnp.minimum(i, n_main_tiles - 1), 0)),
                     pl.BlockSpec((tm, d), lambda i: (jnp.maximum(i - n_main_tiles, 0), 0))]
    else:
        args.append(x)
        in_specs.append(pl.BlockSpec((tm, d), lambda i: (first_tile + i, 0)))
    if mix is not None:
        args.append(mix[0])
        in_specs.append(pl.BlockSpec((tm, MIX_WIDTH), lambda i: (i, 0)))
    names, w_arrays, w_specs = _weight_args(d, ffn, mix[1:] if mix is not None else None, proj, final)
    out_shape = [jax.ShapeDtypeStruct((n_out, d), F32)]
    out_specs = [pl.BlockSpec((tm, d), lambda i: (i, 0))]
    if proj is not None:
        out_shape.append(jax.ShapeDtypeStruct((n_out, IN_WIDTH), BF16))
        out_specs.append(pl.BlockSpec((tm, IN_WIDTH), lambda i: (i, 0)))
    outs = pl.pallas_call(
        functools.partial(_token_kernel, names=names, ff_chunk=_ffn_chunk(d_ff), n_main_tiles=n_main_tiles,
                          split_in=split_in, has_mix=mix is not None, has_proj=proj is not None),
        grid=(n_out // tm,),
        in_specs=in_specs + w_specs,
        out_specs=out_specs,
        out_shape=out_shape,
        scratch_shapes=[pltpu.VMEM((tm, d_ff), BF16)],
        compiler_params=_compiler_params(1),
        name="token_step",
    )(*args, *w_arrays)
    return outs[0] if len(outs) == 1 else tuple(outs)


def _permute_ret_columns(w):
    lead = w.shape[:-1]
    return jnp.swapaxes(w.reshape(lead + (N_RG, RG, 2, HALF)), -3, -2).reshape(lead + (RET_WIDTH,))


def _projection_weight(w):
    parts = [w[..., :OFF_QR], _permute_ret_columns(w[..., OFF_QR:OFF_KR]),
             _permute_ret_columns(w[..., OFF_KR:OFF_VR]), w[..., OFF_VR:]]
    return jnp.concatenate(parts, axis=-1).astype(BF16)


def _mixer_tables(t, s_keys):
    f32 = np.float32
    log_g = np.log1p(-np.exp2(-5.0 - np.arange(RET_HEADS, dtype=f32))).astype(f32)
    freqs = (ROPE_BASE ** (-np.arange(HALF, dtype=f32) / HALF)).astype(f32)
    pos = np.arange(t, dtype=f32)
    ang = pos[:, None] * freqs[None, :]
    cos, sin = np.tile(np.cos(ang), (1, RG)), np.tile(np.sin(ang), (1, RG))
    qs = f32(RET_DK ** -0.5)
    rot = jnp.asarray(np.stack([cos * qs, sin * qs, cos, sin]), F32)

    head = lambda g: slice(g * RG, (g + 1) * RG)
    diff = pos[:, None] - pos[None, :]
    intra = np.where(diff >= 0, np.exp(log_g[:, None, None] * np.maximum(diff, 0)), 0).astype(f32)
    kv_dec = np.exp((t - pos)[:, None] * log_g[None, :])
    q_dec = np.exp(pos[:, None] * log_g[None, :])
    dec = []
    for g in range(N_RG):
        kd = np.tile(np.repeat(kv_dec[:, head(g)], HALF, axis=1), (1, 2))
        qd = np.repeat(q_dec[:, head(g)], RET_DV, axis=1)
        dec += [kd, qd]
    dec = jnp.asarray(np.stack(dec), F32)
    dtab = jnp.asarray(np.stack([np.concatenate(list(intra[head(g)]), axis=1) for g in range(N_RG)]), F32)

    g_t = np.exp(t * log_g)
    shift = -t * freqs
    stab = [np.broadcast_to(np.repeat(g_t[head(g)], RET_DV)[None, :], (HALF, RGW)) for g in range(N_RG)]
    stab += [np.broadcast_to(np.cos(shift)[:, None], (HALF, RGW)),
             np.broadcast_to(np.sin(shift)[:, None], (HALF, RGW))]
    stab = jnp.asarray(np.stack(stab), F32)

    r4 = np.arange(RG * t)[:, None] // t
    lane = np.arange(RGW)[None, :]
    mask_k = jnp.asarray(r4 == (lane % LANES) // HALF, BF16)
    mask_v = jnp.asarray(r4 == lane // RET_DV, BF16)
    rs = np.arange(LANES)[:, None] // HALF
    mask_s = jnp.asarray(rs == lane // RET_DV, F32)
    seg = jnp.asarray(np.arange(RGW)[:, None] // RET_DV == lane // RET_DV, BF16)

    rowk = np.arange(KEY_PAD)[:, None] < s_keys
    lane_v = np.arange(LANES)[None, :] // HEAD_DIM
    v_keep = jnp.stack([jnp.asarray(rowk & (lane_v == p), BF16) for p in range(2)])
    v_ones = jnp.stack([jnp.asarray(np.broadcast_to(lane_v != p, (KEY_PAD, LANES)), BF16) for p in range(2)])
    return dict(rot=rot, dec=dec, dtab=dtab, stab=stab, mask_k=mask_k, mask_v=mask_v, mask_s=mask_s, seg=seg,
                v_keep=v_keep, v_ones=v_ones)


_TABLE_ORDER = ("rot", "dec", "dtab", "stab", "mask_k", "mask_v", "mask_s", "seg", "v_keep", "v_ones")


def _fill_table(sinks_ref, fill_ref, t, s_keys):
    col = lax.broadcasted_iota(jnp.int32, (t, KEY_PAD), 1)
    for kv in range(ATTN_KV_HEADS):
        for j in range(GROUP):
            fill_ref[kv, j * t:(j + 1) * t, :] = jnp.where(col == s_keys, sinks_ref[kv * GROUP + j] * LOG2_E, NEG)


def _attn_scores(q, kwin):
    t = q.shape[0]
    lower = lax.broadcasted_iota(jnp.int32, (t, LANES), 1) < HEAD_DIM
    out = []
    for kv in range(ATTN_KV_HEADS):
        keep = jnp.where(lower if kv == 0 else ~lower, ATTN_SCALE * LOG2_E, 0.0)
        parts = []
        for j in range(2):
            blk = q[:, (2 * kv + j) * LANES:(2 * kv + j + 1) * LANES].astype(F32)
            rolled = pltpu.roll(blk, HEAD_DIM, axis=1)
            lo_head, hi_head = (blk, rolled) if kv == 0 else (rolled, blk)
            parts += [lo_head * keep, hi_head * keep]
        qs = jnp.concatenate(parts, axis=0).astype(BF16)
        out.append(lax.dot_general(qs, kwin, (((1,), (1,)), ((), ())), preferred_element_type=F32))
    return out


def _attn_values(scores, vwin, valid, fill_ref, tb):
    out = []
    for kv, s in enumerate(scores):
        s = jnp.where(valid, s, fill_ref[kv])
        p = jnp.exp2(s - jnp.max(s, axis=-1, keepdims=True))
        vext = vwin * tb["v_keep"][kv] + tb["v_ones"][kv]
        out.append(jnp.dot(p.astype(BF16), vext, preferred_element_type=F32))
    return out


def _attn_normalise(results, t):
    lower = lax.broadcasted_iota(jnp.int32, (t, LANES), 1) < HEAD_DIM
    cols = []
    for kv, res in enumerate(results):
        swapped = pltpu.roll(res, HEAD_DIM, axis=1)
        for j in range(2):
            lo_rows, hi_rows = slice(2 * j * t, (2 * j + 1) * t), slice((2 * j + 1) * t, (2 * j + 2) * t)
            if kv == 0:
                lo, hi = res[lo_rows] / swapped[lo_rows], swapped[hi_rows] / res[hi_rows]
            else:
                lo, hi = swapped[lo_rows] / res[lo_rows], res[hi_rows] / swapped[hi_rows]
            cols.append(jnp.where(lower, lo, hi))
    return jnp.concatenate(cols, axis=-1)


def _ret_scores(zcols, g, tb, row_keep):
    lo = g * RGW
    cq, sq, ck, sk = tb["rot"][0], tb["rot"][1], tb["rot"][2], tb["rot"][3]
    q1 = zcols(OFF_QR + lo, OFF_QR + lo + LANES).astype(F32)
    q2 = zcols(OFF_QR + lo + LANES, OFF_QR + lo + RGW).astype(F32)
    k1 = zcols(OFF_KR + lo, OFF_KR + lo + LANES).astype(F32)
    k2 = zcols(OFF_KR + lo + LANES, OFF_KR + lo + RGW).astype(F32)
    if row_keep is not None:
        k1, k2 = k1 * row_keep, k2 * row_keep
    q_rot = jnp.concatenate([q1 * cq - q2 * sq, q2 * cq + q1 * sq], axis=-1)
    k_rot = jnp.concatenate([k1 * ck - k2 * sk, k2 * ck + k1 * sk], axis=-1)
    qb, kb = q_rot.astype(BF16), k_rot.astype(BF16)
    kdb = (k_rot * tb["dec"][2 * g]).astype(BF16)
    vb = zcols(OFF_VR + lo, OFF_VR + lo + RGW)
    k_bd = jnp.concatenate([kb] * RG, axis=0) * tb["mask_k"][...]
    scores = lax.dot_general(qb, k_bd, (((1,), (1,)), ((), ())), preferred_element_type=F32)
    w_full = lax.dot_general(kdb, vb, (((0,), (0,)), ((), ())), preferred_element_type=F32)
    return qb, vb, scores, w_full


def _ret_outputs(g, qb, vb, scores, w_full, state_ref, tb):
    mask_s = tb["mask_s"][...]
    v_bd = jnp.concatenate([vb] * RG, axis=0) * tb["mask_v"][...]
    intra = jnp.dot((scores * tb["dtab"][g]).astype(BF16), v_bd, preferred_element_type=F32)
    c1, c2 = state_ref[g, 0:HALF, :], state_ref[g, HALF:2 * HALF, :]
    mask_sb = mask_s.astype(BF16)
    s_bd = jnp.concatenate([jnp.concatenate([c.astype(BF16)] * RG, axis=0) * mask_sb for c in (c1, c2)], axis=0)
    cross = jnp.dot(qb, s_bd, preferred_element_type=F32)
    w = []
    for half in range(2):
        wm = w_full[half * LANES:(half + 1) * LANES] * mask_s
        w.append(wm[0:HALF] + wm[HALF:2 * HALF] + wm[2 * HALF:3 * HALF] + wm[3 * HALF:4 * HALF])
    cos_s, sin_s = tb["stab"][N_RG], tb["stab"][N_RG + 1]
    a1 = tb["stab"][g] * c1 + w[0]
    a2 = tb["stab"][g] * c2 + w[1]
    state_ref[g, 0:HALF, :] = a1 * cos_s - a2 * sin_s
    state_ref[g, HALF:2 * HALF, :] = a2 * cos_s + a1 * sin_s
    return intra, cross


def _ret_square_sums(outs, tb):
    parts = []
    for o in outs:
        sq = o * o
        hi = sq.astype(BF16)
        parts += [hi, (sq - hi.astype(F32)).astype(BF16)]
    return jnp.dot(jnp.concatenate(parts, axis=0), tb["seg"][...], preferred_element_type=F32)


def _mixer_stages(zcols, kwin, vwin, valid, fill_ref, state_ref, tb, row_keep, emit_attn, emit_ret):
    attn_scores = _attn_scores(zcols(OFF_QA, OFF_QA + ATTN_WIDTH), kwin())
    ret = [_ret_scores(zcols, g, tb, row_keep) for g in range(N_RG)]
    t = ret[0][0].shape[0]
    yield
    attn_res = _attn_values(attn_scores, vwin(), valid, fill_ref, tb)
    pairs = [_ret_outputs(g, *ret[g], state_ref, tb) for g in range(N_RG)]
    yield
    emit_attn(_attn_normalise(attn_res, t).astype(BF16))
    outs = [intra + cross * tb["dec"][2 * g + 1] for g, (intra, cross) in enumerate(pairs)]
    ssq = _ret_square_sums(outs, tb)
    res = []
    for g, o in enumerate(outs):
        gate = zcols(OFF_GR + g * RGW, OFF_GR + (g + 1) * RGW).astype(F32)
        ss = ssq[2 * g * t:(2 * g + 1) * t] + ssq[(2 * g + 1) * t:(2 * g + 2) * t]
        res.append(o * lax.rsqrt(ss * (1.0 / RET_DV) + EPS) * _silu(gate))
    emit_ret(jnp.concatenate(res, axis=-1).astype(BF16))
    yield


N_STAGES = 3
FIRST_EMIT_STAGE = 2

def _stage_order(n_blocks):
    order = []
    for step in range(n_blocks + N_STAGES - 1):
        order += [step - stage for stage in reversed(range(N_STAGES)) if 0 <= step - stage < n_blocks]
    return order


def _run_blocks(blocks):
    for b in _stage_order(len(blocks)):
        next(blocks[b])


def _state_to_heads(state_ref, out_ref):
    for g in range(N_RG):
        for hh in range(RG):
            out_ref[0, g * RG + hh] = state_ref[g, :, hh * RET_DV:(hh + 1) * RET_DV]


def _prompt_kernel(sinks_ref, z_ref, zmeta_ref, x_ref, *rest, names, tile_chunks, n_g, n_tiles, lead, ff_chunk):
    rest = list(rest)
    take = lambda n: [rest.pop(0) for _ in range(n)]
    p = dict(zip(names, take(len(names))))
    tb = dict(zip(_TABLE_ORDER, take(len(_TABLE_ORDER))))
    o_ref, mixmeta_ref, klast_ref, vlast_ref, sfin_ref = take(5)
    mix_ref, h_ref, xn_ref, kctx_ref, vctx_ref, fill_ref, state_ref = take(7)
    assert not rest
    s = pl.program_id(0)
    g = lax.rem(jnp.minimum(s, n_tiles - 1), n_g)
    ctx = WINDOW_CHUNKS * CHUNK
    rows = tile_chunks * CHUNK
    s_keys = ctx + CHUNK
    col = lax.broadcasted_iota(jnp.int32, (1, KEY_PAD), 1)

    def put_keys(dst, z, n):
        kctx_ref[dst:dst + n, :] = z[:, OFF_KA:OFF_KA + KV_WIDTH]
        vctx_ref[dst:dst + n, :] = z[:, OFF_VA:OFF_VA + KV_WIDTH]

    def block(z_src, out_ref, r0, first_row, row_keep=None):
        rows_c = slice(r0, r0 + CHUNK)
        win = slice(r0, r0 + KEY_PAD)
        valid = (col < s_keys) & (col + (first_row - ctx) >= lead)

        def emit(lo):
            def store(v):
                out_ref[rows_c, lo:lo + v.shape[1]] = v
            return store

        return _mixer_stages(lambda lo, hi: z_src[rows_c, lo:hi], lambda: kctx_ref[win, :], lambda: vctx_ref[win, :],
                             valid, fill_ref, state_ref, tb, row_keep, emit(0), emit(ATTN_WIDTH))

    @pl.when(s == 0)
    def _():
        mix_ref[...] = jnp.zeros(mix_ref.shape, BF16)

    @pl.when(g == 0)
    def _():
        for r in (kctx_ref, vctx_ref):
            r[...] = jnp.zeros(r.shape, BF16)
        state_ref[...] = jnp.zeros(state_ref.shape, F32)
        _fill_table(sinks_ref, fill_ref, CHUNK, s_keys)
        zm = zmeta_ref[...]
        put_keys(ctx, zm, CHUNK)
        row_keep = (lax.broadcasted_iota(jnp.int32, (CHUNK, 1), 0) >= lead).astype(F32)
        _run_blocks([block(zmeta_ref, mixmeta_ref, 0, 0, row_keep)])
        put_keys(ctx - CHUNK, zm, CHUNK)

    units = _token_units(lambda r: x_ref[r, :], mix_ref, p, o_ref, None, h_ref, xn_ref, ff_chunk)
    n_units = _n_token_units(rows, h_ref.shape[1], ff_chunk, True)
    n_mix_reads = len(_row_halves(rows))

    put_keys(ctx, z_ref[...], rows)
    tile_row = (1 + g * tile_chunks) * CHUNK
    blocks = [block(z_ref, mix_ref, c * CHUNK, tile_row + c * CHUNK) for c in range(tile_chunks)]
    order = _stage_order(len(blocks))
    first_emit = [i for i, b in enumerate(order) if b == 0][FIRST_EMIT_STAGE]
    issued = 0
    for i, b in enumerate(order):
        if i == first_emit:
            while issued < n_mix_reads:
                next(units)
                issued += 1
        next(blocks[b])
        target = max(issued, -(-n_units * (i + 1) // len(order)))
        for _ in range(target - issued):
            next(units)
        issued = target
    assert next(units, None) is None

    for r in (kctx_ref, vctx_ref):
        r[0:ctx, :] = r[rows:rows + ctx, :]

    @pl.when((g == n_g - 1) & (s < n_tiles))
    def _():
        klast_ref[0] = kctx_ref[0:ctx, :].astype(F32)
        vlast_ref[0] = vctx_ref[0:ctx, :].astype(F32)
        _state_to_heads(state_ref, sfin_ref)


def _prompt_step(z, x, sinks, tables, ffn, mix, batch, seq, lead, final=None):
    d = x.shape[1]
    d_ff = ffn[2].shape[0]
    n_main = batch * seq
    n_chunks = seq // CHUNK
    tile_chunks = _largest_divisor(n_chunks, MAX_TILE_CHUNKS)
    n_g = n_chunks // tile_chunks
    n_tiles = batch * n_g
    rows = tile_chunks * CHUNK
    ctx = WINDOW_CHUNKS * CHUNK
    assert rows >= ctx and n_main % CHUNK == 0 and ctx + CHUNK < KEY_PAD
    meta_block0 = n_main // CHUNK
    tabs = [tables[k] for k in _TABLE_ORDER]
    names, w_arrays, w_specs = _weight_args(d, ffn, mix, None, final)
    state_shape = (RET_HEADS, RET_DK, RET_DV)
    ctx_rows = ctx + rows + KEY_PAD - (ctx + CHUNK)
    ctx_buf = pltpu.VMEM((ctx_rows, KV_WIDTH), BF16)
    cur = lambda s: jnp.minimum(s, n_tiles - 1)
    prev = lambda s: jnp.maximum(s - 1, 0)
    seq_of = lambda s: cur(s) // n_g
    return pl.pallas_call(
        functools.partial(_prompt_kernel, names=names, tile_chunks=tile_chunks, n_g=n_g, n_tiles=n_tiles, lead=lead,
                          ff_chunk=_ffn_chunk(d_ff)),
        grid=(n_tiles + 1,),
        in_specs=[pl.BlockSpec(memory_space=pltpu.SMEM),
                  pl.BlockSpec((rows, IN_WIDTH), lambda s: (cur(s), 0)),
                  pl.BlockSpec((CHUNK, IN_WIDTH), lambda s: (meta_block0, 0)),
                  pl.BlockSpec((rows, d), lambda s: (prev(s), 0))]
                 + w_specs + [_resident(t.shape) for t in tabs],
        out_specs=[pl.BlockSpec((rows, d), lambda s: (prev(s), 0)),
                   pl.BlockSpec((CHUNK, MIX_WIDTH), lambda s: (0, 0)),
                   pl.BlockSpec((1, ctx, KV_WIDTH), lambda s: (seq_of(s), 0, 0)),
                   pl.BlockSpec((1, ctx, KV_WIDTH), lambda s: (seq_of(s), 0, 0)),
                   pl.BlockSpec((1,) + state_shape, lambda s: (seq_of(s), 0, 0, 0))],
        out_shape=[jax.ShapeDtypeStruct((n_main, d), F32),
                   jax.ShapeDtypeStruct((CHUNK, MIX_WIDTH), BF16),
                   jax.ShapeDtypeStruct((batch, ctx, KV_WIDTH), F32),
                   jax.ShapeDtypeStruct((batch, ctx, KV_WIDTH), F32),
                   jax.ShapeDtypeStruct((batch,) + state_shape, F32)],
        scratch_shapes=[pltpu.VMEM((rows, MIX_WIDTH), BF16), pltpu.VMEM((rows, d_ff), BF16),
                        pltpu.VMEM((rows, d), BF16), ctx_buf, ctx_buf,
                        pltpu.VMEM((ATTN_KV_HEADS, GROUP * CHUNK, KEY_PAD), F32),
                        pltpu.VMEM((N_RG, RET_DK, RGW), F32)],
        compiler_params=_compiler_params(1),
        name="prompt_step",
    )(sinks, z, z, x, *w_arrays, *tabs)


def _sample_mixer_kernel(sinks_ref, z_ref, ck_ref, cv_ref, s0_ref, *rest, cache_rows):
    tb = dict(zip(_TABLE_ORDER, rest[:len(_TABLE_ORDER)]))
    mix_ref, knew_ref, vnew_ref, snew_ref, fill_ref, state_ref = rest[len(_TABLE_ORDER):]
    z = z_ref[...]
    t = z.shape[0]
    s_keys = cache_rows + t
    pad = jnp.zeros((KEY_PAD - s_keys, KV_WIDTH), F32)
    k_all = jnp.concatenate([ck_ref[0], z[:, OFF_KA:OFF_KA + KV_WIDTH].astype(F32)], axis=0)
    v_all = jnp.concatenate([cv_ref[0], z[:, OFF_VA:OFF_VA + KV_WIDTH].astype(F32)], axis=0)
    windows = [jnp.concatenate([a, pad], axis=0).astype(BF16) for a in (k_all, v_all)]
    _fill_table(sinks_ref, fill_ref, t, s_keys)
    for g in range(N_RG):
        for hh in range(RG):
            state_ref[g, :, hh * RET_DV:(hh + 1) * RET_DV] = s0_ref[0, g * RG + hh]
    col = lax.broadcasted_iota(jnp.int32, (1, KEY_PAD), 1)

    def emit(lo):
        def store(v):
            mix_ref[:, lo:lo + v.shape[1]] = v
        return store

    _run_blocks([_mixer_stages(lambda lo, hi: z_ref[:, lo:hi], lambda: windows[0], lambda: windows[1], col < s_keys,
                               fill_ref, state_ref, tb, None, emit(0), emit(ATTN_WIDTH))])
    knew_ref[0] = k_all[s_keys - cache_rows:]
    vnew_ref[0] = v_all[s_keys - cache_rows:]
    _state_to_heads(state_ref, snew_ref)


def _sample_mixer(z, sinks, tables, cache_k, cache_v, state, batch, t, row_offset):
    cache_rows = cache_k.shape[1]
    assert row_offset % t == 0 and cache_rows + t < KEY_PAD
    first_block = row_offset // t
    tabs = [tables[k] for k in _TABLE_ORDER]
    state_shape = (RET_HEADS, RET_DK, RET_DV)
    cache_spec = pl.BlockSpec((1, cache_rows, KV_WIDTH), lambda b: (b, 0, 0))
    state_spec = pl.BlockSpec((1,) + state_shape, lambda b: (b, 0, 0, 0))
    return pl.pallas_call(
        functools.partial(_sample_mixer_kernel, cache_rows=cache_rows),
        grid=(batch,),
        in_specs=[pl.BlockSpec(memory_space=pltpu.SMEM),
                  pl.BlockSpec((t, IN_WIDTH), lambda b: (first_block + b, 0)),
                  cache_spec, cache_spec, state_spec] + [_resident(x.shape) for x in tabs],
        out_specs=[pl.BlockSpec((t, MIX_WIDTH), lambda b: (b, 0)), cache_spec, cache_spec, state_spec],
        out_shape=[jax.ShapeDtypeStruct((batch * t, MIX_WIDTH), BF16),
                   jax.ShapeDtypeStruct(cache_k.shape, F32),
                   jax.ShapeDtypeStruct(cache_v.shape, F32),
                   jax.ShapeDtypeStruct(state.shape, F32)],
        scratch_shapes=[pltpu.VMEM((ATTN_KV_HEADS, GROUP * t, KEY_PAD), F32),
                        pltpu.VMEM((N_RG, RET_DK, RGW), F32)],
        compiler_params=_compiler_params(1),
        name="sample_mixer",
    )(sinks, z, cache_k, cache_v, state, *tabs)


def kernel(x_prompt, x_sample, cache_swa_k, cache_swa_v, state_ret, meta_tokens, w_in, w_out, attn_sinks,
           ffn1_w_in, ffn1_w_out, ffn2_w_in, ffn2_w_out, norm_ffn1_pre, norm_ffn1_post, norm_mix_pre,
           norm_mix_post, norm_ffn2_pre, norm_ffn2_post, final_norm):
    batch, seq, d = x_prompt.shape
    dec_batch, dec_seq, _ = x_sample.shape
    depth = w_in.shape[0]
    cache_rows = cache_swa_k.shape[2]
    lead = CHUNK - N_META
    n_main, n_meta, n_sample = batch * seq, CHUNK, dec_batch * dec_seq
    tm = _row_tile(n_main)
    n_tail = -(-(n_meta + n_sample) // tm) * tm
    n_pad = n_tail - n_meta - n_sample
    assert w_in.shape[2] == IN_WIDTH and w_out.shape[1] == MIX_WIDTH
    assert seq % CHUNK == 0 and cache_rows == WINDOW_CHUNKS * CHUNK and seq >= cache_rows

    x_tail = jnp.concatenate([jnp.zeros((lead, d), x_prompt.dtype), meta_tokens.astype(x_prompt.dtype),
                              x_sample.reshape(n_sample, d), jnp.zeros((n_pad, d), x_prompt.dtype)], axis=0)
    x = (x_prompt.reshape(n_main, d), x_tail)

    tables_p = _mixer_tables(CHUNK, WINDOW_CHUNKS * CHUNK + CHUNK)
    tables_s = _mixer_tables(dec_seq, cache_rows + dec_seq)
    cache_k = cache_swa_k.reshape(depth, dec_batch, cache_rows, KV_WIDTH)
    cache_v = cache_swa_v.reshape(depth, dec_batch, cache_rows, KV_WIDTH)

    w_proj, w_mix = _projection_weight(w_in), w_out.astype(BF16)
    w1_in, w1_out = ffn1_w_in.astype(BF16), ffn1_w_out.astype(BF16)
    w2_in, w2_out = ffn2_w_in.astype(BF16), ffn2_w_out.astype(BF16)

    pk, pv, ps, sk, sv, ss = [], [], [], [], [], []
    for l in range(depth):
        ffn1 = (norm_ffn1_pre[l], _Layer(w1_in, l), _Layer(w1_out, l), norm_ffn1_post[l])
        ffn2 = (norm_ffn2_pre[l], _Layer(w2_in, l), _Layer(w2_out, l), norm_ffn2_post[l])
        mix_w = (_Layer(w_mix, l), norm_mix_post[l])
        final = final_norm if l == depth - 1 else None
        x1, z = _token_step(x, ffn1, tm, n_out=n_main + n_tail, proj=(norm_mix_pre[l], _Layer(w_proj, l)))
        x_main, mix_meta, k_p, v_p, s_p = _prompt_step(z, x1, attn_sinks[l], tables_p, ffn2, mix_w, batch, seq, lead,
                                                       final)
        mix_s, k_s, v_s, s_s = _sample_mixer(z, attn_sinks[l], tables_s, cache_k[l], cache_v[l], state_ret[l],
                                             dec_batch, dec_seq, n_main + n_meta)
        mix_tail = jnp.concatenate([mix_meta, mix_s, jnp.zeros((n_pad, MIX_WIDTH), BF16)], axis=0)
        x_tail = _token_step(x1, ffn2, tm, n_out=n_tail, first_tile=n_main // tm, mix=(mix_tail,) + mix_w,
                             final=final)
        x = (x_main, x_tail)
        pk.append(k_p); pv.append(v_p); ps.append(s_p)
        sk.append(k_s); sv.append(v_s); ss.append(s_s)

    y_main, y_tail = x
    y_prompt = y_main.reshape(batch, seq, d)
    y_sample = y_tail[n_meta:n_meta + n_sample].reshape(dec_batch, dec_seq, d)
    kv_shape_p = (depth, batch, cache_rows, ATTN_KV_HEADS, HEAD_DIM)
    kv_shape_s = (depth, dec_batch, cache_rows, ATTN_KV_HEADS, HEAD_DIM)
    return (y_prompt, y_sample,
            jnp.stack(pk).reshape(kv_shape_p), jnp.stack(pv).reshape(kv_shape_p), jnp.stack(ps),
            jnp.stack(sk).reshape(kv_shape_s), jnp.stack(sv).reshape(kv_shape_s), jnp.stack(ss))
```

```python
import functools
import math
from typing import NamedTuple

import jax
import jax.numpy as jnp
import numpy as np
from jax import lax
from jax.experimental import pallas as pl
from jax.experimental.pallas import tpu as pltpu

CHUNK = 64
N_META = 16
WINDOW_CHUNKS = 2
ATTN_HEADS = 8
ATTN_KV_HEADS = 2
HEAD_DIM = 64
GROUP = ATTN_HEADS // ATTN_KV_HEADS
ATTN_SCALE = HEAD_DIM ** -0.5
LOG2_E = math.log2(math.e)
RET_HEADS = 8
RET_DK = 64
RET_DV = 64
ATTN_WIDTH = ATTN_HEADS * HEAD_DIM
KV_WIDTH = ATTN_KV_HEADS * HEAD_DIM
RET_WIDTH = RET_HEADS * RET_DK
ROPE_BASE = 10000.0
EPS = 1e-6
NEG = -1e30

OFF_QA = 0
OFF_KA = OFF_QA + ATTN_WIDTH
OFF_VA = OFF_KA + KV_WIDTH
OFF_QR = OFF_VA + KV_WIDTH
OFF_KR = OFF_QR + RET_WIDTH
OFF_VR = OFF_KR + RET_WIDTH
OFF_GR = OFF_VR + RET_WIDTH
IN_WIDTH = OFF_GR + RET_WIDTH
MIX_WIDTH = ATTN_WIDTH + RET_WIDTH

LANES = 128
HALF = RET_DK // 2
RG = 4
RGW = RG * RET_DK
N_RG = RET_HEADS // RG
KEY_PAD = 256
V7X_VMEM_LIMIT_BYTES = 56 * 1024 * 1024
FF_CHUNK = 256
MAX_ROW_TILE = 512
ROW_SPLIT = 2
MAX_TILE_CHUNKS = 8

BF16 = jnp.bfloat16
F32 = jnp.float32

assert RGW == 2 * LANES and KV_WIDTH == LANES and GROUP == 4 and ATTN_KV_HEADS == 2


def _row_tile(n, cap=MAX_ROW_TILE):
    t = cap
    while t > 8 and n % t:
        t //= 2
    assert n % t == 0, (n, t)
    return t


def _largest_divisor(n, cap):
    return max(d for d in range(1, cap + 1) if n % d == 0)


def _rms(x):
    return x * lax.rsqrt(jnp.mean(x * x, axis=-1, keepdims=True) + EPS)


def _silu(x):
    return x / (1.0 + jnp.exp(-x))


def _compiler_params(n_axes):
    return pltpu.CompilerParams(dimension_semantics=("arbitrary",) * n_axes,
                                vmem_limit_bytes=V7X_VMEM_LIMIT_BYTES)


def _resident(shape):
    return pl.BlockSpec(shape, lambda *_: (0,) * len(shape), pipeline_mode=pl.Buffered(1))


class _Layer(NamedTuple):
    stack: jax.Array
    index: int

    @property
    def shape(self):
        return self.stack.shape[1:]

    @property
    def spec(self):
        index = self.index
        return pl.BlockSpec((None,) + self.shape, lambda *_: (index,) + (0,) * len(self.shape),
                            pipeline_mode=pl.Buffered(1))


def _row_halves(tm):
    n_split = ROW_SPLIT if tm % (ROW_SPLIT * 16) == 0 else 1
    return [slice(r * tm // n_split, (r + 1) * tm // n_split) for r in range(n_split)]


def _token_units(x_of, mix_ref, p, o_ref, z_ref, h_ref, xn_ref, ff_chunk):
    tm, d_ff = h_ref.shape
    halves = _row_halves(tm)
    if mix_ref is not None:
        for rows in halves:
            y = jnp.dot(mix_ref[rows, :], p["wmix"][...], preferred_element_type=F32)
            o_ref[rows, :] = x_of(rows) + _rms(y) * p["gmix"][...]
            yield
        resid = lambda rows: o_ref[rows, :]
    else:
        resid = x_of
    xn = jnp.concatenate([(_rms(resid(rows)) * p["gpre"][...]).astype(BF16) for rows in halves], axis=0)
    if xn_ref is not None:
        xn_ref[...] = xn
    for j in range(d_ff // ff_chunk):
        lo = j * ff_chunk
        if xn_ref is not None:
            xn = xn_ref[...]
        gate = jnp.dot(xn, p["win"][:, lo:lo + ff_chunk], preferred_element_type=F32)
        yield
        up = jnp.dot(xn, p["win"][:, d_ff + lo:d_ff + lo + ff_chunk], preferred_element_type=F32)
        h_ref[:, lo:lo + ff_chunk] = (_silu(gate) * up).astype(BF16)
        yield
    ys = []
    for rows in halves:
        ys.append(jnp.dot(h_ref[rows, :], p["wout"][...], preferred_element_type=F32))
        yield
    for rows, y in zip(halves, ys):
        xr = resid(rows) + 0.5 * (_rms(y) * p["gpost"][...])
        if z_ref is not None:
            xn = (_rms(xr) * p["gproj"][...]).astype(BF16)
            z_ref[rows, :] = jnp.dot(xn, p["wproj"][...], preferred_element_type=F32).astype(BF16)
        o_ref[rows, :] = _rms(xr) * p["gfin"][...] if "gfin" in p else xr
        yield


def _n_token_units(tm, d_ff, ff_chunk, has_mix):
    return (3 if has_mix else 2) * len(_row_halves(tm)) + 2 * (d_ff // ff_chunk)


def _token_kernel(*refs, names, ff_chunk, n_main_tiles, split_in, has_mix, has_proj):
    refs = list(refs)
    take = lambda n: [refs.pop(0) for _ in range(n)]
    if split_in:
        xm_ref, xt_ref = take(2)
        in_main = pl.program_id(0) < n_main_tiles
        x_of = lambda rows: jnp.where(in_main, xm_ref[rows, :], xt_ref[rows, :])
    else:
        (x_ref,) = take(1)
        x_of = lambda rows: x_ref[rows, :]
    mix_ref = take(1)[0] if has_mix else None
    p = dict(zip(names, take(len(names))))
    (o_ref,) = take(1)
    z_ref = take(1)[0] if has_proj else None
    (h_ref,) = take(1)
    assert not refs
    for _ in _token_units(x_of, mix_ref, p, o_ref, z_ref, h_ref, None, ff_chunk):
        pass


def _ffn_chunk(d_ff):
    return FF_CHUNK if d_ff % FF_CHUNK == 0 else d_ff


def _weight_args(d, ffn, mix=None, proj=None, final=None):
    g_pre, w_in, w_out, g_post = ffn
    gain = lambda g: g if isinstance(g, _Layer) else g.reshape(1, d)
    named = []
    if mix is not None:
        named += [("wmix", mix[0]), ("gmix", gain(mix[1]))]
    named += [("gpre", gain(g_pre)), ("win", w_in), ("wout", w_out), ("gpost", gain(g_post))]
    if proj is not None:
        named += [("gproj", gain(proj[0])), ("wproj", proj[1])]
    if final is not None:
        named.append(("gfin", gain(final)))
    names, arrays = zip(*named)
    specs = [a.spec if isinstance(a, _Layer) else _resident(a.shape) for a in arrays]
    return names, [a.stack if isinstance(a, _Layer) else a for a in arrays], specs


def _token_step(x, ffn, tm, *, n_out, first_tile=0, mix=None, proj=None, final=None):
    split_in = isinstance(x, tuple)
    d = x[0].shape[1] if split_in else x.shape[1]
    d_ff = ffn[2].shape[0]
    n_main_tiles = x[0].shape[0] // tm if split_in else 0
    args, in_specs = [], []
    if split_in:
        assert x[0].shape[0] % tm == 0 and x[1].shape[0] % tm == 0
        args += list(x)
        in_specs += [pl.BlockSpec((tm, d), lambda i: (jnp.minimum(i, n_main_tiles - 1), 0)),
                     pl.BlockSpec((tm, d), lambda i: (jnp.maximum(i - n_main_tiles, 0), 0))]
    else:
        args.append(x)
        in_specs.append(pl.BlockSpec((tm, d), lambda i: (first_tile + i, 0)))
    if mix is not None:
        args.append(mix[0])
        in_specs.append(pl.BlockSpec((tm, MIX_WIDTH), lambda i: (i, 0)))
    names, w_arrays, w_specs = _weight_args(d, ffn, mix[1:] if mix is not None else None, proj, final)
    out_shape = [jax.ShapeDtypeStruct((n_out, d), F32)]
    out_specs = [pl.BlockSpec((tm, d), lambda i: (i, 0))]
    if proj is not None:
        out_shape.append(jax.ShapeDtypeStruct((n_out, IN_WIDTH), BF16))
        out_specs.append(pl.BlockSpec((tm, IN_WIDTH), lambda i: (i, 0)))
    outs = pl.pallas_call(
        functools.partial(_token_kernel, names=names, ff_chunk=_ffn_chunk(d_ff), n_main_tiles=n_main_tiles,
                          split_in=split_in, has_mix=mix is not None, has_proj=proj is not None),
        grid=(n_out // tm,),
        in_specs=in_specs + w_specs,
        out_specs=out_specs,
        out_shape=out_shape,
        scratch_shapes=[pltpu.VMEM((tm, d_ff), BF16)],
        compiler_params=_compiler_params(1),
        name="token_step",
    )(*args, *w_arrays)
    return outs[0] if len(outs) == 1 else tuple(outs)


def _permute_ret_columns(w):
    lead = w.shape[:-1]
    return jnp.swapaxes(w.reshape(lead + (N_RG, RG, 2, HALF)), -3, -2).reshape(lead + (RET_WIDTH,))


def _projection_weight(w):
    parts = [w[..., :OFF_QR], _permute_ret_columns(w[..., OFF_QR:OFF_KR]),
             _permute_ret_columns(w[..., OFF_KR:OFF_VR]), w[..., OFF_VR:]]
    return jnp.concatenate(parts, axis=-1).astype(BF16)


def _mixer_tables(t, s_keys):
    f32 = np.float32
    log_g = np.log1p(-np.exp2(-5.0 - np.arange(RET_HEADS, dtype=f32))).astype(f32)
    freqs = (ROPE_BASE ** (-np.arange(HALF, dtype=f32) / HALF)).astype(f32)
    pos = np.arange(t, dtype=f32)
    ang = pos[:, None] * freqs[None, :]
    cos, sin = np.tile(np.cos(ang), (1, RG)), np.tile(np.sin(ang), (1, RG))
    qs = f32(RET_DK ** -0.5)
    rot = jnp.asarray(np.stack([cos * qs, sin * qs, cos, sin]), F32)

    head = lambda g: slice(g * RG, (g + 1) * RG)
    diff = pos[:, None] - pos[None, :]
    intra = np.where(diff >= 0, np.exp(log_g[:, None, None] * np.maximum(diff, 0)), 0).astype(f32)
    kv_dec = np.exp((t - pos)[:, None] * log_g[None, :])
    q_dec = np.exp(pos[:, None] * log_g[None, :])
    dec = []
    for g in range(N_RG):
        kd = np.tile(np.repeat(kv_dec[:, head(g)], HALF, axis=1), (1, 2))
        qd = np.repeat(q_dec[:, head(g)], RET_DV, axis=1)
        dec += [kd, qd]
    dec = jnp.asarray(np.stack(dec), F32)
    dtab = jnp.asarray(np.stack([np.concatenate(list(intra[head(g)]), axis=1) for g in range(N_RG)]), F32)

    g_t = np.exp(t * log_g)
    shift = -t * freqs
    stab = [np.broadcast_to(np.repeat(g_t[head(g)], RET_DV)[None, :], (HALF, RGW)) for g in range(N_RG)]
    stab += [np.broadcast_to(np.cos(shift)[:, None], (HALF, RGW)),
             np.broadcast_to(np.sin(shift)[:, None], (HALF, RGW))]
    stab = jnp.asarray(np.stack(stab), F32)

    r4 = np.arange(RG * t)[:, None] // t
    lane = np.arange(RGW)[None, :]
    mask_k = jnp.asarray(r4 == (lane % LANES) // HALF, BF16)
    mask_v = jnp.asarray(r4 == lane // RET_DV, BF16)
    rs = np.arange(LANES)[:, None] // HALF
    mask_s = jnp.asarray(rs == lane // RET_DV, F32)
    seg = jnp.asarray(np.arange(RGW)[:, None] // RET_DV == lane // RET_DV, BF16)

    rowk = np.arange(KEY_PAD)[:, None] < s_keys
    lane_v = np.arange(LANES)[None, :] // HEAD_DIM
    v_keep = jnp.stack([jnp.asarray(rowk & (lane_v == p), BF16) for p in range(2)])
    v_ones = jnp.stack([jnp.asarray(np.broadcast_to(lane_v != p, (KEY_PAD, LANES)), BF16) for p in range(2)])
    return dict(rot=rot, dec=dec, dtab=dtab, stab=stab, mask_k=mask_k, mask_v=mask_v, mask_s=mask_s, seg=seg,
                v_keep=v_keep, v_ones=v_ones)


_TABLE_ORDER = ("rot", "dec", "dtab", "stab", "mask_k", "mask_v", "mask_s", "seg", "v_keep", "v_ones")


def _fill_table(sink_of, fill_ref, t, s_keys):
    col = lax.broadcasted_iota(jnp.int32, (t, KEY_PAD), 1)
    for kv in range(ATTN_KV_HEADS):
        for j in range(GROUP):
            fill_ref[kv, j * t:(j + 1) * t, :] = jnp.where(col == s_keys, sink_of(kv * GROUP + j) * LOG2_E, NEG)


def _attn_scores(q, kwin):
    t = q.shape[0]
    lower = lax.broadcasted_iota(jnp.int32, (t, LANES), 1) < HEAD_DIM
    out = []
    for kv in range(ATTN_KV_HEADS):
        keep = jnp.where(lower if kv == 0 else ~lower, ATTN_SCALE * LOG2_E, 0.0)
        parts = []
        for j in range(2):
            blk = q[:, (2 * kv + j) * LANES:(2 * kv + j + 1) * LANES].astype(F32)
            rolled = pltpu.roll(blk, HEAD_DIM, axis=1)
            lo_head, hi_head = (blk, rolled) if kv == 0 else (rolled, blk)
            parts += [lo_head * keep, hi_head * keep]
        qs = jnp.concatenate(parts, axis=0).astype(BF16)
        out.append(lax.dot_general(qs, kwin, (((1,), (1,)), ((), ())), preferred_element_type=F32))
    return out


def _attn_values(scores, vwin, valid, fill_ref, tb):
    out = []
    for kv, s in enumerate(scores):
        s = jnp.where(valid, s, fill_ref[kv])
        p = jnp.exp2(s - jnp.max(s, axis=-1, keepdims=True))
        vext = vwin * tb["v_keep"][kv] + tb["v_ones"][kv]
        out.append(jnp.dot(p.astype(BF16), vext, preferred_element_type=F32))
    return out


def _attn_normalise(results, t):
    lower = lax.broadcasted_iota(jnp.int32, (t, LANES), 1) < HEAD_DIM
    cols = []
    for kv, res in enumerate(results):
        swapped = pltpu.roll(res, HEAD_DIM, axis=1)
        for j in range(2):
            lo_rows, hi_rows = slice(2 * j * t, (2 * j + 1) * t), slice((2 * j + 1) * t, (2 * j + 2) * t)
            if kv == 0:
                lo, hi = res[lo_rows] / swapped[lo_rows], swapped[hi_rows] / res[hi_rows]
            else:
                lo, hi = swapped[lo_rows] / res[lo_rows], res[hi_rows] / swapped[hi_rows]
            cols.append(jnp.where(lower, lo, hi))
    return jnp.concatenate(cols, axis=-1)


def _ret_scores(zcols, g, tb, row_keep):
    lo = g * RGW
    cq, sq, ck, sk = tb["rot"][0], tb["rot"][1], tb["rot"][2], tb["rot"][3]
    q1 = zcols(OFF_QR + lo, OFF_QR + lo + LANES).astype(F32)
    q2 = zcols(OFF_QR + lo + LANES, OFF_QR + lo + RGW).astype(F32)
    k1 = zcols(OFF_KR + lo, OFF_KR + lo + LANES).astype(F32)
    k2 = zcols(OFF_KR + lo + LANES, OFF_KR + lo + RGW).astype(F32)
    if row_keep is not None:
        k1, k2 = k1 * row_keep, k2 * row_keep
    q_rot = jnp.concatenate([q1 * cq - q2 * sq, q2 * cq + q1 * sq], axis=-1)
    k_rot = jnp.concatenate([k1 * ck - k2 * sk, k2 * ck + k1 * sk], axis=-1)
    qb, kb = q_rot.astype(BF16), k_rot.astype(BF16)
    kdb = (k_rot * tb["dec"][2 * g]).astype(BF16)
    vb = zcols(OFF_VR + lo, OFF_VR + lo + RGW)
    k_bd = jnp.concatenate([kb] * RG, axis=0) * tb["mask_k"][...]
    scores = lax.dot_general(qb, k_bd, (((1,), (1,)), ((), ())), preferred_element_type=F32)
    w_full = lax.dot_general(kdb, vb, (((0,), (0,)), ((), ())), preferred_element_type=F32)
    return qb, vb, scores, w_full


def _ret_outputs(g, qb, vb, scores, w_full, state_ref, tb):
    mask_s = tb["mask_s"][...]
    v_bd = jnp.concatenate([vb] * RG, axis=0) * tb["mask_v"][...]
    intra = jnp.dot((scores * tb["dtab"][g]).astype(BF16), v_bd, preferred_element_type=F32)
    c1, c2 = state_ref[g, 0:HALF, :], state_ref[g, HALF:2 * HALF, :]
    mask_sb = mask_s.astype(BF16)
    s_bd = jnp.concatenate([jnp.concatenate([c.astype(BF16)] * RG, axis=0) * mask_sb for c in (c1, c2)], axis=0)
    cross = jnp.dot(qb, s_bd, preferred_element_type=F32)
    w = []
    for half in range(2):
        wm = w_full[half * LANES:(half + 1) * LANES] * mask_s
        w.append(wm[0:HALF] + wm[HALF:2 * HALF] + wm[2 * HALF:3 * HALF] + wm[3 * HALF:4 * HALF])
    cos_s, sin_s = tb["stab"][N_RG], tb["stab"][N_RG + 1]
    a1 = tb["stab"][g] * c1 + w[0]
    a2 = tb["stab"][g] * c2 + w[1]
    state_ref[g, 0:HALF, :] = a1 * cos_s - a2 * sin_s
    state_ref[g, HALF:2 * HALF, :] = a2 * cos_s + a1 * sin_s
    return intra, cross


def _ret_square_sums(outs, tb):
    parts = []
    for o in outs:
        sq = o * o
        hi = sq.astype(BF16)
        parts += [hi, (sq - hi.astype(F32)).astype(BF16)]
    return jnp.dot(jnp.concatenate(parts, axis=0), tb["seg"][...], preferred_element_type=F32)


def _mixer_stages(zcols, kwin, vwin, valid, fill_ref, state_ref, tb, row_keep, emit_attn, emit_ret):
    attn_scores = _attn_scores(zcols(OFF_QA, OFF_QA + ATTN_WIDTH), kwin())
    ret = [_ret_scores(zcols, g, tb, row_keep) for g in range(N_RG)]
    t = ret[0][0].shape[0]
    yield
    attn_res = _attn_values(attn_scores, vwin(), valid, fill_ref, tb)
    pairs = [_ret_outputs(g, *ret[g], state_ref, tb) for g in range(N_RG)]
    yield
    emit_attn(_attn_normalise(attn_res, t).astype(BF16))
    outs = [intra + cross * tb["dec"][2 * g + 1] for g, (intra, cross) in enumerate(pairs)]
    ssq = _ret_square_sums(outs, tb)
    res = []
    for g, o in enumerate(outs):
        gate = zcols(OFF_GR + g * RGW, OFF_GR + (g + 1) * RGW).astype(F32)
        ss = ssq[2 * g * t:(2 * g + 1) * t] + ssq[(2 * g + 1) * t:(2 * g + 2) * t]
        res.append(o * lax.rsqrt(ss * (1.0 / RET_DV) + EPS) * _silu(gate))
    emit_ret(jnp.concatenate(res, axis=-1).astype(BF16))
    yield


N_STAGES = 3
FIRST_EMIT_STAGE = 2

def _stage_order(n_blocks):
    order = []
    for step in range(n_blocks + N_STAGES - 1):
        order += [step - stage for stage in reversed(range(N_STAGES)) if 0 <= step - stage < n_blocks]
    return order


def _run_blocks(blocks):
    for b in _stage_order(len(blocks)):
        next(blocks[b])


def _state_to_heads(state_ref, out_ref):
    for g in range(N_RG):
        for hh in range(RG):
            out_ref[0, g * RG + hh] = state_ref[g, :, hh * RET_DV:(hh + 1) * RET_DV]


def _prompt_kernel(sinks_ref, z_ref, zmeta_ref, x_ref, *rest, names, layer, tile_chunks, n_g, n_tiles, lead,
                   ff_chunk):
    rest = list(rest)
    take = lambda n: [rest.pop(0) for _ in range(n)]
    p = dict(zip(names, take(len(names))))
    tb = dict(zip(_TABLE_ORDER, take(len(_TABLE_ORDER))))
    o_ref, mixmeta_ref, klast_ref, vlast_ref, sfin_ref = take(5)
    mix_ref, h_ref, xn_ref, kctx_ref, vctx_ref, fill_ref, state_ref = take(7)
    assert not rest
    s = pl.program_id(0)
    g = lax.rem(jnp.minimum(s, n_tiles - 1), n_g)
    ctx = WINDOW_CHUNKS * CHUNK
    rows = tile_chunks * CHUNK
    s_keys = ctx + CHUNK
    col = lax.broadcasted_iota(jnp.int32, (1, KEY_PAD), 1)

    def put_keys(dst, z, n):
        kctx_ref[dst:dst + n, :] = z[:, OFF_KA:OFF_KA + KV_WIDTH]
        vctx_ref[dst:dst + n, :] = z[:, OFF_VA:OFF_VA + KV_WIDTH]

    def block(z_src, out_ref, r0, first_row, row_keep=None):
        rows_c = slice(r0, r0 + CHUNK)
        win = slice(r0, r0 + KEY_PAD)
        valid = (col < s_keys) & (col + (first_row - ctx) >= lead)

        def emit(lo):
            def store(v):
                out_ref[rows_c, lo:lo + v.shape[1]] = v
            return store

        return _mixer_stages(lambda lo, hi: z_src[rows_c, lo:hi], lambda: kctx_ref[win, :], lambda: vctx_ref[win, :],
                             valid, fill_ref, state_ref, tb, row_keep, emit(0), emit(ATTN_WIDTH))

    @pl.when(s == 0)
    def _():
        mix_ref[...] = jnp.zeros(mix_ref.shape, BF16)

    @pl.when(g == 0)
    def _():
        for r in (kctx_ref, vctx_ref):
            r[...] = jnp.zeros(r.shape, BF16)
        state_ref[...] = jnp.zeros(state_ref.shape, F32)
        _fill_table(lambda h: sinks_ref[layer, h], fill_ref, CHUNK, s_keys)
        zm = zmeta_ref[...]
        put_keys(ctx, zm, CHUNK)
        row_keep = (lax.broadcasted_iota(jnp.int32, (CHUNK, 1), 0) >= lead).astype(F32)
        _run_blocks([block(zmeta_ref, mixmeta_ref, 0, 0, row_keep)])
        put_keys(ctx - CHUNK, zm, CHUNK)

    units = _token_units(lambda r: x_ref[r, :], mix_ref, p, o_ref, None, h_ref, xn_ref, ff_chunk)
    n_units = _n_token_units(rows, h_ref.shape[1], ff_chunk, True)
    n_mix_reads = len(_row_halves(rows))

    put_keys(ctx, z_ref[...], rows)
    tile_row = (1 + g * tile_chunks) * CHUNK
    blocks = [block(z_ref, mix_ref, c * CHUNK, tile_row + c * CHUNK) for c in range(tile_chunks)]
    order = _stage_order(len(blocks))
    first_emit = [i for i, b in enumerate(order) if b == 0][FIRST_EMIT_STAGE]
    issued = 0
    for i, b in enumerate(order):
        if i == first_emit:
            while issued < n_mix_reads:
                next(units)
                issued += 1
        next(blocks[b])
        target = max(issued, -(-n_units * (i + 1) // len(order)))
        for _ in range(target - issued):
            next(units)
        issued = target
    assert next(units, None) is None

    for r in (kctx_ref, vctx_ref):
        r[0:ctx, :] = r[rows:rows + ctx, :]

    @pl.when((g == n_g - 1) & (s < n_tiles))
    def _():
        klast_ref[0] = kctx_ref[0:ctx, :].astype(F32)
        vlast_ref[0] = vctx_ref[0:ctx, :].astype(F32)
        _state_to_heads(state_ref, sfin_ref)


def _prompt_step(z, x, sinks, layer, tables, ffn, mix, batch, seq, lead, final=None):
    d = x.shape[1]
    d_ff = ffn[2].shape[0]
    n_main = batch * seq
    n_chunks = seq // CHUNK
    tile_chunks = _largest_divisor(n_chunks, MAX_TILE_CHUNKS)
    n_g = n_chunks // tile_chunks
    n_tiles = batch * n_g
    rows = tile_chunks * CHUNK
    ctx = WINDOW_CHUNKS * CHUNK
    assert rows >= ctx and n_main % CHUNK == 0 and ctx + CHUNK < KEY_PAD
    meta_block0 = n_main // CHUNK
    tabs = [tables[k] for k in _TABLE_ORDER]
    names, w_arrays, w_specs = _weight_args(d, ffn, mix, None, final)
    state_shape = (RET_HEADS, RET_DK, RET_DV)
    ctx_rows = ctx + rows + KEY_PAD - (ctx + CHUNK)
    ctx_buf = pltpu.VMEM((ctx_rows, KV_WIDTH), BF16)
    cur = lambda s: jnp.minimum(s, n_tiles - 1)
    prev = lambda s: jnp.maximum(s - 1, 0)
    seq_of = lambda s: cur(s) // n_g
    return pl.pallas_call(
        functools.partial(_prompt_kernel, names=names, layer=layer, tile_chunks=tile_chunks, n_g=n_g,
                          n_tiles=n_tiles, lead=lead, ff_chunk=_ffn_chunk(d_ff)),
        grid=(n_tiles + 1,),
        in_specs=[pl.BlockSpec(memory_space=pltpu.SMEM),
                  pl.BlockSpec((rows, IN_WIDTH), lambda s: (cur(s), 0)),
                  pl.BlockSpec((CHUNK, IN_WIDTH), lambda s: (meta_block0, 0)),
                  pl.BlockSpec((rows, d), lambda s: (prev(s), 0))]
                 + w_specs + [_resident(t.shape) for t in tabs],
        out_specs=[pl.BlockSpec((rows, d), lambda s: (prev(s), 0)),
                   pl.BlockSpec((CHUNK, MIX_WIDTH), lambda s: (0, 0)),
                   pl.BlockSpec((1, ctx, KV_WIDTH), lambda s: (seq_of(s), 0, 0)),
                   pl.BlockSpec((1, ctx, KV_WIDTH), lambda s: (seq_of(s), 0, 0)),
                   pl.BlockSpec((1,) + state_shape, lambda s: (seq_of(s), 0, 0, 0))],
        out_shape=[jax.ShapeDtypeStruct((n_main, d), F32),
                   jax.ShapeDtypeStruct((CHUNK, MIX_WIDTH), BF16),
                   jax.ShapeDtypeStruct((batch, ctx, KV_WIDTH), F32),
                   jax.ShapeDtypeStruct((batch, ctx, KV_WIDTH), F32),
                   jax.ShapeDtypeStruct((batch,) + state_shape, F32)],
        scratch_shapes=[pltpu.VMEM((rows, MIX_WIDTH), BF16), pltpu.VMEM((rows, d_ff), BF16),
                        pltpu.VMEM((rows, d), BF16), ctx_buf, ctx_buf,
                        pltpu.VMEM((ATTN_KV_HEADS, GROUP * CHUNK, KEY_PAD), F32),
                        pltpu.VMEM((N_RG, RET_DK, RGW), F32)],
        compiler_params=_compiler_params(1),
        name="prompt_step",
    )(sinks, z, z, x, *w_arrays, *tabs)


def _sample_mixer_kernel(sinks_ref, z_ref, ck_ref, cv_ref, s0_ref, *rest, layer, cache_rows):
    tb = dict(zip(_TABLE_ORDER, rest[:len(_TABLE_ORDER)]))
    mix_ref, knew_ref, vnew_ref, snew_ref, fill_ref, state_ref = rest[len(_TABLE_ORDER):]
    z = z_ref[...]
    t = z.shape[0]
    s_keys = cache_rows + t
    pad = jnp.zeros((KEY_PAD - s_keys, KV_WIDTH), F32)
    k_all = jnp.concatenate([ck_ref[0], z[:, OFF_KA:OFF_KA + KV_WIDTH].astype(F32)], axis=0)
    v_all = jnp.concatenate([cv_ref[0], z[:, OFF_VA:OFF_VA + KV_WIDTH].astype(F32)], axis=0)
    windows = [jnp.concatenate([a, pad], axis=0).astype(BF16) for a in (k_all, v_all)]
    _fill_table(lambda h: sinks_ref[layer, h], fill_ref, t, s_keys)
    for g in range(N_RG):
        for hh in range(RG):
            state_ref[g, :, hh * RET_DV:(hh + 1) * RET_DV] = s0_ref[0, g * RG + hh]
    col = lax.broadcasted_iota(jnp.int32, (1, KEY_PAD), 1)

    def emit(lo):
        def store(v):
            mix_ref[:, lo:lo + v.shape[1]] = v
        return store

    _run_blocks([_mixer_stages(lambda lo, hi: z_ref[:, lo:hi], lambda: windows[0], lambda: windows[1], col < s_keys,
                               fill_ref, state_ref, tb, None, emit(0), emit(ATTN_WIDTH))])
    knew_ref[0] = k_all[s_keys - cache_rows:]
    vnew_ref[0] = v_all[s_keys - cache_rows:]
    _state_to_heads(state_ref, snew_ref)


def _sample_mixer(z, sinks, layer, tables, cache_k, cache_v, state, batch, t, row_offset):
    cache_rows = cache_k.shape[2]
    assert row_offset % t == 0 and cache_rows + t < KEY_PAD
    first_block = row_offset // t
    tabs = [tables[k] for k in _TABLE_ORDER]
    state_shape = (RET_HEADS, RET_DK, RET_DV)
    cache_spec = pl.BlockSpec((1, cache_rows, KV_WIDTH), lambda b: (b, 0, 0))
    state_spec = pl.BlockSpec((1,) + state_shape, lambda b: (b, 0, 0, 0))
    cache_in = pl.BlockSpec((None, 1, cache_rows, KV_WIDTH), lambda b: (layer, b, 0, 0))
    state_in = pl.BlockSpec((None, 1) + state_shape, lambda b: (layer, b, 0, 0, 0))
    return pl.pallas_call(
        functools.partial(_sample_mixer_kernel, layer=layer, cache_rows=cache_rows),
        grid=(batch,),
        in_specs=[pl.BlockSpec(memory_space=pltpu.SMEM),
                  pl.BlockSpec((t, IN_WIDTH), lambda b: (first_block + b, 0)),
                  cache_in, cache_in, state_in] + [_resident(x.shape) for x in tabs],
        out_specs=[pl.BlockSpec((t, MIX_WIDTH), lambda b: (b, 0)), cache_spec, cache_spec, state_spec],
        out_shape=[jax.ShapeDtypeStruct((batch * t, MIX_WIDTH), BF16),
                   jax.ShapeDtypeStruct(cache_k.shape[1:], F32),
                   jax.ShapeDtypeStruct(cache_v.shape[1:], F32),
                   jax.ShapeDtypeStruct(state.shape[1:], F32)],
        scratch_shapes=[pltpu.VMEM((ATTN_KV_HEADS, GROUP * t, KEY_PAD), F32),
                        pltpu.VMEM((N_RG, RET_DK, RGW), F32)],
        compiler_params=_compiler_params(1),
        name="sample_mixer",
    )(sinks, z, cache_k, cache_v, state, *tabs)


def kernel(x_prompt, x_sample, cache_swa_k, cache_swa_v, state_ret, meta_tokens, w_in, w_out, attn_sinks,
           ffn1_w_in, ffn1_w_out, ffn2_w_in, ffn2_w_out, norm_ffn1_pre, norm_ffn1_post, norm_mix_pre,
           norm_mix_post, norm_ffn2_pre, norm_ffn2_post, final_norm):
    batch, seq, d = x_prompt.shape
    dec_batch, dec_seq, _ = x_sample.shape
    depth = w_in.shape[0]
    cache_rows = cache_swa_k.shape[2]
    lead = CHUNK - N_META
    n_main, n_meta, n_sample = batch * seq, CHUNK, dec_batch * dec_seq
    tm = _row_tile(n_main)
    n_tail = -(-(n_meta + n_sample) // tm) * tm
    n_pad = n_tail - n_meta - n_sample
    assert w_in.shape[2] == IN_WIDTH and w_out.shape[1] == MIX_WIDTH
    assert seq % CHUNK == 0 and cache_rows == WINDOW_CHUNKS * CHUNK and seq >= cache_rows

    x_tail = jnp.concatenate([jnp.zeros((lead, d), x_prompt.dtype), meta_tokens.astype(x_prompt.dtype),
                              x_sample.reshape(n_sample, d), jnp.zeros((n_pad, d), x_prompt.dtype)], axis=0)
    x = (x_prompt.reshape(n_main, d), x_tail)

    tables_p = _mixer_tables(CHUNK, WINDOW_CHUNKS * CHUNK + CHUNK)
    tables_s = _mixer_tables(dec_seq, cache_rows + dec_seq)
    cache_k = cache_swa_k.reshape(depth, dec_batch, cache_rows, KV_WIDTH)
    cache_v = cache_swa_v.reshape(depth, dec_batch, cache_rows, KV_WIDTH)

    w_proj, w_mix = _projection_weight(w_in), w_out.astype(BF16)
    w1_in, w1_out = ffn1_w_in.astype(BF16), ffn1_w_out.astype(BF16)
    w2_in, w2_out = ffn2_w_in.astype(BF16), ffn2_w_out.astype(BF16)

    gains = {name: g.reshape(depth, 1, d) for name, g in dict(
        f1pre=norm_ffn1_pre, f1post=norm_ffn1_post, mpre=norm_mix_pre, mpost=norm_mix_post, f2pre=norm_ffn2_pre,
        f2post=norm_ffn2_post).items()}

    pk, pv, ps, sk, sv, ss = [], [], [], [], [], []
    for l in range(depth):
        at = lambda stack: _Layer(stack, l)
        ffn1 = (at(gains["f1pre"]), at(w1_in), at(w1_out), at(gains["f1post"]))
        ffn2 = (at(gains["f2pre"]), at(w2_in), at(w2_out), at(gains["f2post"]))
        mix_w = (at(w_mix), at(gains["mpost"]))
        final = final_norm if l == depth - 1 else None
        x1, z = _token_step(x, ffn1, tm, n_out=n_main + n_tail, proj=(at(gains["mpre"]), at(w_proj)))
        x_main, mix_meta, k_p, v_p, s_p = _prompt_step(z, x1, attn_sinks, l, tables_p, ffn2, mix_w, batch, seq, lead,
                                                       final)
        mix_s, k_s, v_s, s_s = _sample_mixer(z, attn_sinks, l, tables_s, cache_k, cache_v, state_ret,
                                             dec_batch, dec_seq, n_main + n_meta)
        mix_tail = jnp.concatenate([mix_meta, mix_s, jnp.zeros((n_pad, MIX_WIDTH), BF16)], axis=0)
        x_tail = _token_step(x1, ffn2, tm, n_out=n_tail, first_tile=n_main // tm, mix=(mix_tail,) + mix_w,
                             final=final)
        x = (x_main, x_tail)
        pk.append(k_p); pv.append(v_p); ps.append(s_p)
        sk.append(k_s); sv.append(v_s); ss.append(s_s)

    y_main, y_tail = x
    y_prompt = y_main.reshape(batch, seq, d)
    y_sample = y_tail[n_meta:n_meta + n_sample].reshape(dec_batch, dec_seq, d)
    kv_shape_p = (depth, batch, cache_rows, ATTN_KV_HEADS, HEAD_DIM)
    kv_shape_s = (depth, dec_batch, cache_rows, ATTN_KV_HEADS, HEAD_DIM)
    return (y_prompt, y_sample,
            jnp.stack(pk).reshape(kv_shape_p), jnp.stack(pv).reshape(kv_shape_p), jnp.stack(ps),
            jnp.stack(sk).reshape(kv_shape_s), jnp.stack(sv).reshape(kv_shape_s), jnp.stack(ss))
```

```python
import functools
import math
from typing import NamedTuple

import jax
import jax.numpy as jnp
import numpy as np
from jax import lax
from jax.experimental import pallas as pl
from jax.experimental.pallas import tpu as pltpu

CHUNK = 64
N_META = 16
WINDOW_CHUNKS = 2
ATTN_HEADS = 8
ATTN_KV_HEADS = 2
HEAD_DIM = 64
GROUP = ATTN_HEADS // ATTN_KV_HEADS
ATTN_SCALE = HEAD_DIM ** -0.5
LOG2_E = math.log2(math.e)
RET_HEADS = 8
RET_DK = 64
RET_DV = 64
ATTN_WIDTH = ATTN_HEADS * HEAD_DIM
KV_WIDTH = ATTN_KV_HEADS * HEAD_DIM
RET_WIDTH = RET_HEADS * RET_DK
ROPE_BASE = 10000.0
EPS = 1e-6
NEG = -1e30

OFF_QA = 0
OFF_KA = OFF_QA + ATTN_WIDTH
OFF_VA = OFF_KA + KV_WIDTH
OFF_QR = OFF_VA + KV_WIDTH
OFF_KR = OFF_QR + RET_WIDTH
OFF_VR = OFF_KR + RET_WIDTH
OFF_GR = OFF_VR + RET_WIDTH
IN_WIDTH = OFF_GR + RET_WIDTH
MIX_WIDTH = ATTN_WIDTH + RET_WIDTH

LANES = 128
HALF = RET_DK // 2
RG = 4
RGW = RG * RET_DK
N_RG = RET_HEADS // RG
KEY_PAD = 256
V7X_VMEM_LIMIT_BYTES = 56 * 1024 * 1024
FF_CHUNK = 256
MAX_ROW_TILE = 512
ROW_SPLIT = 2
MAX_TILE_CHUNKS = 8

BF16 = jnp.bfloat16
F32 = jnp.float32

assert RGW == 2 * LANES and KV_WIDTH == LANES and GROUP == 4 and ATTN_KV_HEADS == 2


def _row_tile(n, cap=MAX_ROW_TILE):
    t = cap
    while t > 8 and n % t:
        t //= 2
    assert n % t == 0, (n, t)
    return t


def _largest_divisor(n, cap):
    return max(d for d in range(1, cap + 1) if n % d == 0)


def _rms(x):
    return x * lax.rsqrt(jnp.mean(x * x, axis=-1, keepdims=True) + EPS)


def _silu(x):
    return x / (1.0 + jnp.exp(-x))


def _compiler_params(n_axes):
    return pltpu.CompilerParams(dimension_semantics=("arbitrary",) * n_axes,
                                vmem_limit_bytes=V7X_VMEM_LIMIT_BYTES)


def _resident(shape):
    return pl.BlockSpec(shape, lambda *_: (0,) * len(shape), pipeline_mode=pl.Buffered(1))


class _Layer(NamedTuple):
    stack: jax.Array
    index: int

    @property
    def shape(self):
        return self.stack.shape[1:]

    @property
    def spec(self):
        index = self.index
        return pl.BlockSpec((None,) + self.shape, lambda *_: (index,) + (0,) * len(self.shape),
                            pipeline_mode=pl.Buffered(1))


def _row_halves(tm):
    n_split = ROW_SPLIT if tm % (ROW_SPLIT * 16) == 0 else 1
    return [slice(r * tm // n_split, (r + 1) * tm // n_split) for r in range(n_split)]


def _token_units(x_of, mix_ref, p, o_ref, z_ref, h_ref, xn_ref, ff_chunk):
    tm, d_ff = h_ref.shape
    halves = _row_halves(tm)
    if mix_ref is not None:
        for rows in halves:
            y = jnp.dot(mix_ref[rows, :], p["wmix"][...], preferred_element_type=F32)
            o_ref[rows, :] = x_of(rows) + _rms(y) * p["gmix"][...]
            yield
        resid = lambda rows: o_ref[rows, :]
    else:
        resid = x_of
    xn = jnp.concatenate([(_rms(resid(rows)) * p["gpre"][...]).astype(BF16) for rows in halves], axis=0)
    if xn_ref is not None:
        xn_ref[...] = xn
    d = xn.shape[1]
    for j in range(d_ff // ff_chunk):
        lo = j * ff_chunk
        acts = []
        for col in (lo, d_ff + lo):
            if xn_ref is None:
                acts.append(jnp.dot(xn, p["win"][:, col:col + ff_chunk], preferred_element_type=F32))
            else:
                part = None
                for ks in (slice(0, d // 2), slice(d // 2, d)):
                    if part is not None:
                        yield
                    r = jnp.dot(xn_ref[:, ks], p["win"][ks, col:col + ff_chunk], preferred_element_type=F32)
                    part = r if part is None else part + r
                acts.append(part)
            if len(acts) == 1:
                yield
        h_ref[:, lo:lo + ff_chunk] = (_silu(acts[0]) * acts[1]).astype(BF16)
        yield
    ys = []
    for rows in halves:
        ys.append(jnp.dot(h_ref[rows, :], p["wout"][...], preferred_element_type=F32))
        yield
    for rows, y in zip(halves, ys):
        xr = resid(rows) + 0.5 * (_rms(y) * p["gpost"][...])
        if z_ref is not None:
            xn = (_rms(xr) * p["gproj"][...]).astype(BF16)
            z_ref[rows, :] = jnp.dot(xn, p["wproj"][...], preferred_element_type=F32).astype(BF16)
        o_ref[rows, :] = _rms(xr) * p["gfin"][...] if "gfin" in p else xr
        yield


def _n_token_units(tm, d_ff, ff_chunk, has_mix, staged):
    return (3 if has_mix else 2) * len(_row_halves(tm)) + (4 if staged else 2) * (d_ff // ff_chunk)


def _token_kernel(*refs, names, ff_chunk, n_main_tiles, split_in, has_mix, has_proj):
    refs = list(refs)
    take = lambda n: [refs.pop(0) for _ in range(n)]
    if split_in:
        xm_ref, xt_ref = take(2)
        in_main = pl.program_id(0) < n_main_tiles
        x_of = lambda rows: jnp.where(in_main, xm_ref[rows, :], xt_ref[rows, :])
    else:
        (x_ref,) = take(1)
        x_of = lambda rows: x_ref[rows, :]
    mix_ref = take(1)[0] if has_mix else None
    p = dict(zip(names, take(len(names))))
    (o_ref,) = take(1)
    z_ref = take(1)[0] if has_proj else None
    (h_ref,) = take(1)
    assert not refs
    for _ in _token_units(x_of, mix_ref, p, o_ref, z_ref, h_ref, None, ff_chunk):
        pass


def _ffn_chunk(d_ff):
    return FF_CHUNK if d_ff % FF_CHUNK == 0 else d_ff


def _weight_args(d, ffn, mix=None, proj=None, final=None):
    g_pre, w_in, w_out, g_post = ffn
    gain = lambda g: g if isinstance(g, _Layer) else g.reshape(1, d)
    named = []
    if mix is not None:
        named += [("wmix", mix[0]), ("gmix", gain(mix[1]))]
    named += [("gpre", gain(g_pre)), ("win", w_in), ("wout", w_out), ("gpost", gain(g_post))]
    if proj is not None:
        named += [("gproj", gain(proj[0])), ("wproj", proj[1])]
    if final is not None:
        named.append(("gfin", gain(final)))
    names, arrays = zip(*named)
    specs = [a.spec if isinstance(a, _Layer) else _resident(a.shape) for a in arrays]
    return names, [a.stack if isinstance(a, _Layer) else a for a in arrays], specs


def _token_step(x, ffn, tm, *, n_out, first_tile=0, mix=None, proj=None, final=None):
    split_in = isinstance(x, tuple)
    d = x[0].shape[1] if split_in else x.shape[1]
    d_ff = ffn[2].shape[0]
    n_main_tiles = x[0].shape[0] // tm if split_in else 0
    args, in_specs = [], []
    if split_in:
        assert x[0].shape[0] % tm == 0 and x[1].shape[0] % tm == 0
        args += list(x)
        in_specs += [pl.BlockSpec((tm, d), lambda i: (jnp.minimum(i, n_main_tiles - 1), 0)),
                     pl.BlockSpec((tm, d), lambda i: (jnp.maximum(i - n_main_tiles, 0), 0))]
    else:
        args.append(x)
        in_specs.append(pl.BlockSpec((tm, d), lambda i: (first_tile + i, 0)))
    if mix is not None:
        args.append(mix[0])
        in_specs.append(pl.BlockSpec((tm, MIX_WIDTH), lambda i: (i, 0)))
    names, w_arrays, w_specs = _weight_args(d, ffn, mix[1:] if mix is not None else None, proj, final)
    out_shape = [jax.ShapeDtypeStruct((n_out, d), F32)]
    out_specs = [pl.BlockSpec((tm, d), lambda i: (i, 0))]
    if proj is not None:
        out_shape.append(jax.ShapeDtypeStruct((n_out, IN_WIDTH), BF16))
        out_specs.append(pl.BlockSpec((tm, IN_WIDTH), lambda i: (i, 0)))
    outs = pl.pallas_call(
        functools.partial(_token_kernel, names=names, ff_chunk=_ffn_chunk(d_ff), n_main_tiles=n_main_tiles,
                          split_in=split_in, has_mix=mix is not None, has_proj=proj is not None),
        grid=(n_out // tm,),
        in_specs=in_specs + w_specs,
        out_specs=out_specs,
        out_shape=out_shape,
        scratch_shapes=[pltpu.VMEM((tm, d_ff), BF16)],
        compiler_params=_compiler_params(1),
        name="token_step",
    )(*args, *w_arrays)
    return outs[0] if len(outs) == 1 else tuple(outs)


def _permute_ret_columns(w):
    lead = w.shape[:-1]
    return jnp.swapaxes(w.reshape(lead + (N_RG, RG, 2, HALF)), -3, -2).reshape(lead + (RET_WIDTH,))


def _projection_weight(w):
    parts = [w[..., :OFF_QR], _permute_ret_columns(w[..., OFF_QR:OFF_KR]),
             _permute_ret_columns(w[..., OFF_KR:OFF_VR]), w[..., OFF_VR:]]
    return jnp.concatenate(parts, axis=-1).astype(BF16)


def _mixer_tables(t, s_keys):
    f32 = np.float32
    log_g = np.log1p(-np.exp2(-5.0 - np.arange(RET_HEADS, dtype=f32))).astype(f32)
    freqs = (ROPE_BASE ** (-np.arange(HALF, dtype=f32) / HALF)).astype(f32)
    pos = np.arange(t, dtype=f32)
    ang = pos[:, None] * freqs[None, :]
    cos, sin = np.tile(np.cos(ang), (1, RG)), np.tile(np.sin(ang), (1, RG))
    qs = f32(RET_DK ** -0.5)
    rot = jnp.asarray(np.stack([cos * qs, sin * qs, cos, sin]), F32)

    head = lambda g: slice(g * RG, (g + 1) * RG)
    diff = pos[:, None] - pos[None, :]
    intra = np.where(diff >= 0, np.exp(log_g[:, None, None] * np.maximum(diff, 0)), 0).astype(f32)
    kv_dec = np.exp((t - pos)[:, None] * log_g[None, :])
    q_dec = np.exp(pos[:, None] * log_g[None, :])
    dec = []
    for g in range(N_RG):
        kd = np.tile(np.repeat(kv_dec[:, head(g)], HALF, axis=1), (1, 2))
        qd = np.repeat(q_dec[:, head(g)], RET_DV, axis=1)
        dec += [kd, qd]
    dec = jnp.asarray(np.stack(dec), F32)
    dtab = jnp.asarray(np.stack([np.concatenate(list(intra[head(g)]), axis=1) for g in range(N_RG)]), F32)

    g_t = np.exp(t * log_g)
    shift = -t * freqs
    stab = [np.broadcast_to(np.repeat(g_t[head(g)], RET_DV)[None, :], (HALF, RGW)) for g in range(N_RG)]
    stab += [np.broadcast_to(np.cos(shift)[:, None], (HALF, RGW)),
             np.broadcast_to(np.sin(shift)[:, None], (HALF, RGW))]
    stab = jnp.asarray(np.stack(stab), F32)

    r4 = np.arange(RG * t)[:, None] // t
    lane = np.arange(RGW)[None, :]
    mask_k = jnp.asarray(r4 == (lane % LANES) // HALF, BF16)
    mask_v = jnp.asarray(r4 == lane // RET_DV, BF16)
    rs = np.arange(LANES)[:, None] // HALF
    mask_s = jnp.asarray(rs == lane // RET_DV, F32)
    seg = jnp.asarray(np.arange(RGW)[:, None] // RET_DV == lane // RET_DV, BF16)

    rowk = np.arange(KEY_PAD)[:, None] < s_keys
    lane_v = np.arange(LANES)[None, :] // HEAD_DIM
    v_keep = jnp.stack([jnp.asarray(rowk & (lane_v == p), BF16) for p in range(2)])
    v_ones = jnp.stack([jnp.asarray(np.broadcast_to(lane_v != p, (KEY_PAD, LANES)), BF16) for p in range(2)])
    return dict(rot=rot, dec=dec, dtab=dtab, stab=stab, mask_k=mask_k, mask_v=mask_v, mask_s=mask_s, seg=seg,
                v_keep=v_keep, v_ones=v_ones)


_TABLE_ORDER = ("rot", "dec", "dtab", "stab", "mask_k", "mask_v", "mask_s", "seg", "v_keep", "v_ones")


def _fill_table(sink_of, fill_ref, t, s_keys):
    col = lax.broadcasted_iota(jnp.int32, (t, KEY_PAD), 1)
    for kv in range(ATTN_KV_HEADS):
        for j in range(GROUP):
            fill_ref[kv, j * t:(j + 1) * t, :] = jnp.where(col == s_keys, sink_of(kv * GROUP + j) * LOG2_E, NEG)


def _attn_scores(q, kwin):
    t = q.shape[0]
    lower = lax.broadcasted_iota(jnp.int32, (t, LANES), 1) < HEAD_DIM
    out = []
    for kv in range(ATTN_KV_HEADS):
        keep = jnp.where(lower if kv == 0 else ~lower, ATTN_SCALE * LOG2_E, 0.0)
        parts = []
        for j in range(2):
            blk = q[:, (2 * kv + j) * LANES:(2 * kv + j + 1) * LANES].astype(F32)
            rolled = pltpu.roll(blk, HEAD_DIM, axis=1)
            lo_head, hi_head = (blk, rolled) if kv == 0 else (rolled, blk)
            parts += [lo_head * keep, hi_head * keep]
        qs = jnp.concatenate(parts, axis=0).astype(BF16)
        out.append(lax.dot_general(qs, kwin, (((1,), (1,)), ((), ())), preferred_element_type=F32))
        yield False
    return out


def _attn_values(scores, vwin, valid, fill_ref, tb):
    out = []
    for kv, s in enumerate(scores):
        s = jnp.where(valid, s, fill_ref[kv])
        p = jnp.exp2(s - jnp.max(s, axis=-1, keepdims=True))
        vext = vwin * tb["v_keep"][kv] + tb["v_ones"][kv]
        out.append(jnp.dot(p.astype(BF16), vext, preferred_element_type=F32))
        yield False
    return out


def _attn_normalise(results, t):
    lower = lax.broadcasted_iota(jnp.int32, (t, LANES), 1) < HEAD_DIM
    cols = []
    for kv, res in enumerate(results):
        swapped = pltpu.roll(res, HEAD_DIM, axis=1)
        for j in range(2):
            lo_rows, hi_rows = slice(2 * j * t, (2 * j + 1) * t), slice((2 * j + 1) * t, (2 * j + 2) * t)
            if kv == 0:
                lo, hi = res[lo_rows] / swapped[lo_rows], swapped[hi_rows] / res[hi_rows]
            else:
                lo, hi = swapped[lo_rows] / res[lo_rows], res[hi_rows] / swapped[hi_rows]
            cols.append(jnp.where(lower, lo, hi))
    return jnp.concatenate(cols, axis=-1)


def _ret_scores(zcols, g, tb, row_keep):
    lo = g * RGW
    cq, sq, ck, sk = tb["rot"][0], tb["rot"][1], tb["rot"][2], tb["rot"][3]
    q1 = zcols(OFF_QR + lo, OFF_QR + lo + LANES).astype(F32)
    q2 = zcols(OFF_QR + lo + LANES, OFF_QR + lo + RGW).astype(F32)
    k1 = zcols(OFF_KR + lo, OFF_KR + lo + LANES).astype(F32)
    k2 = zcols(OFF_KR + lo + LANES, OFF_KR + lo + RGW).astype(F32)
    if row_keep is not None:
        k1, k2 = k1 * row_keep, k2 * row_keep
    q_rot = jnp.concatenate([q1 * cq - q2 * sq, q2 * cq + q1 * sq], axis=-1)
    k_rot = jnp.concatenate([k1 * ck - k2 * sk, k2 * ck + k1 * sk], axis=-1)
    qb, kb = q_rot.astype(BF16), k_rot.astype(BF16)
    kdb = (k_rot * tb["dec"][2 * g]).astype(BF16)
    vb = zcols(OFF_VR + lo, OFF_VR + lo + RGW)
    k_bd = jnp.concatenate([kb] * RG, axis=0) * tb["mask_k"][...]
    scores = lax.dot_general(qb, k_bd, (((1,), (1,)), ((), ())), preferred_element_type=F32)
    yield False
    w_full = lax.dot_general(kdb, vb, (((0,), (0,)), ((), ())), preferred_element_type=F32)
    yield False
    return qb, vb, scores, w_full


def _ret_outputs(g, qb, vb, scores, w_full, state_ref, tb):
    mask_s = tb["mask_s"][...]
    v_bd = jnp.concatenate([vb] * RG, axis=0) * tb["mask_v"][...]
    intra = jnp.dot((scores * tb["dtab"][g]).astype(BF16), v_bd, preferred_element_type=F32)
    yield False
    c1, c2 = state_ref[g, 0:HALF, :], state_ref[g, HALF:2 * HALF, :]
    mask_sb = mask_s.astype(BF16)
    s_bd = jnp.concatenate([jnp.concatenate([c.astype(BF16)] * RG, axis=0) * mask_sb for c in (c1, c2)], axis=0)
    cross = jnp.dot(qb, s_bd, preferred_element_type=F32)
    yield False
    w = []
    for half in range(2):
        wm = w_full[half * LANES:(half + 1) * LANES] * mask_s
        w.append(wm[0:HALF] + wm[HALF:2 * HALF] + wm[2 * HALF:3 * HALF] + wm[3 * HALF:4 * HALF])
    cos_s, sin_s = tb["stab"][N_RG], tb["stab"][N_RG + 1]
    a1 = tb["stab"][g] * c1 + w[0]
    a2 = tb["stab"][g] * c2 + w[1]
    state_ref[g, 0:HALF, :] = a1 * cos_s - a2 * sin_s
    state_ref[g, HALF:2 * HALF, :] = a2 * cos_s + a1 * sin_s
    return intra, cross


def _ret_square_sums(outs, tb):
    parts = []
    for o in outs:
        sq = o * o
        hi = sq.astype(BF16)
        parts += [hi, (sq - hi.astype(F32)).astype(BF16)]
    return jnp.dot(jnp.concatenate(parts, axis=0), tb["seg"][...], preferred_element_type=F32)


def _mixer_stages(zcols, kwin, vwin, valid, fill_ref, state_ref, tb, row_keep, emit_attn, emit_ret):
    attn_scores = yield from _attn_scores(zcols(OFF_QA, OFF_QA + ATTN_WIDTH), kwin())
    ret = []
    for g in range(N_RG):
        ret.append((yield from _ret_scores(zcols, g, tb, row_keep)))
    t = ret[0][0].shape[0]
    yield True
    attn_res = yield from _attn_values(attn_scores, vwin(), valid, fill_ref, tb)
    pairs = []
    for g in range(N_RG):
        pairs.append((yield from _ret_outputs(g, *ret[g], state_ref, tb)))
    yield True
    emit_attn(_attn_normalise(attn_res, t).astype(BF16))
    outs = [intra + cross * tb["dec"][2 * g + 1] for g, (intra, cross) in enumerate(pairs)]
    ssq = _ret_square_sums(outs, tb)
    res = []
    for g, o in enumerate(outs):
        gate = zcols(OFF_GR + g * RGW, OFF_GR + (g + 1) * RGW).astype(F32)
        ss = ssq[2 * g * t:(2 * g + 1) * t] + ssq[(2 * g + 1) * t:(2 * g + 2) * t]
        res.append(o * lax.rsqrt(ss * (1.0 / RET_DV) + EPS) * _silu(gate))
    emit_ret(jnp.concatenate(res, axis=-1).astype(BF16))
    yield True


N_STAGES = 3
FIRST_EMIT_STAGE = 2
MATMUL_GROUPS_PER_BLOCK = 2 * (ATTN_KV_HEADS + 2 * N_RG + 1) + 1

def _stage_order(n_blocks):
    order = []
    for step in range(n_blocks + N_STAGES - 1):
        order += [step - stage for stage in reversed(range(N_STAGES)) if 0 <= step - stage < n_blocks]
    return order


def _run_stage(block, between=lambda: None):
    done = False
    while not done:
        done = next(block)
        between()


def _run_blocks(blocks):
    for b in _stage_order(len(blocks)):
        _run_stage(blocks[b])


def _state_to_heads(state_ref, out_ref):
    for g in range(N_RG):
        for hh in range(RG):
            out_ref[0, g * RG + hh] = state_ref[g, :, hh * RET_DV:(hh + 1) * RET_DV]


def _prompt_kernel(sinks_ref, z_ref, zmeta_ref, x_ref, *rest, names, layer, tile_chunks, n_g, n_tiles, lead,
                   ff_chunk):
    rest = list(rest)
    take = lambda n: [rest.pop(0) for _ in range(n)]
    p = dict(zip(names, take(len(names))))
    tb = dict(zip(_TABLE_ORDER, take(len(_TABLE_ORDER))))
    o_ref, mixmeta_ref, klast_ref, vlast_ref, sfin_ref = take(5)
    mix_ref, h_ref, xn_ref, kctx_ref, vctx_ref, fill_ref, state_ref = take(7)
    assert not rest
    s = pl.program_id(0)
    g = lax.rem(jnp.minimum(s, n_tiles - 1), n_g)
    ctx = WINDOW_CHUNKS * CHUNK
    rows = tile_chunks * CHUNK
    s_keys = ctx + CHUNK
    col = lax.broadcasted_iota(jnp.int32, (1, KEY_PAD), 1)

    def put_keys(dst, z, n):
        kctx_ref[dst:dst + n, :] = z[:, OFF_KA:OFF_KA + KV_WIDTH]
        vctx_ref[dst:dst + n, :] = z[:, OFF_VA:OFF_VA + KV_WIDTH]

    def block(z_src, out_ref, r0, first_row, row_keep=None):
        rows_c = slice(r0, r0 + CHUNK)
        win = slice(r0, r0 + KEY_PAD)
        valid = (col < s_keys) & (col + (first_row - ctx) >= lead)

        def emit(lo):
            def store(v):
                out_ref[rows_c, lo:lo + v.shape[1]] = v
            return store

        return _mixer_stages(lambda lo, hi: z_src[rows_c, lo:hi], lambda: kctx_ref[win, :], lambda: vctx_ref[win, :],
                             valid, fill_ref, state_ref, tb, row_keep, emit(0), emit(ATTN_WIDTH))

    @pl.when(s == 0)
    def _():
        mix_ref[...] = jnp.zeros(mix_ref.shape, BF16)

    @pl.when(g == 0)
    def _():
        for r in (kctx_ref, vctx_ref):
            r[...] = jnp.zeros(r.shape, BF16)
        state_ref[...] = jnp.zeros(state_ref.shape, F32)
        _fill_table(lambda h: sinks_ref[layer, h], fill_ref, CHUNK, s_keys)
        zm = zmeta_ref[...]
        put_keys(ctx, zm, CHUNK)
        row_keep = (lax.broadcasted_iota(jnp.int32, (CHUNK, 1), 0) >= lead).astype(F32)
        _run_blocks([block(zmeta_ref, mixmeta_ref, 0, 0, row_keep)])
        put_keys(ctx - CHUNK, zm, CHUNK)

    units = _token_units(lambda r: x_ref[r, :], mix_ref, p, o_ref, None, h_ref, xn_ref, ff_chunk)
    n_units = _n_token_units(rows, h_ref.shape[1], ff_chunk, True, True)
    n_mix_reads = len(_row_halves(rows))

    put_keys(ctx, z_ref[...], rows)
    tile_row = (1 + g * tile_chunks) * CHUNK
    blocks = [block(z_ref, mix_ref, c * CHUNK, tile_row + c * CHUNK) for c in range(tile_chunks)]
    order = _stage_order(len(blocks))
    first_emit = [i for i, b in enumerate(order) if b == 0][FIRST_EMIT_STAGE]
    n_slots = len(blocks) * MATMUL_GROUPS_PER_BLOCK
    progress = dict(slots=0, issued=0)

    def issue_units(at_least=0):
        progress["slots"] += 1
        target = max(at_least, progress["issued"], -(-n_units * progress["slots"] // n_slots))
        for _ in range(target - progress["issued"]):
            next(units)
        progress["issued"] = target

    for i, b in enumerate(order):
        if i == first_emit:
            progress["slots"] -= 1
            issue_units(at_least=n_mix_reads)
        _run_stage(blocks[b], issue_units)
    assert progress["slots"] == n_slots and next(units, None) is None

    for r in (kctx_ref, vctx_ref):
        r[0:ctx, :] = r[rows:rows + ctx, :]

    @pl.when((g == n_g - 1) & (s < n_tiles))
    def _():
        klast_ref[0] = kctx_ref[0:ctx, :].astype(F32)
        vlast_ref[0] = vctx_ref[0:ctx, :].astype(F32)
        _state_to_heads(state_ref, sfin_ref)


def _prompt_step(z, x, sinks, layer, tables, ffn, mix, batch, seq, lead, final=None):
    d = x.shape[1]
    d_ff = ffn[2].shape[0]
    n_main = batch * seq
    n_chunks = seq // CHUNK
    tile_chunks = _largest_divisor(n_chunks, MAX_TILE_CHUNKS)
    n_g = n_chunks // tile_chunks
    n_tiles = batch * n_g
    rows = tile_chunks * CHUNK
    ctx = WINDOW_CHUNKS * CHUNK
    assert rows >= ctx and n_main % CHUNK == 0 and ctx + CHUNK < KEY_PAD
    meta_block0 = n_main // CHUNK
    tabs = [tables[k] for k in _TABLE_ORDER]
    names, w_arrays, w_specs = _weight_args(d, ffn, mix, None, final)
    state_shape = (RET_HEADS, RET_DK, RET_DV)
    ctx_rows = ctx + rows + KEY_PAD - (ctx + CHUNK)
    ctx_buf = pltpu.VMEM((ctx_rows, KV_WIDTH), BF16)
    cur = lambda s: jnp.minimum(s, n_tiles - 1)
    prev = lambda s: jnp.maximum(s - 1, 0)
    seq_of = lambda s: cur(s) // n_g
    return pl.pallas_call(
        functools.partial(_prompt_kernel, names=names, layer=layer, tile_chunks=tile_chunks, n_g=n_g,
                          n_tiles=n_tiles, lead=lead, ff_chunk=_ffn_chunk(d_ff)),
        grid=(n_tiles + 1,),
        in_specs=[pl.BlockSpec(memory_space=pltpu.SMEM),
                  pl.BlockSpec((rows, IN_WIDTH), lambda s: (cur(s), 0)),
                  pl.BlockSpec((CHUNK, IN_WIDTH), lambda s: (meta_block0, 0)),
                  pl.BlockSpec((rows, d), lambda s: (prev(s), 0))]
                 + w_specs + [_resident(t.shape) for t in tabs],
        out_specs=[pl.BlockSpec((rows, d), lambda s: (prev(s), 0)),
                   pl.BlockSpec((CHUNK, MIX_WIDTH), lambda s: (0, 0)),
                   pl.BlockSpec((1, ctx, KV_WIDTH), lambda s: (seq_of(s), 0, 0)),
                   pl.BlockSpec((1, ctx, KV_WIDTH), lambda s: (seq_of(s), 0, 0)),
                   pl.BlockSpec((1,) + state_shape, lambda s: (seq_of(s), 0, 0, 0))],
        out_shape=[jax.ShapeDtypeStruct((n_main, d), F32),
                   jax.ShapeDtypeStruct((CHUNK, MIX_WIDTH), BF16),
                   jax.ShapeDtypeStruct((batch, ctx, KV_WIDTH), F32),
                   jax.ShapeDtypeStruct((batch, ctx, KV_WIDTH), F32),
                   jax.ShapeDtypeStruct((batch,) + state_shape, F32)],
        scratch_shapes=[pltpu.VMEM((rows, MIX_WIDTH), BF16), pltpu.VMEM((rows, d_ff), BF16),
                        pltpu.VMEM((rows, d), BF16), ctx_buf, ctx_buf,
                        pltpu.VMEM((ATTN_KV_HEADS, GROUP * CHUNK, KEY_PAD), F32),
                        pltpu.VMEM((N_RG, RET_DK, RGW), F32)],
        compiler_params=_compiler_params(1),
        name="prompt_step",
    )(sinks, z, z, x, *w_arrays, *tabs)


def _sample_mixer_kernel(sinks_ref, z_ref, ck_ref, cv_ref, s0_ref, *rest, layer, cache_rows):
    tb = dict(zip(_TABLE_ORDER, rest[:len(_TABLE_ORDER)]))
    mix_ref, knew_ref, vnew_ref, snew_ref, fill_ref, state_ref = rest[len(_TABLE_ORDER):]
    z = z_ref[...]
    t = z.shape[0]
    s_keys = cache_rows + t
    pad = jnp.zeros((KEY_PAD - s_keys, KV_WIDTH), F32)
    k_all = jnp.concatenate([ck_ref[0], z[:, OFF_KA:OFF_KA + KV_WIDTH].astype(F32)], axis=0)
    v_all = jnp.concatenate([cv_ref[0], z[:, OFF_VA:OFF_VA + KV_WIDTH].astype(F32)], axis=0)
    windows = [jnp.concatenate([a, pad], axis=0).astype(BF16) for a in (k_all, v_all)]
    _fill_table(lambda h: sinks_ref[layer, h], fill_ref, t, s_keys)
    for g in range(N_RG):
        for hh in range(RG):
            state_ref[g, :, hh * RET_DV:(hh + 1) * RET_DV] = s0_ref[0, g * RG + hh]
    col = lax.broadcasted_iota(jnp.int32, (1, KEY_PAD), 1)

    def emit(lo):
        def store(v):
            mix_ref[:, lo:lo + v.shape[1]] = v
        return store

    _run_blocks([_mixer_stages(lambda lo, hi: z_ref[:, lo:hi], lambda: windows[0], lambda: windows[1], col < s_keys,
                               fill_ref, state_ref, tb, None, emit(0), emit(ATTN_WIDTH))])
    knew_ref[0] = k_all[s_keys - cache_rows:]
    vnew_ref[0] = v_all[s_keys - cache_rows:]
    _state_to_heads(state_ref, snew_ref)


def _sample_mixer(z, sinks, layer, tables, cache_k, cache_v, state, batch, t, row_offset):
    cache_rows = cache_k.shape[2]
    assert row_offset % t == 0 and cache_rows + t < KEY_PAD
    first_block = row_offset // t
    tabs = [tables[k] for k in _TABLE_ORDER]
    state_shape = (RET_HEADS, RET_DK, RET_DV)
    cache_spec = pl.BlockSpec((1, cache_rows, KV_WIDTH), lambda b: (b, 0, 0))
    state_spec = pl.BlockSpec((1,) + state_shape, lambda b: (b, 0, 0, 0))
    cache_in = pl.BlockSpec((None, 1, cache_rows, KV_WIDTH), lambda b: (layer, b, 0, 0))
    state_in = pl.BlockSpec((None, 1) + state_shape, lambda b: (layer, b, 0, 0, 0))
    return pl.pallas_call(
        functools.partial(_sample_mixer_kernel, layer=layer, cache_rows=cache_rows),
        grid=(batch,),
        in_specs=[pl.BlockSpec(memory_space=pltpu.SMEM),
                  pl.BlockSpec((t, IN_WIDTH), lambda b: (first_block + b, 0)),
                  cache_in, cache_in, state_in] + [_resident(x.shape) for x in tabs],
        out_specs=[pl.BlockSpec((t, MIX_WIDTH), lambda b: (b, 0)), cache_spec, cache_spec, state_spec],
        out_shape=[jax.ShapeDtypeStruct((batch * t, MIX_WIDTH), BF16),
                   jax.ShapeDtypeStruct(cache_k.shape[1:], F32),
                   jax.ShapeDtypeStruct(cache_v.shape[1:], F32),
                   jax.ShapeDtypeStruct(state.shape[1:], F32)],
        scratch_shapes=[pltpu.VMEM((ATTN_KV_HEADS, GROUP * t, KEY_PAD), F32),
                        pltpu.VMEM((N_RG, RET_DK, RGW), F32)],
        compiler_params=_compiler_params(1),
        name="sample_mixer",
    )(sinks, z, cache_k, cache_v, state, *tabs)


def kernel(x_prompt, x_sample, cache_swa_k, cache_swa_v, state_ret, meta_tokens, w_in, w_out, attn_sinks,
           ffn1_w_in, ffn1_w_out, ffn2_w_in, ffn2_w_out, norm_ffn1_pre, norm_ffn1_post, norm_mix_pre,
           norm_mix_post, norm_ffn2_pre, norm_ffn2_post, final_norm):
    batch, seq, d = x_prompt.shape
    dec_batch, dec_seq, _ = x_sample.shape
    depth = w_in.shape[0]
    cache_rows = cache_swa_k.shape[2]
    lead = CHUNK - N_META
    n_main, n_meta, n_sample = batch * seq, CHUNK, dec_batch * dec_seq
    tm = _row_tile(n_main)
    n_tail = -(-(n_meta + n_sample) // tm) * tm
    n_pad = n_tail - n_meta - n_sample
    assert w_in.shape[2] == IN_WIDTH and w_out.shape[1] == MIX_WIDTH
    assert seq % CHUNK == 0 and cache_rows == WINDOW_CHUNKS * CHUNK and seq >= cache_rows

    x_tail = jnp.concatenate([jnp.zeros((lead, d), x_prompt.dtype), meta_tokens.astype(x_prompt.dtype),
                              x_sample.reshape(n_sample, d), jnp.zeros((n_pad, d), x_prompt.dtype)], axis=0)
    x = (x_prompt.reshape(n_main, d), x_tail)

    tables_p = _mixer_tables(CHUNK, WINDOW_CHUNKS * CHUNK + CHUNK)
    tables_s = _mixer_tables(dec_seq, cache_rows + dec_seq)
    cache_k = cache_swa_k.reshape(depth, dec_batch, cache_rows, KV_WIDTH)
    cache_v = cache_swa_v.reshape(depth, dec_batch, cache_rows, KV_WIDTH)

    w_proj, w_mix = _projection_weight(w_in), w_out.astype(BF16)
    w1_in, w1_out = ffn1_w_in.astype(BF16), ffn1_w_out.astype(BF16)
    w2_in, w2_out = ffn2_w_in.astype(BF16), ffn2_w_out.astype(BF16)

    gains = {name: g.reshape(depth, 1, d) for name, g in dict(
        f1pre=norm_ffn1_pre, f1post=norm_ffn1_post, mpre=norm_mix_pre, mpost=norm_mix_post, f2pre=norm_ffn2_pre,
        f2post=norm_ffn2_post).items()}

    pk, pv, ps, sk, sv, ss = [], [], [], [], [], []
    for l in range(depth):
        at = lambda stack: _Layer(stack, l)
        ffn1 = (at(gains["f1pre"]), at(w1_in), at(w1_out), at(gains["f1post"]))
        ffn2 = (at(gains["f2pre"]), at(w2_in), at(w2_out), at(gains["f2post"]))
        mix_w = (at(w_mix), at(gains["mpost"]))
        final = final_norm if l == depth - 1 else None
        x1, z = _token_step(x, ffn1, tm, n_out=n_main + n_tail, proj=(at(gains["mpre"]), at(w_proj)))
        x_main, mix_meta, k_p, v_p, s_p = _prompt_step(z, x1, attn_sinks, l, tables_p, ffn2, mix_w, batch, seq, lead,
                                                       final)
        mix_s, k_s, v_s, s_s = _sample_mixer(z, attn_sinks, l, tables_s, cache_k, cache_v, state_ret,
                                             dec_batch, dec_seq, n_main + n_meta)
        mix_tail = jnp.concatenate([mix_meta, mix_s, jnp.zeros((n_pad, MIX_WIDTH), BF16)], axis=0)
        x_tail = _token_step(x1, ffn2, tm, n_out=n_tail, first_tile=n_main // tm, mix=(mix_tail,) + mix_w,
                             final=final)
        x = (x_main, x_tail)
        pk.append(k_p); pv.append(v_p); ps.append(s_p)
        sk.append(k_s); sv.append(v_s); ss.append(s_s)

    y_main, y_tail = x
    y_prompt = y_main.reshape(batch, seq, d)
    y_sample = y_tail[n_meta:n_meta + n_sample].reshape(dec_batch, dec_seq, d)
    kv_shape_p = (depth, batch, cache_rows, ATTN_KV_HEADS, HEAD_DIM)
    kv_shape_s = (depth, dec_batch, cache_rows, ATTN_KV_HEADS, HEAD_DIM)
    return (y_prompt, y_sample,
            jnp.stack(pk).reshape(kv_shape_p), jnp.stack(pv).reshape(kv_shape_p), jnp.stack(ps),
            jnp.stack(sk).reshape(kv_shape_s), jnp.stack(sv).reshape(kv_shape_s), jnp.stack(ss))
```

```python
import functools
import math
from typing import NamedTuple

import jax
import jax.numpy as jnp
import numpy as np
from jax import lax
from jax.experimental import pallas as pl
from jax.experimental.pallas import tpu as pltpu

CHUNK = 64
N_META = 16
WINDOW_CHUNKS = 2
ATTN_HEADS = 8
ATTN_KV_HEADS = 2
HEAD_DIM = 64
GROUP = ATTN_HEADS // ATTN_KV_HEADS
ATTN_SCALE = HEAD_DIM ** -0.5
LOG2_E = math.log2(math.e)
RET_HEADS = 8
RET_DK = 64
RET_DV = 64
ATTN_WIDTH = ATTN_HEADS * HEAD_DIM
KV_WIDTH = ATTN_KV_HEADS * HEAD_DIM
RET_WIDTH = RET_HEADS * RET_DK
ROPE_BASE = 10000.0
EPS = 1e-6
NEG = -1e30

OFF_QA = 0
OFF_KA = OFF_QA + ATTN_WIDTH
OFF_VA = OFF_KA + KV_WIDTH
OFF_QR = OFF_VA + KV_WIDTH
OFF_KR = OFF_QR + RET_WIDTH
OFF_VR = OFF_KR + RET_WIDTH
OFF_GR = OFF_VR + RET_WIDTH
IN_WIDTH = OFF_GR + RET_WIDTH
MIX_WIDTH = ATTN_WIDTH + RET_WIDTH

LANES = 128
F32_SUBLANES = 8
BF16_SUBLANES = 16
HALF = RET_DK // 2
RG = 4
RGW = RG * RET_DK
N_RG = RET_HEADS // RG
KEY_PAD = 256
V7X_VMEM_LIMIT_BYTES = 56 * 1024 * 1024
FF_CHUNK = 256
MAX_ROW_TILE = 512
ROW_SPLIT = 2
MAX_TILE_CHUNKS = 8

BF16 = jnp.bfloat16
F32 = jnp.float32

assert RGW == 2 * LANES and KV_WIDTH == LANES and GROUP == 4 and ATTN_KV_HEADS == 2


def _row_tile(n, cap=MAX_ROW_TILE):
    t = cap
    while t > F32_SUBLANES and n % t:
        t //= 2
    assert n % t == 0, (n, t)
    return t


def _largest_divisor(n, cap):
    return max(d for d in range(1, cap + 1) if n % d == 0)


def _rms(x):
    return x * lax.rsqrt(jnp.mean(x * x, axis=-1, keepdims=True) + EPS)


def _silu(x):
    return x / (1.0 + jnp.exp(-x))


def _compiler_params(n_axes):
    return pltpu.CompilerParams(dimension_semantics=("arbitrary",) * n_axes,
                                vmem_limit_bytes=V7X_VMEM_LIMIT_BYTES)


def _resident(shape):
    return pl.BlockSpec(shape, lambda *_: (0,) * len(shape), pipeline_mode=pl.Buffered(1))


class _Layer(NamedTuple):
    stack: jax.Array
    index: int

    @property
    def shape(self):
        return self.stack.shape[1:]

    @property
    def spec(self):
        index = self.index
        return pl.BlockSpec((None,) + self.shape, lambda *_: (index,) + (0,) * len(self.shape),
                            pipeline_mode=pl.Buffered(1))


def _row_halves(tm):
    n_split = ROW_SPLIT if tm % (ROW_SPLIT * BF16_SUBLANES) == 0 else 1
    return [slice(r * tm // n_split, (r + 1) * tm // n_split) for r in range(n_split)]


def _token_units(x_of, mix_ref, p, o_ref, z_ref, h_ref, xn_ref, ff_chunk):
    tm, d_ff = h_ref.shape
    halves = _row_halves(tm)
    if mix_ref is not None:
        for rows in halves:
            y = jnp.dot(mix_ref[rows, :], p["wmix"][...], preferred_element_type=F32)
            o_ref[rows, :] = x_of(rows) + _rms(y) * p["gmix"][...]
            yield
        resid = lambda rows: o_ref[rows, :]
    else:
        resid = x_of
    xn = jnp.concatenate([(_rms(resid(rows)) * p["gpre"][...]).astype(BF16) for rows in halves], axis=0)
    if xn_ref is not None:
        xn_ref[...] = xn
    for j in range(d_ff // ff_chunk):
        lo = j * ff_chunk
        if xn_ref is not None:
            xn = xn_ref[...]
        gate = jnp.dot(xn, p["win"][:, lo:lo + ff_chunk], preferred_element_type=F32)
        yield
        up = jnp.dot(xn, p["win"][:, d_ff + lo:d_ff + lo + ff_chunk], preferred_element_type=F32)
        h_ref[:, lo:lo + ff_chunk] = (_silu(gate) * up).astype(BF16)
        yield
    ys = []
    for rows in halves:
        ys.append(jnp.dot(h_ref[rows, :], p["wout"][...], preferred_element_type=F32))
        yield
    for rows, y in zip(halves, ys):
        xr = resid(rows) + 0.5 * (_rms(y) * p["gpost"][...])
        if z_ref is not None:
            xn = (_rms(xr) * p["gproj"][...]).astype(BF16)
            z_ref[rows, :] = jnp.dot(xn, p["wproj"][...], preferred_element_type=F32).astype(BF16)
        o_ref[rows, :] = _rms(xr) * p["gfin"][...] if "gfin" in p else xr
        yield


def _n_token_units(tm, d_ff, ff_chunk, has_mix):
    return (3 if has_mix else 2) * len(_row_halves(tm)) + 2 * (d_ff // ff_chunk)


def _token_kernel(*refs, names, ff_chunk, n_main_tiles, split_in, has_mix, has_proj):
    refs = list(refs)
    take = lambda n: [refs.pop(0) for _ in range(n)]
    if split_in:
        xm_ref, xt_ref = take(2)
        in_main = pl.program_id(0) < n_main_tiles
        x_of = lambda rows: jnp.where(in_main, xm_ref[rows, :], xt_ref[rows, :])
    else:
        (x_ref,) = take(1)
        x_of = lambda rows: x_ref[rows, :]
    mix_ref = take(1)[0] if has_mix else None
    p = dict(zip(names, take(len(names))))
    (o_ref,) = take(1)
    z_ref = take(1)[0] if has_proj else None
    (h_ref,) = take(1)
    assert not refs
    for _ in _token_units(x_of, mix_ref, p, o_ref, z_ref, h_ref, None, ff_chunk):
        pass


def _ffn_chunk(d_ff):
    return FF_CHUNK if d_ff % FF_CHUNK == 0 else d_ff


def _weight_args(d, ffn, mix=None, proj=None, final=None):
    g_pre, w_in, w_out, g_post = ffn
    gain = lambda g: g if isinstance(g, _Layer) else g.reshape(1, d)
    named = []
    if mix is not None:
        named += [("wmix", mix[0]), ("gmix", gain(mix[1]))]
    named += [("gpre", gain(g_pre)), ("win", w_in), ("wout", w_out), ("gpost", gain(g_post))]
    if proj is not None:
        named += [("gproj", gain(proj[0])), ("wproj", proj[1])]
    if final is not None:
        named.append(("gfin", gain(final)))
    names, arrays = zip(*named)
    specs = [a.spec if isinstance(a, _Layer) else _resident(a.shape) for a in arrays]
    return names, [a.stack if isinstance(a, _Layer) else a for a in arrays], specs


def _token_step(x, ffn, tm, *, n_out, first_tile=0, mix=None, proj=None, final=None):
    split_in = isinstance(x, tuple)
    d = x[0].shape[1] if split_in else x.shape[1]
    d_ff = ffn[2].shape[0]
    n_main_tiles = x[0].shape[0] // tm if split_in else 0
    args, in_specs = [], []
    if split_in:
        assert x[0].shape[0] % tm == 0 and x[1].shape[0] % tm == 0
        args += list(x)
        in_specs += [pl.BlockSpec((tm, d), lambda i: (jnp.minimum(i, n_main_tiles - 1), 0)),
                     pl.BlockSpec((tm, d), lambda i: (jnp.maximum(i - n_main_tiles, 0), 0))]
    else:
        args.append(x)
        in_specs.append(pl.BlockSpec((tm, d), lambda i: (first_tile + i, 0)))
    if mix is not None:
        args.append(mix[0])
        in_specs.append(pl.BlockSpec((tm, MIX_WIDTH), lambda i: (i, 0)))
    names, w_arrays, w_specs = _weight_args(d, ffn, mix[1:] if mix is not None else None, proj, final)
    out_shape = [jax.ShapeDtypeStruct((n_out, d), F32)]
    out_specs = [pl.BlockSpec((tm, d), lambda i: (i, 0))]
    if proj is not None:
        out_shape.append(jax.ShapeDtypeStruct((n_out, IN_WIDTH), BF16))
        out_specs.append(pl.BlockSpec((tm, IN_WIDTH), lambda i: (i, 0)))
    outs = pl.pallas_call(
        functools.partial(_token_kernel, names=names, ff_chunk=_ffn_chunk(d_ff), n_main_tiles=n_main_tiles,
                          split_in=split_in, has_mix=mix is not None, has_proj=proj is not None),
        grid=(n_out // tm,),
        in_specs=in_specs + w_specs,
        out_specs=out_specs,
        out_shape=out_shape,
        scratch_shapes=[pltpu.VMEM((tm, d_ff), BF16)],
        compiler_params=_compiler_params(1),
        name="token_step",
    )(*args, *w_arrays)
    return outs[0] if len(outs) == 1 else tuple(outs)


def _permute_ret_columns(w):
    lead = w.shape[:-1]
    return jnp.swapaxes(w.reshape(lead + (N_RG, RG, 2, HALF)), -3, -2).reshape(lead + (RET_WIDTH,))


def _projection_weight(w):
    parts = [w[..., :OFF_QR], _permute_ret_columns(w[..., OFF_QR:OFF_KR]),
             _permute_ret_columns(w[..., OFF_KR:OFF_VR]), w[..., OFF_VR:]]
    return jnp.concatenate(parts, axis=-1).astype(BF16)


def _mixer_tables(t, s_keys):
    f32 = np.float32
    log_g = np.log1p(-np.exp2(-5.0 - np.arange(RET_HEADS, dtype=f32))).astype(f32)
    freqs = (ROPE_BASE ** (-np.arange(HALF, dtype=f32) / HALF)).astype(f32)
    pos = np.arange(t, dtype=f32)
    ang = pos[:, None] * freqs[None, :]
    cos, sin = np.tile(np.cos(ang), (1, RG)), np.tile(np.sin(ang), (1, RG))
    qs = f32(RET_DK ** -0.5)
    rot = jnp.asarray(np.stack([cos * qs, sin * qs, cos, sin]), F32)

    head = lambda g: slice(g * RG, (g + 1) * RG)
    diff = pos[:, None] - pos[None, :]
    intra = np.where(diff >= 0, np.exp(log_g[:, None, None] * np.maximum(diff, 0)), 0).astype(f32)
    kv_dec = np.exp((t - pos)[:, None] * log_g[None, :])
    q_dec = np.exp(pos[:, None] * log_g[None, :])
    dec = []
    for g in range(N_RG):
        kd = np.tile(np.repeat(kv_dec[:, head(g)], HALF, axis=1), (1, 2))
        qd = np.repeat(q_dec[:, head(g)], RET_DV, axis=1)
        dec += [kd, qd]
    dec = jnp.asarray(np.stack(dec), F32)
    dtab = jnp.asarray(np.stack([np.concatenate(list(intra[head(g)]), axis=1) for g in range(N_RG)]), F32)

    g_t = np.exp(t * log_g)
    shift = -t * freqs
    stab = [np.broadcast_to(np.repeat(g_t[head(g)], RET_DV)[None, :], (HALF, RGW)) for g in range(N_RG)]
    stab += [np.broadcast_to(np.cos(shift)[:, None], (HALF, RGW)),
             np.broadcast_to(np.sin(shift)[:, None], (HALF, RGW))]
    stab = jnp.asarray(np.stack(stab), F32)

    r4 = np.arange(RG * t)[:, None] // t
    lane = np.arange(RGW)[None, :]
    mask_k = jnp.asarray(r4 == (lane % LANES) // HALF, BF16)
    mask_v = jnp.asarray(r4 == lane // RET_DV, BF16)
    rs = np.arange(LANES)[:, None] // HALF
    mask_s = jnp.asarray(rs == lane // RET_DV, F32)
    seg = jnp.asarray(np.arange(RGW)[:, None] // RET_DV == lane // RET_DV, BF16)

    rowk = np.arange(KEY_PAD)[:, None] < s_keys
    lane_v = np.arange(LANES)[None, :] // HEAD_DIM
    v_keep = jnp.stack([jnp.asarray(rowk & (lane_v == p), BF16) for p in range(2)])
    v_ones = jnp.stack([jnp.asarray(np.broadcast_to(lane_v != p, (KEY_PAD, LANES)), BF16) for p in range(2)])
    return dict(rot=rot, dec=dec, dtab=dtab, stab=stab, mask_k=mask_k, mask_v=mask_v, mask_s=mask_s, seg=seg,
                v_keep=v_keep, v_ones=v_ones)


_TABLE_ORDER = ("rot", "dec", "dtab", "stab", "mask_k", "mask_v", "mask_s", "seg", "v_keep", "v_ones")


def _fill_table(sink_of, fill_ref, t, s_keys):
    col = lax.broadcasted_iota(jnp.int32, (t, KEY_PAD), 1)
    for kv in range(ATTN_KV_HEADS):
        for j in range(GROUP):
            fill_ref[kv, j * t:(j + 1) * t, :] = jnp.where(col == s_keys, sink_of(kv * GROUP + j) * LOG2_E, NEG)


def _attn_scores(q, kwin):
    t = q.shape[0]
    lower = lax.broadcasted_iota(jnp.int32, (t, LANES), 1) < HEAD_DIM
    out = []
    for kv in range(ATTN_KV_HEADS):
        keep = jnp.where(lower if kv == 0 else ~lower, ATTN_SCALE * LOG2_E, 0.0)
        parts = []
        for j in range(2):
            blk = q[:, (2 * kv + j) * LANES:(2 * kv + j + 1) * LANES].astype(F32)
            rolled = pltpu.roll(blk, HEAD_DIM, axis=1)
            lo_head, hi_head = (blk, rolled) if kv == 0 else (rolled, blk)
            parts += [lo_head * keep, hi_head * keep]
        qs = jnp.concatenate(parts, axis=0).astype(BF16)
        out.append(lax.dot_general(qs, kwin, (((1,), (1,)), ((), ())), preferred_element_type=F32))
    return out


def _attn_values(scores, vwin, valid, fill_ref, tb):
    out = []
    for kv, s in enumerate(scores):
        s = jnp.where(valid, s, fill_ref[kv])
        p = jnp.exp2(s - jnp.max(s, axis=-1, keepdims=True))
        vext = vwin * tb["v_keep"][kv] + tb["v_ones"][kv]
        out.append(jnp.dot(p.astype(BF16), vext, preferred_element_type=F32))
    return out


def _attn_normalise(results, t):
    lower = lax.broadcasted_iota(jnp.int32, (t, LANES), 1) < HEAD_DIM
    cols = []
    for kv, res in enumerate(results):
        swapped = pltpu.roll(res, HEAD_DIM, axis=1)
        for j in range(2):
            lo_rows, hi_rows = slice(2 * j * t, (2 * j + 1) * t), slice((2 * j + 1) * t, (2 * j + 2) * t)
            if kv == 0:
                lo, hi = res[lo_rows] / swapped[lo_rows], swapped[hi_rows] / res[hi_rows]
            else:
                lo, hi = swapped[lo_rows] / res[lo_rows], res[hi_rows] / swapped[hi_rows]
            cols.append(jnp.where(lower, lo, hi))
    return jnp.concatenate(cols, axis=-1)


def _ret_scores(zcols, g, tb, row_keep):
    lo = g * RGW
    cq, sq, ck, sk = tb["rot"][0], tb["rot"][1], tb["rot"][2], tb["rot"][3]
    q1 = zcols(OFF_QR + lo, OFF_QR + lo + LANES).astype(F32)
    q2 = zcols(OFF_QR + lo + LANES, OFF_QR + lo + RGW).astype(F32)
    k1 = zcols(OFF_KR + lo, OFF_KR + lo + LANES).astype(F32)
    k2 = zcols(OFF_KR + lo + LANES, OFF_KR + lo + RGW).astype(F32)
    if row_keep is not None:
        k1, k2 = k1 * row_keep, k2 * row_keep
    q_rot = jnp.concatenate([q1 * cq - q2 * sq, q2 * cq + q1 * sq], axis=-1)
    k_rot = jnp.concatenate([k1 * ck - k2 * sk, k2 * ck + k1 * sk], axis=-1)
    qb, kb = q_rot.astype(BF16), k_rot.astype(BF16)
    kdb = (k_rot * tb["dec"][2 * g]).astype(BF16)
    vb = zcols(OFF_VR + lo, OFF_VR + lo + RGW)
    k_bd = jnp.concatenate([kb] * RG, axis=0) * tb["mask_k"][...]
    scores = lax.dot_general(qb, k_bd, (((1,), (1,)), ((), ())), preferred_element_type=F32)
    w_full = lax.dot_general(kdb, vb, (((0,), (0,)), ((), ())), preferred_element_type=F32)
    return qb, vb, scores, w_full


def _ret_outputs(g, qb, vb, scores, w_full, state_ref, tb):
    mask_s = tb["mask_s"][...]
    v_bd = jnp.concatenate([vb] * RG, axis=0) * tb["mask_v"][...]
    intra = jnp.dot((scores * tb["dtab"][g]).astype(BF16), v_bd, preferred_element_type=F32)
    c1, c2 = state_ref[g, 0:HALF, :], state_ref[g, HALF:2 * HALF, :]
    mask_sb = mask_s.astype(BF16)
    s_bd = jnp.concatenate([jnp.concatenate([c.astype(BF16)] * RG, axis=0) * mask_sb for c in (c1, c2)], axis=0)
    cross = jnp.dot(qb, s_bd, preferred_element_type=F32)
    w = []
    for half in range(2):
        wm = w_full[half * LANES:(half + 1) * LANES] * mask_s
        w.append(wm[0:HALF] + wm[HALF:2 * HALF] + wm[2 * HALF:3 * HALF] + wm[3 * HALF:4 * HALF])
    cos_s, sin_s = tb["stab"][N_RG], tb["stab"][N_RG + 1]
    a1 = tb["stab"][g] * c1 + w[0]
    a2 = tb["stab"][g] * c2 + w[1]
    state_ref[g, 0:HALF, :] = a1 * cos_s - a2 * sin_s
    state_ref[g, HALF:2 * HALF, :] = a2 * cos_s + a1 * sin_s
    return intra, cross


def _ret_square_sums(outs, tb):
    parts = []
    for o in outs:
        sq = o * o
        hi = sq.astype(BF16)
        parts += [hi, (sq - hi.astype(F32)).astype(BF16)]
    return jnp.dot(jnp.concatenate(parts, axis=0), tb["seg"][...], preferred_element_type=F32)


def _mixer_stages(zcols, kwin, vwin, valid, fill_ref, state_ref, tb, row_keep, emit_attn, emit_ret):
    attn_scores = _attn_scores(zcols(OFF_QA, OFF_QA + ATTN_WIDTH), kwin())
    ret = [_ret_scores(zcols, g, tb, row_keep) for g in range(N_RG)]
    t = ret[0][0].shape[0]
    yield
    attn_res = _attn_values(attn_scores, vwin(), valid, fill_ref, tb)
    pairs = [_ret_outputs(g, *ret[g], state_ref, tb) for g in range(N_RG)]
    yield
    emit_attn(_attn_normalise(attn_res, t).astype(BF16))
    outs = [intra + cross * tb["dec"][2 * g + 1] for g, (intra, cross) in enumerate(pairs)]
    ssq = _ret_square_sums(outs, tb)
    res = []
    for g, o in enumerate(outs):
        gate = zcols(OFF_GR + g * RGW, OFF_GR + (g + 1) * RGW).astype(F32)
        ss = ssq[2 * g * t:(2 * g + 1) * t] + ssq[(2 * g + 1) * t:(2 * g + 2) * t]
        res.append(o * lax.rsqrt(ss * (1.0 / RET_DV) + EPS) * _silu(gate))
    emit_ret(jnp.concatenate(res, axis=-1).astype(BF16))
    yield


N_STAGES = 3
FIRST_EMIT_STAGE = 2

def _stage_order(n_blocks):
    order = []
    for step in range(n_blocks + N_STAGES - 1):
        order += [step - stage for stage in reversed(range(N_STAGES)) if 0 <= step - stage < n_blocks]
    return order


def _run_blocks(blocks):
    for b in _stage_order(len(blocks)):
        next(blocks[b])


def _state_to_heads(state_ref, out_ref):
    for g in range(N_RG):
        for hh in range(RG):
            out_ref[0, g * RG + hh] = state_ref[g, :, hh * RET_DV:(hh + 1) * RET_DV]


def _prompt_kernel(sinks_ref, z_ref, zmeta_ref, x_ref, *rest, names, layer, tile_chunks, n_g, n_tiles, lead,
                   ff_chunk):
    rest = list(rest)
    take = lambda n: [rest.pop(0) for _ in range(n)]
    p = dict(zip(names, take(len(names))))
    tb = dict(zip(_TABLE_ORDER, take(len(_TABLE_ORDER))))
    o_ref, mixmeta_ref, klast_ref, vlast_ref, sfin_ref = take(5)
    mix_ref, h_ref, xn_ref, kctx_ref, vctx_ref, fill_ref, state_ref = take(7)
    assert not rest
    s = pl.program_id(0)
    g = lax.rem(jnp.minimum(s, n_tiles - 1), n_g)
    ctx = WINDOW_CHUNKS * CHUNK
    rows = tile_chunks * CHUNK
    s_keys = ctx + CHUNK
    col = lax.broadcasted_iota(jnp.int32, (1, KEY_PAD), 1)

    def put_keys(dst, z, n):
        kctx_ref[dst:dst + n, :] = z[:, OFF_KA:OFF_KA + KV_WIDTH]
        vctx_ref[dst:dst + n, :] = z[:, OFF_VA:OFF_VA + KV_WIDTH]

    def block(z_src, out_ref, r0, first_row, row_keep=None):
        rows_c = slice(r0, r0 + CHUNK)
        win = slice(r0, r0 + KEY_PAD)
        valid = (col < s_keys) & (col + (first_row - ctx) >= lead)

        def emit(lo):
            def store(v):
                out_ref[rows_c, lo:lo + v.shape[1]] = v
            return store

        return _mixer_stages(lambda lo, hi: z_src[rows_c, lo:hi], lambda: kctx_ref[win, :], lambda: vctx_ref[win, :],
                             valid, fill_ref, state_ref, tb, row_keep, emit(0), emit(ATTN_WIDTH))

    @pl.when(s == 0)
    def _():
        mix_ref[...] = jnp.zeros(mix_ref.shape, BF16)

    @pl.when(g == 0)
    def _():
        for r in (kctx_ref, vctx_ref):
            r[...] = jnp.zeros(r.shape, BF16)
        state_ref[...] = jnp.zeros(state_ref.shape, F32)
        _fill_table(lambda h: sinks_ref[layer, h], fill_ref, CHUNK, s_keys)
        zm = zmeta_ref[...]
        put_keys(ctx, zm, CHUNK)
        row_keep = (lax.broadcasted_iota(jnp.int32, (CHUNK, 1), 0) >= lead).astype(F32)
        _run_blocks([block(zmeta_ref, mixmeta_ref, 0, 0, row_keep)])
        put_keys(ctx - CHUNK, zm, CHUNK)

    units = _token_units(lambda r: x_ref[r, :], mix_ref, p, o_ref, None, h_ref, xn_ref, ff_chunk)
    n_units = _n_token_units(rows, h_ref.shape[1], ff_chunk, True)
    n_mix_reads = len(_row_halves(rows))

    put_keys(ctx, z_ref[...], rows)
    tile_row = (1 + g * tile_chunks) * CHUNK
    blocks = [block(z_ref, mix_ref, c * CHUNK, tile_row + c * CHUNK) for c in range(tile_chunks)]
    order = _stage_order(len(blocks))
    first_emit = [i for i, b in enumerate(order) if b == 0][FIRST_EMIT_STAGE]
    issued = 0
    for i, b in enumerate(order):
        if i == first_emit:
            while issued < n_mix_reads:
                next(units)
                issued += 1
        next(blocks[b])
        target = max(issued, -(-n_units * (i + 1) // len(order)))
        for _ in range(target - issued):
            next(units)
        issued = target
    assert next(units, None) is None

    for r in (kctx_ref, vctx_ref):
        r[0:ctx, :] = r[rows:rows + ctx, :]

    @pl.when((g == n_g - 1) & (s < n_tiles))
    def _():
        klast_ref[0] = kctx_ref[0:ctx, :].astype(F32)
        vlast_ref[0] = vctx_ref[0:ctx, :].astype(F32)
        _state_to_heads(state_ref, sfin_ref)


def _prompt_step(z, x, sinks, layer, tables, ffn, mix, batch, seq, lead, final=None):
    d = x.shape[1]
    d_ff = ffn[2].shape[0]
    n_main = batch * seq
    n_chunks = seq // CHUNK
    tile_chunks = _largest_divisor(n_chunks, MAX_TILE_CHUNKS)
    n_g = n_chunks // tile_chunks
    n_tiles = batch * n_g
    rows = tile_chunks * CHUNK
    ctx = WINDOW_CHUNKS * CHUNK
    assert rows >= ctx and n_main % CHUNK == 0 and ctx + CHUNK < KEY_PAD
    meta_block0 = n_main // CHUNK
    tabs = [tables[k] for k in _TABLE_ORDER]
    names, w_arrays, w_specs = _weight_args(d, ffn, mix, None, final)
    state_shape = (RET_HEADS, RET_DK, RET_DV)
    ctx_rows = ctx + rows + KEY_PAD - (ctx + CHUNK)
    ctx_buf = pltpu.VMEM((ctx_rows, KV_WIDTH), BF16)
    cur = lambda s: jnp.minimum(s, n_tiles - 1)
    prev = lambda s: jnp.maximum(s - 1, 0)
    seq_of = lambda s: cur(s) // n_g
    return pl.pallas_call(
        functools.partial(_prompt_kernel, names=names, layer=layer, tile_chunks=tile_chunks, n_g=n_g,
                          n_tiles=n_tiles, lead=lead, ff_chunk=_ffn_chunk(d_ff)),
        grid=(n_tiles + 1,),
        in_specs=[pl.BlockSpec(memory_space=pltpu.SMEM),
                  pl.BlockSpec((rows, IN_WIDTH), lambda s: (cur(s), 0)),
                  pl.BlockSpec((CHUNK, IN_WIDTH), lambda s: (meta_block0, 0)),
                  pl.BlockSpec((rows, d), lambda s: (prev(s), 0))]
                 + w_specs + [_resident(t.shape) for t in tabs],
        out_specs=[pl.BlockSpec((rows, d), lambda s: (prev(s), 0)),
                   pl.BlockSpec((CHUNK, MIX_WIDTH), lambda s: (0, 0)),
                   pl.BlockSpec((1, ctx, KV_WIDTH), lambda s: (seq_of(s), 0, 0)),
                   pl.BlockSpec((1, ctx, KV_WIDTH), lambda s: (seq_of(s), 0, 0)),
                   pl.BlockSpec((1,) + state_shape, lambda s: (seq_of(s), 0, 0, 0))],
        out_shape=[jax.ShapeDtypeStruct((n_main, d), F32),
                   jax.ShapeDtypeStruct((CHUNK, MIX_WIDTH), BF16),
                   jax.ShapeDtypeStruct((batch, ctx, KV_WIDTH), F32),
                   jax.ShapeDtypeStruct((batch, ctx, KV_WIDTH), F32),
                   jax.ShapeDtypeStruct((batch,) + state_shape, F32)],
        scratch_shapes=[pltpu.VMEM((rows, MIX_WIDTH), BF16), pltpu.VMEM((rows, d_ff), BF16),
                        pltpu.VMEM((rows, d), BF16), ctx_buf, ctx_buf,
                        pltpu.VMEM((ATTN_KV_HEADS, GROUP * CHUNK, KEY_PAD), F32),
                        pltpu.VMEM((N_RG, RET_DK, RGW), F32)],
        compiler_params=_compiler_params(1),
        name="prompt_step",
    )(sinks, z, z, x, *w_arrays, *tabs)


def _sample_mixer_kernel(sinks_ref, z_ref, ck_ref, cv_ref, s0_ref, *rest, layer, cache_rows):
    tb = dict(zip(_TABLE_ORDER, rest[:len(_TABLE_ORDER)]))
    mix_ref, knew_ref, vnew_ref, snew_ref, fill_ref, state_ref = rest[len(_TABLE_ORDER):]
    z = z_ref[...]
    t = z.shape[0]
    s_keys = cache_rows + t
    pad = jnp.zeros((KEY_PAD - s_keys, KV_WIDTH), F32)
    k_all = jnp.concatenate([ck_ref[0], z[:, OFF_KA:OFF_KA + KV_WIDTH].astype(F32)], axis=0)
    v_all = jnp.concatenate([cv_ref[0], z[:, OFF_VA:OFF_VA + KV_WIDTH].astype(F32)], axis=0)
    windows = [jnp.concatenate([a, pad], axis=0).astype(BF16) for a in (k_all, v_all)]
    _fill_table(lambda h: sinks_ref[layer, h], fill_ref, t, s_keys)
    for g in range(N_RG):
        for hh in range(RG):
            state_ref[g, :, hh * RET_DV:(hh + 1) * RET_DV] = s0_ref[0, g * RG + hh]
    col = lax.broadcasted_iota(jnp.int32, (1, KEY_PAD), 1)

    def emit(lo):
        def store(v):
            mix_ref[:, lo:lo + v.shape[1]] = v
        return store

    _run_blocks([_mixer_stages(lambda lo, hi: z_ref[:, lo:hi], lambda: windows[0], lambda: windows[1], col < s_keys,
                               fill_ref, state_ref, tb, None, emit(0), emit(ATTN_WIDTH))])
    knew_ref[0] = k_all[s_keys - cache_rows:]
    vnew_ref[0] = v_all[s_keys - cache_rows:]
    _state_to_heads(state_ref, snew_ref)


def _sample_mixer(z, sinks, layer, tables, cache_k, cache_v, state, batch, t, row_offset):
    cache_rows = cache_k.shape[2]
    assert row_offset % t == 0 and cache_rows + t < KEY_PAD
    first_block = row_offset // t
    tabs = [tables[k] for k in _TABLE_ORDER]
    state_shape = (RET_HEADS, RET_DK, RET_DV)
    cache_spec = pl.BlockSpec((1, cache_rows, KV_WIDTH), lambda b: (b, 0, 0))
    state_spec = pl.BlockSpec((1,) + state_shape, lambda b: (b, 0, 0, 0))
    cache_in = pl.BlockSpec((None, 1, cache_rows, KV_WIDTH), lambda b: (layer, b, 0, 0))
    state_in = pl.BlockSpec((None, 1) + state_shape, lambda b: (layer, b, 0, 0, 0))
    return pl.pallas_call(
        functools.partial(_sample_mixer_kernel, layer=layer, cache_rows=cache_rows),
        grid=(batch,),
        in_specs=[pl.BlockSpec(memory_space=pltpu.SMEM),
                  pl.BlockSpec((t, IN_WIDTH), lambda b: (first_block + b, 0)),
                  cache_in, cache_in, state_in] + [_resident(x.shape) for x in tabs],
        out_specs=[pl.BlockSpec((t, MIX_WIDTH), lambda b: (b, 0)), cache_spec, cache_spec, state_spec],
        out_shape=[jax.ShapeDtypeStruct((batch * t, MIX_WIDTH), BF16),
                   jax.ShapeDtypeStruct(cache_k.shape[1:], F32),
                   jax.ShapeDtypeStruct(cache_v.shape[1:], F32),
                   jax.ShapeDtypeStruct(state.shape[1:], F32)],
        scratch_shapes=[pltpu.VMEM((ATTN_KV_HEADS, GROUP * t, KEY_PAD), F32),
                        pltpu.VMEM((N_RG, RET_DK, RGW), F32)],
        compiler_params=_compiler_params(1),
        name="sample_mixer",
    )(sinks, z, cache_k, cache_v, state, *tabs)


def kernel(x_prompt, x_sample, cache_swa_k, cache_swa_v, state_ret, meta_tokens, w_in, w_out, attn_sinks,
           ffn1_w_in, ffn1_w_out, ffn2_w_in, ffn2_w_out, norm_ffn1_pre, norm_ffn1_post, norm_mix_pre,
           norm_mix_post, norm_ffn2_pre, norm_ffn2_post, final_norm):
    batch, seq, d = x_prompt.shape
    dec_batch, dec_seq, _ = x_sample.shape
    depth = w_in.shape[0]
    cache_rows = cache_swa_k.shape[2]
    lead = CHUNK - N_META
    n_main, n_meta, n_sample = batch * seq, CHUNK, dec_batch * dec_seq
    tm = _row_tile(n_main)
    n_tail = -(-(n_meta + n_sample) // tm) * tm
    n_pad = n_tail - n_meta - n_sample
    assert w_in.shape[2] == IN_WIDTH and w_out.shape[1] == MIX_WIDTH
    assert seq % CHUNK == 0 and cache_rows == WINDOW_CHUNKS * CHUNK and seq >= cache_rows

    x_tail = jnp.concatenate([jnp.zeros((lead, d), x_prompt.dtype), meta_tokens.astype(x_prompt.dtype),
                              x_sample.reshape(n_sample, d), jnp.zeros((n_pad, d), x_prompt.dtype)], axis=0)
    x = (x_prompt.reshape(n_main, d), x_tail)

    tables_p = _mixer_tables(CHUNK, WINDOW_CHUNKS * CHUNK + CHUNK)
    tables_s = _mixer_tables(dec_seq, cache_rows + dec_seq)
    cache_k = cache_swa_k.reshape(depth, dec_batch, cache_rows, KV_WIDTH)
    cache_v = cache_swa_v.reshape(depth, dec_batch, cache_rows, KV_WIDTH)

    w_proj, w_mix = _projection_weight(w_in), w_out.astype(BF16)
    w1_in, w1_out = ffn1_w_in.astype(BF16), ffn1_w_out.astype(BF16)
    w2_in, w2_out = ffn2_w_in.astype(BF16), ffn2_w_out.astype(BF16)

    gains = {name: g.reshape(depth, 1, d) for name, g in dict(
        f1pre=norm_ffn1_pre, f1post=norm_ffn1_post, mpre=norm_mix_pre, mpost=norm_mix_post, f2pre=norm_ffn2_pre,
        f2post=norm_ffn2_post).items()}

    pk, pv, ps, sk, sv, ss = [], [], [], [], [], []
    for l in range(depth):
        at = lambda stack: _Layer(stack, l)
        ffn1 = (at(gains["f1pre"]), at(w1_in), at(w1_out), at(gains["f1post"]))
        ffn2 = (at(gains["f2pre"]), at(w2_in), at(w2_out), at(gains["f2post"]))
        mix_w = (at(w_mix), at(gains["mpost"]))
        final = final_norm if l == depth - 1 else None
        x1, z = _token_step(x, ffn1, tm, n_out=n_main + n_tail, proj=(at(gains["mpre"]), at(w_proj)))
        x_main, mix_meta, k_p, v_p, s_p = _prompt_step(z, x1, attn_sinks, l, tables_p, ffn2, mix_w, batch, seq, lead,
                                                       final)
        mix_s, k_s, v_s, s_s = _sample_mixer(z, attn_sinks, l, tables_s, cache_k, cache_v, state_ret,
                                             dec_batch, dec_seq, n_main + n_meta)
        mix_tail = jnp.concatenate([mix_meta, mix_s, jnp.zeros((n_pad, MIX_WIDTH), BF16)], axis=0)
        x_tail = _token_step(x1, ffn2, tm, n_out=n_tail, first_tile=n_main // tm, mix=(mix_tail,) + mix_w,
                             final=final)
        x = (x_main, x_tail)
        pk.append(k_p); pv.append(v_p); ps.append(s_p)
        sk.append(k_s); sv.append(v_s); ss.append(s_s)

    y_main, y_tail = x
    y_prompt = y_main.reshape(batch, seq, d)
    y_sample = y_tail[n_meta:n_meta + n_sample].reshape(dec_batch, dec_seq, d)
    kv_shape_p = (depth, batch, cache_rows, ATTN_KV_HEADS, HEAD_DIM)
    kv_shape_s = (depth, dec_batch, cache_rows, ATTN_KV_HEADS, HEAD_DIM)
    return (y_prompt, y_sample,
            jnp.stack(pk).reshape(kv_shape_p), jnp.stack(pv).reshape(kv_shape_p), jnp.stack(ps),
            jnp.stack(sk).reshape(kv_shape_s), jnp.stack(sv).reshape(kv_shape_s), jnp.stack(ss))
```

```python
import functools
import math
from typing import NamedTuple

import jax
import jax.numpy as jnp
import numpy as np
from jax import lax
from jax.experimental import pallas as pl
from jax.experimental.pallas import tpu as pltpu

CHUNK = 64
N_META = 16
WINDOW_CHUNKS = 2
ATTN_HEADS = 8
ATTN_KV_HEADS = 2
HEAD_DIM = 64
GROUP = ATTN_HEADS // ATTN_KV_HEADS
ATTN_SCALE = HEAD_DIM ** -0.5
LOG2_E = math.log2(math.e)
RET_HEADS = 8
RET_DK = 64
RET_DV = 64
ATTN_WIDTH = ATTN_HEADS * HEAD_DIM
KV_WIDTH = ATTN_KV_HEADS * HEAD_DIM
RET_WIDTH = RET_HEADS * RET_DK
ROPE_BASE = 10000.0
EPS = 1e-6
NEG = -1e30

OFF_QA = 0
OFF_KA = OFF_QA + ATTN_WIDTH
OFF_VA = OFF_KA + KV_WIDTH
OFF_QR = OFF_VA + KV_WIDTH
OFF_KR = OFF_QR + RET_WIDTH
OFF_VR = OFF_KR + RET_WIDTH
OFF_GR = OFF_VR + RET_WIDTH
IN_WIDTH = OFF_GR + RET_WIDTH
MIX_WIDTH = ATTN_WIDTH + RET_WIDTH

LANES = 128
F32_SUBLANES = 8
BF16_SUBLANES = 16
HALF = RET_DK // 2
RG = 4
RGW = RG * RET_DK
N_RG = RET_HEADS // RG
KEY_PAD = 256
V7X_VMEM_LIMIT_BYTES = 56 * 1024 * 1024
FF_CHUNK = 256
MAX_ROW_TILE = 512
ROW_SPLIT = 2
MAX_TILE_CHUNKS = 8

BF16 = jnp.bfloat16
F32 = jnp.float32

assert RGW == 2 * LANES and KV_WIDTH == LANES and GROUP == 4 and ATTN_KV_HEADS == 2


def _row_tile(n, cap=MAX_ROW_TILE):
    t = cap
    while t > F32_SUBLANES and n % t:
        t //= 2
    assert n % t == 0, (n, t)
    return t


def _largest_divisor(n, cap):
    return max(d for d in range(1, cap + 1) if n % d == 0)


def _rms(x):
    return x * lax.rsqrt(jnp.mean(x * x, axis=-1, keepdims=True) + EPS)


def _silu(x):
    return x / (1.0 + jnp.exp(-x))


def _compiler_params(n_axes):
    return pltpu.CompilerParams(dimension_semantics=("arbitrary",) * n_axes,
                                vmem_limit_bytes=V7X_VMEM_LIMIT_BYTES)


def _resident(shape):
    return pl.BlockSpec(shape, lambda *_: (0,) * len(shape), pipeline_mode=pl.Buffered(1))


class _Layer(NamedTuple):
    stack: jax.Array
    index: int

    @property
    def shape(self):
        return self.stack.shape[1:]

    @property
    def spec(self):
        index = self.index
        return pl.BlockSpec((None,) + self.shape, lambda *_: (index,) + (0,) * len(self.shape),
                            pipeline_mode=pl.Buffered(1))


def _row_halves(tm):
    n_split = ROW_SPLIT if tm % (ROW_SPLIT * BF16_SUBLANES) == 0 else 1
    return [slice(r * tm // n_split, (r + 1) * tm // n_split) for r in range(n_split)]


def _token_units(x_of, mix_ref, p, o_ref, z_ref, h_ref, xn_ref, ff_chunk):
    tm, d_ff = h_ref.shape
    halves = _row_halves(tm)
    if mix_ref is not None:
        for rows in halves:
            y = jnp.dot(mix_ref[rows, :], p["wmix"][...], preferred_element_type=F32)
            o_ref[rows, :] = x_of(rows) + _rms(y) * p["gmix"][...]
            yield
        resid = lambda rows: o_ref[rows, :]
    else:
        resid = x_of
    xn = jnp.concatenate([(_rms(resid(rows)) * p["gpre"][...]).astype(BF16) for rows in halves], axis=0)
    if xn_ref is not None:
        xn_ref[...] = xn
    for j in range(d_ff // ff_chunk):
        lo = j * ff_chunk
        if xn_ref is not None:
            xn = xn_ref[...]
        gate = jnp.dot(xn, p["win"][:, lo:lo + ff_chunk], preferred_element_type=F32)
        yield
        up = jnp.dot(xn, p["win"][:, d_ff + lo:d_ff + lo + ff_chunk], preferred_element_type=F32)
        h_ref[:, lo:lo + ff_chunk] = (_silu(gate) * up).astype(BF16)
        yield
    ys = []
    for rows in halves:
        ys.append(jnp.dot(h_ref[rows, :], p["wout"][...], preferred_element_type=F32))
        yield
    for rows, y in zip(halves, ys):
        xr = resid(rows) + 0.5 * (_rms(y) * p["gpost"][...])
        if z_ref is not None:
            xn = (_rms(xr) * p["gproj"][...]).astype(BF16)
            z_ref[rows, :] = jnp.dot(xn, p["wproj"][...], preferred_element_type=F32).astype(BF16)
        o_ref[rows, :] = _rms(xr) * p["gfin"][...] if "gfin" in p else xr
        yield


def _n_token_units(tm, d_ff, ff_chunk, has_mix):
    return (3 if has_mix else 2) * len(_row_halves(tm)) + 2 * (d_ff // ff_chunk)


def _token_kernel(*refs, names, ff_chunk, has_mix, has_proj):
    refs = list(refs)
    take = lambda n: [refs.pop(0) for _ in range(n)]
    (x_ref,) = take(1)
    x_of = lambda rows: x_ref[rows, :]
    mix_ref = take(1)[0] if has_mix else None
    p = dict(zip(names, take(len(names))))
    (o_ref,) = take(1)
    z_ref = take(1)[0] if has_proj else None
    (h_ref,) = take(1)
    assert not refs
    for _ in _token_units(x_of, mix_ref, p, o_ref, z_ref, h_ref, None, ff_chunk):
        pass


def _ffn_chunk(d_ff):
    return FF_CHUNK if d_ff % FF_CHUNK == 0 else d_ff


def _weight_args(d, ffn, mix=None, proj=None, final=None):
    g_pre, w_in, w_out, g_post = ffn
    gain = lambda g: g if isinstance(g, _Layer) else g.reshape(1, d)
    named = []
    if mix is not None:
        named += [("wmix", mix[0]), ("gmix", gain(mix[1]))]
    named += [("gpre", gain(g_pre)), ("win", w_in), ("wout", w_out), ("gpost", gain(g_post))]
    if proj is not None:
        named += [("gproj", gain(proj[0])), ("wproj", proj[1])]
    if final is not None:
        named.append(("gfin", gain(final)))
    names, arrays = zip(*named)
    specs = [a.spec if isinstance(a, _Layer) else _resident(a.shape) for a in arrays]
    return names, [a.stack if isinstance(a, _Layer) else a for a in arrays], specs


def _token_step(x, ffn, tm, *, mix=None, proj=None, final=None):
    n_out, d = x.shape
    assert n_out % tm == 0
    d_ff = ffn[2].shape[0]
    args = [x]
    in_specs = [pl.BlockSpec((tm, d), lambda i: (i, 0))]
    if mix is not None:
        args.append(mix[0])
        in_specs.append(pl.BlockSpec((tm, MIX_WIDTH), lambda i: (i, 0)))
    names, w_arrays, w_specs = _weight_args(d, ffn, mix[1:] if mix is not None else None, proj, final)
    out_shape = [jax.ShapeDtypeStruct((n_out, d), F32)]
    out_specs = [pl.BlockSpec((tm, d), lambda i: (i, 0))]
    if proj is not None:
        out_shape.append(jax.ShapeDtypeStruct((n_out, IN_WIDTH), BF16))
        out_specs.append(pl.BlockSpec((tm, IN_WIDTH), lambda i: (i, 0)))
    outs = pl.pallas_call(
        functools.partial(_token_kernel, names=names, ff_chunk=_ffn_chunk(d_ff), has_mix=mix is not None,
                          has_proj=proj is not None),
        grid=(n_out // tm,),
        in_specs=in_specs + w_specs,
        out_specs=out_specs,
        out_shape=out_shape,
        scratch_shapes=[pltpu.VMEM((tm, d_ff), BF16)],
        compiler_params=_compiler_params(1),
        name="token_step",
    )(*args, *w_arrays)
    return outs[0] if len(outs) == 1 else tuple(outs)


def _permute_ret_columns(w):
    lead = w.shape[:-1]
    return jnp.swapaxes(w.reshape(lead + (N_RG, RG, 2, HALF)), -3, -2).reshape(lead + (RET_WIDTH,))


def _projection_weight(w):
    parts = [w[..., :OFF_QR], _permute_ret_columns(w[..., OFF_QR:OFF_KR]),
             _permute_ret_columns(w[..., OFF_KR:OFF_VR]), w[..., OFF_VR:]]
    return jnp.concatenate(parts, axis=-1).astype(BF16)


def _mixer_tables(t, s_keys):
    f32 = np.float32
    log_g = np.log1p(-np.exp2(-5.0 - np.arange(RET_HEADS, dtype=f32))).astype(f32)
    freqs = (ROPE_BASE ** (-np.arange(HALF, dtype=f32) / HALF)).astype(f32)
    pos = np.arange(t, dtype=f32)
    ang = pos[:, None] * freqs[None, :]
    cos, sin = np.tile(np.cos(ang), (1, RG)), np.tile(np.sin(ang), (1, RG))
    qs = f32(RET_DK ** -0.5)
    rot = jnp.asarray(np.stack([cos * qs, sin * qs, cos, sin]), F32)

    head = lambda g: slice(g * RG, (g + 1) * RG)
    diff = pos[:, None] - pos[None, :]
    intra = np.where(diff >= 0, np.exp(log_g[:, None, None] * np.maximum(diff, 0)), 0).astype(f32)
    kv_dec = np.exp((t - pos)[:, None] * log_g[None, :])
    q_dec = np.exp(pos[:, None] * log_g[None, :])
    dec = []
    for g in range(N_RG):
        kd = np.tile(np.repeat(kv_dec[:, head(g)], HALF, axis=1), (1, 2))
        qd = np.repeat(q_dec[:, head(g)], RET_DV, axis=1)
        dec += [kd, qd]
    dec = jnp.asarray(np.stack(dec), F32)
    dtab = jnp.asarray(np.stack([np.concatenate(list(intra[head(g)]), axis=1) for g in range(N_RG)]), F32)

    g_t = np.exp(t * log_g)
    shift = -t * freqs
    stab = [np.broadcast_to(np.repeat(g_t[head(g)], RET_DV)[None, :], (HALF, RGW)) for g in range(N_RG)]
    stab += [np.broadcast_to(np.cos(shift)[:, None], (HALF, RGW)),
             np.broadcast_to(np.sin(shift)[:, None], (HALF, RGW))]
    stab = jnp.asarray(np.stack(stab), F32)

    r4 = np.arange(RG * t)[:, None] // t
    lane = np.arange(RGW)[None, :]
    mask_k = jnp.asarray(r4 == (lane % LANES) // HALF, BF16)
    mask_v = jnp.asarray(r4 == lane // RET_DV, BF16)
    rs = np.arange(LANES)[:, None] // HALF
    mask_s = jnp.asarray(rs == lane // RET_DV, F32)
    seg = jnp.asarray(np.arange(RGW)[:, None] // RET_DV == lane // RET_DV, BF16)

    rowk = np.arange(KEY_PAD)[:, None] < s_keys
    lane_v = np.arange(LANES)[None, :] // HEAD_DIM
    v_keep = jnp.stack([jnp.asarray(rowk & (lane_v == p), BF16) for p in range(2)])
    v_ones = jnp.stack([jnp.asarray(np.broadcast_to(lane_v != p, (KEY_PAD, LANES)), BF16) for p in range(2)])
    return dict(rot=rot, dec=dec, dtab=dtab, stab=stab, mask_k=mask_k, mask_v=mask_v, mask_s=mask_s, seg=seg,
                v_keep=v_keep, v_ones=v_ones)


_TABLE_ORDER = ("rot", "dec", "dtab", "stab", "mask_k", "mask_v", "mask_s", "seg", "v_keep", "v_ones")


def _fill_table(sink_of, fill_ref, t, s_keys):
    col = lax.broadcasted_iota(jnp.int32, (t, KEY_PAD), 1)
    for kv in range(ATTN_KV_HEADS):
        for j in range(GROUP):
            fill_ref[kv, j * t:(j + 1) * t, :] = jnp.where(col == s_keys, sink_of(kv * GROUP + j) * LOG2_E, NEG)


def _attn_scores(q, kwin):
    t = q.shape[0]
    lower = lax.broadcasted_iota(jnp.int32, (t, LANES), 1) < HEAD_DIM
    out = []
    for kv in range(ATTN_KV_HEADS):
        keep = jnp.where(lower if kv == 0 else ~lower, ATTN_SCALE * LOG2_E, 0.0)
        parts = []
        for j in range(2):
            blk = q[:, (2 * kv + j) * LANES:(2 * kv + j + 1) * LANES].astype(F32)
            rolled = pltpu.roll(blk, HEAD_DIM, axis=1)
            lo_head, hi_head = (blk, rolled) if kv == 0 else (rolled, blk)
            parts += [lo_head * keep, hi_head * keep]
        qs = jnp.concatenate(parts, axis=0).astype(BF16)
        out.append(lax.dot_general(qs, kwin, (((1,), (1,)), ((), ())), preferred_element_type=F32))
    return out


def _attn_values(scores, vwin, valid, fill_ref, tb):
    out = []
    for kv, s in enumerate(scores):
        s = jnp.where(valid, s, fill_ref[kv])
        p = jnp.exp2(s - jnp.max(s, axis=-1, keepdims=True))
        vext = vwin * tb["v_keep"][kv] + tb["v_ones"][kv]
        out.append(jnp.dot(p.astype(BF16), vext, preferred_element_type=F32))
    return out


def _attn_normalise(results, t):
    lower = lax.broadcasted_iota(jnp.int32, (t, LANES), 1) < HEAD_DIM
    cols = []
    for kv, res in enumerate(results):
        swapped = pltpu.roll(res, HEAD_DIM, axis=1)
        for j in range(2):
            lo_rows, hi_rows = slice(2 * j * t, (2 * j + 1) * t), slice((2 * j + 1) * t, (2 * j + 2) * t)
            if kv == 0:
                lo, hi = res[lo_rows] / swapped[lo_rows], swapped[hi_rows] / res[hi_rows]
            else:
                lo, hi = swapped[lo_rows] / res[lo_rows], res[hi_rows] / swapped[hi_rows]
            cols.append(jnp.where(lower, lo, hi))
    return jnp.concatenate(cols, axis=-1)


def _ret_scores(zcols, g, tb, row_keep):
    lo = g * RGW
    cq, sq, ck, sk = tb["rot"][0], tb["rot"][1], tb["rot"][2], tb["rot"][3]
    q1 = zcols(OFF_QR + lo, OFF_QR + lo + LANES).astype(F32)
    q2 = zcols(OFF_QR + lo + LANES, OFF_QR + lo + RGW).astype(F32)
    k1 = zcols(OFF_KR + lo, OFF_KR + lo + LANES).astype(F32)
    k2 = zcols(OFF_KR + lo + LANES, OFF_KR + lo + RGW).astype(F32)
    if row_keep is not None:
        k1, k2 = k1 * row_keep, k2 * row_keep
    q_rot = jnp.concatenate([q1 * cq - q2 * sq, q2 * cq + q1 * sq], axis=-1)
    k_rot = jnp.concatenate([k1 * ck - k2 * sk, k2 * ck + k1 * sk], axis=-1)
    qb, kb = q_rot.astype(BF16), k_rot.astype(BF16)
    kdb = (k_rot * tb["dec"][2 * g]).astype(BF16)
    vb = zcols(OFF_VR + lo, OFF_VR + lo + RGW)
    k_bd = jnp.concatenate([kb] * RG, axis=0) * tb["mask_k"][...]
    scores = lax.dot_general(qb, k_bd, (((1,), (1,)), ((), ())), preferred_element_type=F32)
    w_full = lax.dot_general(kdb, vb, (((0,), (0,)), ((), ())), preferred_element_type=F32)
    return qb, vb, scores, w_full


def _ret_outputs(g, qb, vb, scores, w_full, state_ref, tb):
    mask_s = tb["mask_s"][...]
    v_bd = jnp.concatenate([vb] * RG, axis=0) * tb["mask_v"][...]
    intra = jnp.dot((scores * tb["dtab"][g]).astype(BF16), v_bd, preferred_element_type=F32)
    c1, c2 = state_ref[g, 0:HALF, :], state_ref[g, HALF:2 * HALF, :]
    mask_sb = mask_s.astype(BF16)
    s_bd = jnp.concatenate([jnp.concatenate([c.astype(BF16)] * RG, axis=0) * mask_sb for c in (c1, c2)], axis=0)
    cross = jnp.dot(qb, s_bd, preferred_element_type=F32)
    w = []
    for half in range(2):
        wm = w_full[half * LANES:(half + 1) * LANES] * mask_s
        w.append(wm[0:HALF] + wm[HALF:2 * HALF] + wm[2 * HALF:3 * HALF] + wm[3 * HALF:4 * HALF])
    cos_s, sin_s = tb["stab"][N_RG], tb["stab"][N_RG + 1]
    a1 = tb["stab"][g] * c1 + w[0]
    a2 = tb["stab"][g] * c2 + w[1]
    state_ref[g, 0:HALF, :] = a1 * cos_s - a2 * sin_s
    state_ref[g, HALF:2 * HALF, :] = a2 * cos_s + a1 * sin_s
    return intra, cross


def _ret_square_sums(outs, tb):
    parts = []
    for o in outs:
        sq = o * o
        hi = sq.astype(BF16)
        parts += [hi, (sq - hi.astype(F32)).astype(BF16)]
    return jnp.dot(jnp.concatenate(parts, axis=0), tb["seg"][...], preferred_element_type=F32)


def _mixer_stages(zcols, kwin, vwin, valid, fill_ref, state_ref, tb, row_keep, emit_attn, emit_ret):
    attn_scores = _attn_scores(zcols(OFF_QA, OFF_QA + ATTN_WIDTH), kwin())
    ret = [_ret_scores(zcols, g, tb, row_keep) for g in range(N_RG)]
    t = ret[0][0].shape[0]
    yield
    attn_res = _attn_values(attn_scores, vwin(), valid, fill_ref, tb)
    pairs = [_ret_outputs(g, *ret[g], state_ref, tb) for g in range(N_RG)]
    yield
    emit_attn(_attn_normalise(attn_res, t).astype(BF16))
    outs = [intra + cross * tb["dec"][2 * g + 1] for g, (intra, cross) in enumerate(pairs)]
    ssq = _ret_square_sums(outs, tb)
    res = []
    for g, o in enumerate(outs):
        gate = zcols(OFF_GR + g * RGW, OFF_GR + (g + 1) * RGW).astype(F32)
        ss = ssq[2 * g * t:(2 * g + 1) * t] + ssq[(2 * g + 1) * t:(2 * g + 2) * t]
        res.append(o * lax.rsqrt(ss * (1.0 / RET_DV) + EPS) * _silu(gate))
    emit_ret(jnp.concatenate(res, axis=-1).astype(BF16))
    yield


N_STAGES = 3
FIRST_EMIT_STAGE = 2

def _stage_order(n_blocks):
    order = []
    for step in range(n_blocks + N_STAGES - 1):
        order += [step - stage for stage in reversed(range(N_STAGES)) if 0 <= step - stage < n_blocks]
    return order


def _run_blocks(blocks):
    for b in _stage_order(len(blocks)):
        next(blocks[b])


def _state_to_heads(state_ref, out_ref):
    for g in range(N_RG):
        for hh in range(RG):
            out_ref[0, g * RG + hh] = state_ref[g, :, hh * RET_DV:(hh + 1) * RET_DV]


def _prompt_kernel(sinks_ref, z_ref, zmeta_ref, x_ref, *rest, names, layer, tile_chunks, n_g, n_tiles, lead,
                   ff_chunk):
    rest = list(rest)
    take = lambda n: [rest.pop(0) for _ in range(n)]
    p = dict(zip(names, take(len(names))))
    tb = dict(zip(_TABLE_ORDER, take(len(_TABLE_ORDER))))
    o_ref, mixmeta_ref, klast_ref, vlast_ref, sfin_ref = take(5)
    mix_ref, h_ref, xn_ref, kctx_ref, vctx_ref, fill_ref, state_ref = take(7)
    assert not rest
    s = pl.program_id(0)
    g = lax.rem(jnp.minimum(s, n_tiles - 1), n_g)
    ctx = WINDOW_CHUNKS * CHUNK
    rows = tile_chunks * CHUNK
    s_keys = ctx + CHUNK
    col = lax.broadcasted_iota(jnp.int32, (1, KEY_PAD), 1)

    def put_keys(dst, z, n):
        kctx_ref[dst:dst + n, :] = z[:, OFF_KA:OFF_KA + KV_WIDTH]
        vctx_ref[dst:dst + n, :] = z[:, OFF_VA:OFF_VA + KV_WIDTH]

    def block(z_src, out_ref, r0, first_row, row_keep=None):
        rows_c = slice(r0, r0 + CHUNK)
        win = slice(r0, r0 + KEY_PAD)
        valid = (col < s_keys) & (col + (first_row - ctx) >= lead)

        def emit(lo):
            def store(v):
                out_ref[rows_c, lo:lo + v.shape[1]] = v
            return store

        return _mixer_stages(lambda lo, hi: z_src[rows_c, lo:hi], lambda: kctx_ref[win, :], lambda: vctx_ref[win, :],
                             valid, fill_ref, state_ref, tb, row_keep, emit(0), emit(ATTN_WIDTH))

    @pl.when(s == 0)
    def _():
        mix_ref[...] = jnp.zeros(mix_ref.shape, BF16)

    @pl.when(g == 0)
    def _():
        for r in (kctx_ref, vctx_ref):
            r[...] = jnp.zeros(r.shape, BF16)
        state_ref[...] = jnp.zeros(state_ref.shape, F32)
        _fill_table(lambda h: sinks_ref[layer, h], fill_ref, CHUNK, s_keys)
        zm = zmeta_ref[...]
        put_keys(ctx, zm, CHUNK)
        row_keep = (lax.broadcasted_iota(jnp.int32, (CHUNK, 1), 0) >= lead).astype(F32)
        _run_blocks([block(zmeta_ref, mixmeta_ref, 0, 0, row_keep)])
        put_keys(ctx - CHUNK, zm, CHUNK)

    units = _token_units(lambda r: x_ref[r, :], mix_ref, p, o_ref, None, h_ref, xn_ref, ff_chunk)
    n_units = _n_token_units(rows, h_ref.shape[1], ff_chunk, True)
    n_mix_reads = len(_row_halves(rows))

    put_keys(ctx, z_ref[...], rows)
    tile_row = (1 + g * tile_chunks) * CHUNK
    blocks = [block(z_ref, mix_ref, c * CHUNK, tile_row + c * CHUNK) for c in range(tile_chunks)]
    order = _stage_order(len(blocks))
    first_emit = [i for i, b in enumerate(order) if b == 0][FIRST_EMIT_STAGE]
    issued = 0
    for i, b in enumerate(order):
        if i == first_emit:
            while issued < n_mix_reads:
                next(units)
                issued += 1
        next(blocks[b])
        target = max(issued, -(-n_units * (i + 1) // len(order)))
        for _ in range(target - issued):
            next(units)
        issued = target
    assert next(units, None) is None

    for r in (kctx_ref, vctx_ref):
        r[0:ctx, :] = r[rows:rows + ctx, :]

    @pl.when((g == n_g - 1) & (s < n_tiles))
    def _():
        klast_ref[0] = kctx_ref[0:ctx, :].astype(F32)
        vlast_ref[0] = vctx_ref[0:ctx, :].astype(F32)
        _state_to_heads(state_ref, sfin_ref)


def _prompt_step(z, z_tail, x, sinks, layer, tables, ffn, mix, batch, seq, lead, final=None):
    d = x.shape[1]
    d_ff = ffn[2].shape[0]
    n_main = batch * seq
    n_chunks = seq // CHUNK
    tile_chunks = _largest_divisor(n_chunks, MAX_TILE_CHUNKS)
    n_g = n_chunks // tile_chunks
    n_tiles = batch * n_g
    rows = tile_chunks * CHUNK
    ctx = WINDOW_CHUNKS * CHUNK
    assert rows >= ctx and n_main % CHUNK == 0 and ctx + CHUNK < KEY_PAD
    tabs = [tables[k] for k in _TABLE_ORDER]
    names, w_arrays, w_specs = _weight_args(d, ffn, mix, None, final)
    state_shape = (RET_HEADS, RET_DK, RET_DV)
    ctx_rows = ctx + rows + KEY_PAD - (ctx + CHUNK)
    ctx_buf = pltpu.VMEM((ctx_rows, KV_WIDTH), BF16)
    cur = lambda s: jnp.minimum(s, n_tiles - 1)
    prev = lambda s: jnp.maximum(s - 1, 0)
    seq_of = lambda s: cur(s) // n_g
    return pl.pallas_call(
        functools.partial(_prompt_kernel, names=names, layer=layer, tile_chunks=tile_chunks, n_g=n_g,
                          n_tiles=n_tiles, lead=lead, ff_chunk=_ffn_chunk(d_ff)),
        grid=(n_tiles + 1,),
        in_specs=[pl.BlockSpec(memory_space=pltpu.SMEM),
                  pl.BlockSpec((rows, IN_WIDTH), lambda s: (cur(s), 0)),
                  pl.BlockSpec((CHUNK, IN_WIDTH), lambda s: (0, 0)),
                  pl.BlockSpec((rows, d), lambda s: (prev(s), 0))]
                 + w_specs + [_resident(t.shape) for t in tabs],
        out_specs=[pl.BlockSpec((rows, d), lambda s: (prev(s), 0)),
                   pl.BlockSpec((CHUNK, MIX_WIDTH), lambda s: (0, 0)),
                   pl.BlockSpec((1, ctx, KV_WIDTH), lambda s: (seq_of(s), 0, 0)),
                   pl.BlockSpec((1, ctx, KV_WIDTH), lambda s: (seq_of(s), 0, 0)),
                   pl.BlockSpec((1,) + state_shape, lambda s: (seq_of(s), 0, 0, 0))],
        out_shape=[jax.ShapeDtypeStruct((n_main, d), F32),
                   jax.ShapeDtypeStruct((CHUNK, MIX_WIDTH), BF16),
                   jax.ShapeDtypeStruct((batch, ctx, KV_WIDTH), F32),
                   jax.ShapeDtypeStruct((batch, ctx, KV_WIDTH), F32),
                   jax.ShapeDtypeStruct((batch,) + state_shape, F32)],
        scratch_shapes=[pltpu.VMEM((rows, MIX_WIDTH), BF16), pltpu.VMEM((rows, d_ff), BF16),
                        pltpu.VMEM((rows, d), BF16), ctx_buf, ctx_buf,
                        pltpu.VMEM((ATTN_KV_HEADS, GROUP * CHUNK, KEY_PAD), F32),
                        pltpu.VMEM((N_RG, RET_DK, RGW), F32)],
        compiler_params=_compiler_params(1),
        name="prompt_step",
    )(sinks, z, z_tail, x, *w_arrays, *tabs)


def _sample_mixer_kernel(sinks_ref, z_ref, ck_ref, cv_ref, s0_ref, *rest, layer, cache_rows):
    tb = dict(zip(_TABLE_ORDER, rest[:len(_TABLE_ORDER)]))
    mix_ref, knew_ref, vnew_ref, snew_ref, fill_ref, state_ref = rest[len(_TABLE_ORDER):]
    z = z_ref[...]
    t = z.shape[0]
    s_keys = cache_rows + t
    pad = jnp.zeros((KEY_PAD - s_keys, KV_WIDTH), F32)
    k_all = jnp.concatenate([ck_ref[0], z[:, OFF_KA:OFF_KA + KV_WIDTH].astype(F32)], axis=0)
    v_all = jnp.concatenate([cv_ref[0], z[:, OFF_VA:OFF_VA + KV_WIDTH].astype(F32)], axis=0)
    windows = [jnp.concatenate([a, pad], axis=0).astype(BF16) for a in (k_all, v_all)]
    _fill_table(lambda h: sinks_ref[layer, h], fill_ref, t, s_keys)
    for g in range(N_RG):
        for hh in range(RG):
            state_ref[g, :, hh * RET_DV:(hh + 1) * RET_DV] = s0_ref[0, g * RG + hh]
    col = lax.broadcasted_iota(jnp.int32, (1, KEY_PAD), 1)

    def emit(lo):
        def store(v):
            mix_ref[:, lo:lo + v.shape[1]] = v
        return store

    _run_blocks([_mixer_stages(lambda lo, hi: z_ref[:, lo:hi], lambda: windows[0], lambda: windows[1], col < s_keys,
                               fill_ref, state_ref, tb, None, emit(0), emit(ATTN_WIDTH))])
    knew_ref[0] = k_all[s_keys - cache_rows:]
    vnew_ref[0] = v_all[s_keys - cache_rows:]
    _state_to_heads(state_ref, snew_ref)


def _sample_mixer(z, sinks, layer, tables, cache_k, cache_v, state, batch, t, row_offset):
    cache_rows = cache_k.shape[2]
    assert row_offset % t == 0 and cache_rows + t < KEY_PAD
    first_block = row_offset // t
    tabs = [tables[k] for k in _TABLE_ORDER]
    state_shape = (RET_HEADS, RET_DK, RET_DV)
    cache_spec = pl.BlockSpec((1, cache_rows, KV_WIDTH), lambda b: (b, 0, 0))
    state_spec = pl.BlockSpec((1,) + state_shape, lambda b: (b, 0, 0, 0))
    cache_in = pl.BlockSpec((None, 1, cache_rows, KV_WIDTH), lambda b: (layer, b, 0, 0))
    state_in = pl.BlockSpec((None, 1) + state_shape, lambda b: (layer, b, 0, 0, 0))
    return pl.pallas_call(
        functools.partial(_sample_mixer_kernel, layer=layer, cache_rows=cache_rows),
        grid=(batch,),
        in_specs=[pl.BlockSpec(memory_space=pltpu.SMEM),
                  pl.BlockSpec((t, IN_WIDTH), lambda b: (first_block + b, 0)),
                  cache_in, cache_in, state_in] + [_resident(x.shape) for x in tabs],
        out_specs=[pl.BlockSpec((t, MIX_WIDTH), lambda b: (b, 0)), cache_spec, cache_spec, state_spec],
        out_shape=[jax.ShapeDtypeStruct((batch * t, MIX_WIDTH), BF16),
                   jax.ShapeDtypeStruct(cache_k.shape[1:], F32),
                   jax.ShapeDtypeStruct(cache_v.shape[1:], F32),
                   jax.ShapeDtypeStruct(state.shape[1:], F32)],
        scratch_shapes=[pltpu.VMEM((ATTN_KV_HEADS, GROUP * t, KEY_PAD), F32),
                        pltpu.VMEM((N_RG, RET_DK, RGW), F32)],
        compiler_params=_compiler_params(1),
        name="sample_mixer",
    )(sinks, z, cache_k, cache_v, state, *tabs)


def kernel(x_prompt, x_sample, cache_swa_k, cache_swa_v, state_ret, meta_tokens, w_in, w_out, attn_sinks,
           ffn1_w_in, ffn1_w_out, ffn2_w_in, ffn2_w_out, norm_ffn1_pre, norm_ffn1_post, norm_mix_pre,
           norm_mix_post, norm_ffn2_pre, norm_ffn2_post, final_norm):
    batch, seq, d = x_prompt.shape
    dec_batch, dec_seq, _ = x_sample.shape
    depth = w_in.shape[0]
    cache_rows = cache_swa_k.shape[2]
    lead = CHUNK - N_META
    n_main, n_meta, n_sample = batch * seq, CHUNK, dec_batch * dec_seq
    tm = _row_tile(n_main)
    n_tail = -(-(n_meta + n_sample) // tm) * tm
    n_pad = n_tail - n_meta - n_sample
    assert w_in.shape[2] == IN_WIDTH and w_out.shape[1] == MIX_WIDTH
    assert seq % CHUNK == 0 and cache_rows == WINDOW_CHUNKS * CHUNK and seq >= cache_rows

    x_tail = jnp.concatenate([jnp.zeros((lead, d), x_prompt.dtype), meta_tokens.astype(x_prompt.dtype),
                              x_sample.reshape(n_sample, d), jnp.zeros((n_pad, d), x_prompt.dtype)], axis=0)
    x = (x_prompt.reshape(n_main, d), x_tail)

    tables_p = _mixer_tables(CHUNK, WINDOW_CHUNKS * CHUNK + CHUNK)
    tables_s = _mixer_tables(dec_seq, cache_rows + dec_seq)
    cache_k = cache_swa_k.reshape(depth, dec_batch, cache_rows, KV_WIDTH)
    cache_v = cache_swa_v.reshape(depth, dec_batch, cache_rows, KV_WIDTH)

    w_proj, w_mix = _projection_weight(w_in), w_out.astype(BF16)
    w1_in, w1_out = ffn1_w_in.astype(BF16), ffn1_w_out.astype(BF16)
    w2_in, w2_out = ffn2_w_in.astype(BF16), ffn2_w_out.astype(BF16)

    gains = {name: g.reshape(depth, 1, d) for name, g in dict(
        f1pre=norm_ffn1_pre, f1post=norm_ffn1_post, mpre=norm_mix_pre, mpost=norm_mix_post, f2pre=norm_ffn2_pre,
        f2post=norm_ffn2_post).items()}

    pk, pv, ps, sk, sv, ss = [], [], [], [], [], []
    for l in range(depth):
        at = lambda stack: _Layer(stack, l)
        ffn1 = (at(gains["f1pre"]), at(w1_in), at(w1_out), at(gains["f1post"]))
        ffn2 = (at(gains["f2pre"]), at(w2_in), at(w2_out), at(gains["f2post"]))
        mix_w = (at(w_mix), at(gains["mpost"]))
        final = final_norm if l == depth - 1 else None
        proj = (at(gains["mpre"]), at(w_proj))
        x1_main, z_main = _token_step(x[0], ffn1, tm, proj=proj)
        x1_tail, z_tail = _token_step(x[1], ffn1, tm, proj=proj)
        x_main, mix_meta, k_p, v_p, s_p = _prompt_step(z_main, z_tail, x1_main, attn_sinks, l, tables_p, ffn2, mix_w,
                                                       batch, seq, lead, final)
        mix_s, k_s, v_s, s_s = _sample_mixer(z_tail, attn_sinks, l, tables_s, cache_k, cache_v, state_ret,
                                             dec_batch, dec_seq, n_meta)
        mix_tail = jnp.concatenate([mix_meta, mix_s, jnp.zeros((n_pad, MIX_WIDTH), BF16)], axis=0)
        x_tail = _token_step(x1_tail, ffn2, tm, mix=(mix_tail,) + mix_w, final=final)
        x = (x_main, x_tail)
        pk.append(k_p); pv.append(v_p); ps.append(s_p)
        sk.append(k_s); sv.append(v_s); ss.append(s_s)

    y_main, y_tail = x
    y_prompt = y_main.reshape(batch, seq, d)
    y_sample = y_tail[n_meta:n_meta + n_sample].reshape(dec_batch, dec_seq, d)
    kv_shape_p = (depth, batch, cache_rows, ATTN_KV_HEADS, HEAD_DIM)
    kv_shape_s = (depth, dec_batch, cache_rows, ATTN_KV_HEADS, HEAD_DIM)
    return (y_prompt, y_sample,
            jnp.stack(pk).reshape(kv_shape_p), jnp.stack(pv).reshape(kv_shape_p), jnp.stack(ps),
            jnp.stack(sk).reshape(kv_shape_s), jnp.stack(sv).reshape(kv_shape_s), jnp.stack(ss))
```
